```python
import math
import jax, jax.numpy as jnp
from jax import lax
import numpy as np

D_MODEL = 1024
BATCH = 32
SEQ = 256
DEPTH = 4
DEC_BATCH = 8
DEC_SEQ = 1024
PAST_LEN = 256

GRID_W = 64
HEAD_DIM = 64
N_HEADS_A = 8
N_HEADS_B = 8
D_A = N_HEADS_A * HEAD_DIM
D_B = N_HEADS_B * HEAD_DIM
WIN_R = 8
WIN_C = 16
Q_BLK_C = 16
K_BAND_C = 32
N_HEADS_C = 8
DK_C = D_MODEL // N_HEADS_C
DV_C = D_MODEL // N_HEADS_C
D_C = N_HEADS_C * DK_C
D_IN_EVEN = 3 * D_A + 4 * D_B
D_IN_ODD = 5 * D_C
D_FF = ((8 * D_MODEL // 3 + 255) // 256) * 256
RET_CHUNK = 64
HGRN_CHUNK = 32
ROPE_BASE = 10000.0
EPS = 1e-6
MASK_NEG = -1e30
F_MIN = 1e-30
N_EVEN = (DEPTH + 1) // 2
N_ODD = DEPTH // 2

kernel_name = "hybrid_na_retention_hgrn2_diffusion_step"

F32 = jnp.float32


def _rmsnorm(x, g=None):
    xf = x.astype(F32)
    y = xf * lax.rsqrt(jnp.mean(xf * xf, axis=-1, keepdims=True) + EPS)
    if g is not None:
        y = y * g.astype(F32)
    return y.astype(x.dtype)


def _heads(a, dh):
    b, t, _ = a.shape
    return a.reshape(b, t, -1, dh).transpose(0, 2, 1, 3)


def _merge(a):
    b, h, t, d = a.shape
    return a.transpose(0, 2, 1, 3).reshape(b, t, h * d)


def _flip(a):
    return jnp.flip(a, axis=2)


def _modulation(cvec, w_mod, b_mod):
    m = jax.nn.silu(cvec) @ w_mod + b_mod
    return jnp.split(m[..., None, :], 6, axis=-1)


def _axial_rope(x):
    n = x.shape[-2]
    t = jnp.arange(n)
    half = HEAD_DIM // 2
    nf = half // 2
    inv = ROPE_BASE ** (-jnp.arange(nf, dtype=F32) / nf)

    def rot(xa, pos):
        ang = pos.astype(F32)[:, None] * inv[None, :]
        cos, sin = jnp.cos(ang), jnp.sin(ang)
        x1, x2 = xa[..., :nf], xa[..., nf:]
        return jnp.concatenate([x1 * cos - x2 * sin, x1 * sin + x2 * cos], axis=-1)

    xf = x.astype(F32)
    out = jnp.concatenate([rot(xf[..., :half], t // GRID_W), rot(xf[..., half:], t % GRID_W)], axis=-1)
    return out.astype(x.dtype)


def _retention_scan(q, k, v, log_g, s0):
    b, h, t, dk = q.shape
    dv = v.shape[-1]
    c = RET_CHUNK
    n = t // c

    def chunks(a):
        return jnp.moveaxis(a.astype(F32).reshape(b, h, n, c, a.shape[-1]), 2, 0)

    pos = jnp.arange(c, dtype=F32)
    diff = pos[:, None] - pos[None, :]
    dmask = jnp.where(diff >= 0, jnp.exp(log_g[:, None, None] * jnp.maximum(diff, 0.0)), 0.0)
    q_dec = jnp.exp(log_g[:, None] * (pos + 1.0))[:, :, None]
    k_dec = jnp.exp(log_g[:, None] * (c - 1.0 - pos))[:, :, None]
    s_dec = jnp.exp(log_g * c)[:, None, None]

    def step(s, xs):
        qi, ki, vi = xs
        att = jnp.einsum('bhid,bhjd->bhij', qi, ki) * dmask
        o = jnp.einsum('bhij,bhjv->bhiv', att, vi) + jnp.einsum('bhid,bhdv->bhiv', qi * q_dec, s)
        s = s * s_dec + jnp.einsum('bhjd,bhjv->bhdv', ki * k_dec, vi)
        return s, o

    s_fin, o = lax.scan(step, s0.astype(F32), (chunks(q), chunks(k), chunks(v)))
    return jnp.moveaxis(o, 0, 2).reshape(b, h, t, dv), s_fin


def _gla_scan(q, k, v, log_f, s0):
    b, h, t, dk = q.shape
    dv = v.shape[-1]
    c = HGRN_CHUNK
    n = t // c

    def chunks(a):
        return jnp.moveaxis(a.astype(F32).reshape(b, h, n, c, a.shape[-1]), 2, 0)

    causal = jnp.tril(jnp.ones((c, c), dtype=bool))[:, :, None]

    def step(s, xs):
        qi, ki, vi, fi = xs
        bcum = jnp.cumsum(fi, axis=2)
        rel = bcum[:, :, :, None, :] - bcum[:, :, None, :, :]
        dec = jnp.where(causal, jnp.exp(jnp.minimum(rel, 0.0)), 0.0)
        att = jnp.einsum('bhtd,bhsd,bhtsd->bhts', qi, ki, dec)
        o = jnp.einsum('bhts,bhsv->bhtv', att, vi) + jnp.einsum('bhtd,bhdv->bhtv', qi * jnp.exp(bcum), s)
        blast = bcum[:, :, -1:, :]
        s = s * jnp.exp(blast[:, :, 0, :, None]) + jnp.einsum('bhsd,bhsv->bhdv', ki * jnp.exp(blast - bcum), vi)
        return s, o

    s_fin, o = lax.scan(step, s0.astype(F32), (chunks(q), chunks(k), chunks(v), chunks(log_f)))
    return jnp.moveaxis(o, 0, 2).reshape(b, h, t, dv), s_fin


def _even_project(h, w_in):
    z = h @ w_in
    parts = jnp.split(z, [D_A, 2 * D_A, 3 * D_A, 3 * D_A + D_B, 3 * D_A + 2 * D_B, 3 * D_A + 3 * D_B], axis=-1)
    return [_heads(p, HEAD_DIM) for p in parts]


def _attend_context(q, k, v):
    s = jnp.einsum('bhqd,bhkd->bhqk', q, k).astype(F32) * (HEAD_DIM ** -0.5)
    p = jax.nn.softmax(s, axis=-1).astype(v.dtype)
    return jnp.einsum('bhqk,bhkd->bhqd', p, v)


def _neighbourhood_attend(q, k, v, k_ctx, v_ctx, rpb):
    b, h, n, dh = q.shape
    rows = n // GRID_W
    wr = min(WIN_R, rows)
    nblk = GRID_W // Q_BLK_C
    r = jnp.arange(rows)
    row_idx = jnp.clip(r - wr // 2, 0, rows - wr)[:, None] + jnp.arange(wr)[None, :]
    qcol = jnp.arange(GRID_W).reshape(nblk, Q_BLK_C)
    band0 = jnp.clip(qcol[:, 0] - WIN_C // 2, 0, GRID_W - K_BAND_C)
    col_idx = band0[:, None] + jnp.arange(K_BAND_C)[None, :]
    win0 = jnp.clip(qcol - WIN_C // 2, 0, GRID_W - WIN_C)
    valid = (col_idx[:, None, :] >= win0[:, :, None]) & (col_idx[:, None, :] < win0[:, :, None] + WIN_C)
    row_off = row_idx - r[:, None] + (WIN_R - 1)
    col_off = jnp.clip(col_idx[:, None, :] - qcol[:, :, None], 1 - WIN_C, WIN_C - 1) + (WIN_C - 1)
    bias = rpb.astype(F32)[:, row_off[:, None, None, :, None], col_off[None, :, :, None, :]]

    def band(a):
        return a.reshape(b, h, rows, GRID_W, dh)[:, :, row_idx][:, :, :, :, col_idx]

    scale = HEAD_DIM ** -0.5
    qg = q.reshape(b, h, rows, nblk, Q_BLK_C, dh)
    s_loc = jnp.einsum('bhrjqd,bhrwjkd->bhrjqwk', qg, band(k)).astype(F32) * scale + bias[None]
    s_loc = jnp.where(valid[:, :, None, :], s_loc, MASK_NEG)
    s_ctx = jnp.einsum('bhrjqd,bhpd->bhrjqp', qg, k_ctx).astype(F32) * scale
    nl = wr * K_BAND_C
    s = jnp.concatenate([s_loc.reshape(b, h, rows, nblk, Q_BLK_C, nl), s_ctx], axis=-1)
    p = jax.nn.softmax(s, axis=-1).astype(v.dtype)
    p_loc = p[..., :nl].reshape(b, h, rows, nblk, Q_BLK_C, wr, K_BAND_C)
    o = (jnp.einsum('bhrjqwk,bhrwjkd->bhrjqd', p_loc, band(v))
         + jnp.einsum('bhrjqp,bhpd->bhrjqd', p[..., nl:], v_ctx))
    return o.reshape(b, h, n, dh)


def _retention_bidir(q, k, v, g, ret_decay, s0):
    log_g = -jnp.exp(ret_decay.astype(F32))
    k = k * (HEAD_DIM ** -0.5)
    o_f, s_f = _retention_scan(q, k, v, log_g[0], s0[:, 0])
    o_b, s_b = _retention_scan(_flip(q), _flip(k), _flip(v), log_g[1], s0[:, 1])
    o = _rmsnorm(o_f + _flip(o_b)) * jax.nn.silu(g.astype(F32))
    return o.astype(q.dtype), jnp.stack([s_f, s_b], axis=1)


def _hgrn_bidir(h, w_in, w_out, lb, gnorm, s0):
    z = h @ w_in
    q, f_fw, f_bw, i, g = [_heads(p, DK_C) for p in jnp.split(z, 5, axis=-1)]
    q = jax.nn.silu(q.astype(F32)) * (DK_C ** -0.5)

    def gates(fr, lbd):
        lbh = lbd.reshape(N_HEADS_C, 1, DK_C)
        frf = fr.astype(F32)
        f = lbh + (1.0 - lbh) * jax.nn.sigmoid(frf)
        inp = (1.0 - lbh) * jax.nn.sigmoid(-frf)
        return jnp.log(jnp.maximum(f, F_MIN)), inp

    lf_f, k_f = gates(f_fw, lb[0])
    lf_b, k_b = gates(f_bw, lb[1])
    v = i.astype(F32)
    o_f, s_f = _gla_scan(q, k_f, v, lf_f, s0[:, 0])
    o_b, s_b = _gla_scan(_flip(q), _flip(k_b), _flip(v), _flip(lf_b), s0[:, 1])
    o = _rmsnorm(o_f + _flip(o_b), gnorm) * jax.nn.silu(g.astype(F32))
    return _merge(o.astype(h.dtype)) @ w_out, jnp.stack([s_f, s_b], axis=1)


def _swiglu(h, w_in, w_out):
    a, u = jnp.split(h @ w_in, 2, axis=-1)
    return (jax.nn.silu(a) * u) @ w_out


def setup_inputs(seed: int = 0) -> dict:
    key = jax.random.key(seed)
    ks = jax.random.split(key, 20)

    def nrm(k, shape, s):
        return jax.random.normal(k, shape, F32) * s

    ret_base = jnp.log(-jnp.log(1.0 - jnp.power(2.0, -5.0 - jnp.arange(N_HEADS_B, dtype=F32))))
    return {
        "x_prompt": nrm(ks[0], (BATCH, SEQ, D_MODEL), 1.0),
        "x_sample": nrm(ks[1], (DEC_BATCH, DEC_SEQ, D_MODEL), 1.0),
        "cache_kv": nrm(ks[2], (DEC_BATCH, N_EVEN, 2, N_HEADS_A, PAST_LEN, HEAD_DIM), 1.0),
        "state_ret": nrm(ks[3], (DEC_BATCH, N_EVEN, 2, N_HEADS_B, HEAD_DIM, HEAD_DIM), 0.5),
        "state_hgrn": nrm(ks[4], (DEC_BATCH, N_ODD, 2, N_HEADS_C, DK_C, DV_C), 0.5),
        "c": nrm(ks[5], (DEC_BATCH, D_MODEL), 1.0),
        "c_ctx": nrm(ks[6], (D_MODEL,), 1.0),
        "w_mod": nrm(ks[7], (DEPTH, D_MODEL, 6 * D_MODEL), 0.5 * D_MODEL ** -0.5),
        "b_mod": nrm(ks[8], (DEPTH, 6 * D_MODEL), 0.02),
        "norm_g": 1.0 + nrm(ks[9], (DEPTH, 4, D_MODEL), 0.02),
        "w_in_even": nrm(ks[10], (N_EVEN, D_MODEL, D_IN_EVEN), D_MODEL ** -0.5),
        "w_out_even": nrm(ks[11], (N_EVEN, D_A + D_B, D_MODEL), (D_A + D_B) ** -0.5),
        "rpb": nrm(ks[12], (N_EVEN, N_HEADS_A, 2 * WIN_R - 1, 2 * WIN_C - 1), 0.1),
        "ret_decay": ret_base + nrm(ks[13], (N_EVEN, 2, N_HEADS_B), 0.1),
        "w_in_odd": nrm(ks[14], (N_ODD, D_MODEL, D_IN_ODD), D_MODEL ** -0.5),
        "w_out_odd": nrm(ks[15], (N_ODD, D_C, D_MODEL), D_C ** -0.5),
        "hgrn_lb": nrm(ks[16], (N_ODD, 2, D_C), 1.0),
        "hgrn_gnorm": 1.0 + nrm(ks[17], (N_ODD, DV_C), 0.02),
        "w_ffn_in": nrm(ks[18], (DEPTH, D_MODEL, 2 * D_FF), D_MODEL ** -0.5),
        "w_ffn_out": nrm(ks[19], (DEPTH, D_FF, D_MODEL), D_FF ** -0.5),
    }


def reference(x_prompt, x_sample, cache_kv, state_ret, state_hgrn, c, c_ctx, w_mod, b_mod, norm_g,
              w_in_even, w_out_even, rpb, ret_decay, w_in_odd, w_out_odd, hgrn_lb, hgrn_gnorm,
              w_ffn_in, w_ffn_out):
    p_lb = jax.nn.softmax(hgrn_lb.astype(F32), axis=0)
    lower = jnp.clip(jnp.cumsum(p_lb, axis=0) - p_lb[0], 0.0, 1.0)
    bp = x_prompt.shape[0]
    xp, xs = x_prompt, x_sample
    kv_states, ret_states, hg_states = [], [], []
    for l in range(DEPTH):
        sh1p, sc1p, gt1p, sh2p, sc2p, gt2p = _modulation(c_ctx, w_mod[l], b_mod[l])
        sh1s, sc1s, gt1s, sh2s, sc2s, gt2s = _modulation(c, w_mod[l], b_mod[l])
        hp = _rmsnorm(xp, norm_g[l, 0]) * (1.0 + sc1p) + sh1p
        hs = _rmsnorm(xs, norm_g[l, 0]) * (1.0 + sc1s) + sh1s
        if l % 2 == 0:
            e = l // 2
            qa, ka, va, qb, kb, vb, gb = _even_project(hp, w_in_even[e])
            oa = _attend_context(qa, ka, va)
            zero_ret = jnp.zeros((bp, 2, N_HEADS_B, HEAD_DIM, HEAD_DIM), F32)
            ob, s_ret = _retention_bidir(qb, kb, vb, gb, ret_decay[e], zero_ret)
            mp = jnp.concatenate([_merge(oa), _merge(ob)], axis=-1) @ w_out_even[e]
            kv_states.append(jnp.stack([ka, va], axis=1))
            ret_states.append(s_ret)
            qa, ka, va, qb, kb, vb, gb = _even_project(hs, w_in_even[e])
            oa = _neighbourhood_attend(qa, ka, va, cache_kv[:, e, 0], cache_kv[:, e, 1], rpb[e])
            ob, _ = _retention_bidir(_axial_rope(qb), _axial_rope(kb), vb, gb, ret_decay[e], state_ret[:, e])
            ms = jnp.concatenate([_merge(oa), _merge(ob)], axis=-1) @ w_out_even[e]
        else:
            o_i = l // 2
            zero_hg = jnp.zeros((bp, 2, N_HEADS_C, DK_C, DV_C), F32)
            mp, s_hg = _hgrn_bidir(hp, w_in_odd[o_i], w_out_odd[o_i], lower[o_i], hgrn_gnorm[o_i], zero_hg)
            hg_states.append(s_hg)
            ms, _ = _hgrn_bidir(hs, w_in_odd[o_i], w_out_odd[o_i], lower[o_i], hgrn_gnorm[o_i], state_hgrn[:, o_i])
        xp = xp + gt1p * _rmsnorm(mp, norm_g[l, 1])
        xs = xs + gt1s * _rmsnorm(ms, norm_g[l, 1])
        hp = _rmsnorm(xp, norm_g[l, 2]) * (1.0 + sc2p) + sh2p
        hs = _rmsnorm(xs, norm_g[l, 2]) * (1.0 + sc2s) + sh2s
        xp = xp + gt2p * _rmsnorm(_swiglu(hp, w_ffn_in[l], w_ffn_out[l]), norm_g[l, 3])
        xs = xs + gt2s * _rmsnorm(_swiglu(hs, w_ffn_in[l], w_ffn_out[l]), norm_g[l, 3])
    new_cache_kv = jnp.stack(kv_states, axis=1).astype(x_prompt.dtype)
    new_state_ret = jnp.stack(ret_states, axis=1).astype(x_prompt.dtype)
    new_state_hgrn = jnp.stack(hg_states, axis=1).astype(x_prompt.dtype)
    return (xp, xs, new_cache_kv, new_state_ret, new_state_hgrn)
```

```python
import functools

import numpy as np
import jax
import jax.numpy as jnp
from jax import lax
from jax.experimental import pallas as pl
from jax.experimental.pallas import tpu as pltpu

F32 = jnp.float32
BF16 = jnp.bfloat16

D_MODEL = 1024
DEPTH = 4
GRID_W = 64
HEAD_DIM = 64
N_HEADS_A = 8
N_HEADS_B = 8
D_A = N_HEADS_A * HEAD_DIM
D_B = N_HEADS_B * HEAD_DIM
WIN_R = 8
WIN_C = 16
N_HEADS_C = 8
DK_C = D_MODEL // N_HEADS_C
D_FF = ((8 * D_MODEL // 3 + 255) // 256) * 256
HGRN_CHUNK = 32
ROPE_BASE = 10000.0
EPS = 1e-6
MASK_NEG = -1e30
F_MIN = 1e-30

GROUP_ROWS = 1024
MOD_ROWS = 8
LANES = 128
MIB = 1024 * 1024

SH1, SC1, GT1, SH2, SC2, GT2 = range(6)

NA_QROWS = 4
NA_KROWS = 12
NA_KSTART = (0, 0, 4, 4)


def _params(sem, vmem_mib):
    return pltpu.CompilerParams(dimension_semantics=sem, vmem_limit_bytes=vmem_mib * MIB)


def _sigmoid(x):
    return 1.0 / (1.0 + jnp.exp(-x))


def _silu(x):
    return x * _sigmoid(x)


def _dot(a, b):
    return jnp.dot(a, b, preferred_element_type=F32)


def _dot_nt(a, b):
    return lax.dot_general(a, b, (((1,), (1,)), ((), ())), preferred_element_type=F32)


def _dot_tn(a, b):
    return lax.dot_general(a, b, (((0,), (0,)), ((), ())), preferred_element_type=F32)


def _mod_kernel(c_ref, w_ref, b_ref, o_ref):
    s = _silu(c_ref[...]).astype(BF16)
    o_ref[...] = _dot(s, w_ref[...].astype(BF16)) + b_ref[...]


def _modulation(cvec, w_mod, b_mod):
    rows = cvec.shape[0]
    n = w_mod.shape[-1]
    tn = 1536
    return pl.pallas_call(
        _mod_kernel,
        grid=(DEPTH, n // tn),
        in_specs=[
            pl.BlockSpec((rows, D_MODEL), lambda l, j: (0, 0)),
            pl.BlockSpec((None, D_MODEL, tn), lambda l, j: (l, 0, j)),
            pl.BlockSpec((None, 1, tn), lambda l, j: (l, 0, j)),
        ],
        out_specs=pl.BlockSpec((None, rows, tn), lambda l, j: (l, 0, j)),
        out_shape=jax.ShapeDtypeStruct((DEPTH, rows, n), F32),
        compiler_params=_params(("arbitrary", "arbitrary"), 32),
        name="modulation",
    )(cvec, w_mod, b_mod.reshape(DEPTH, 1, n))


def _norm_mod(x, g, mod, sh_row, sc_row):
    y = x * lax.rsqrt(jnp.mean(x * x, axis=-1, keepdims=True) + EPS) * g
    return y * (1.0 + mod[sc_row:sc_row + 1]) + mod[sh_row:sh_row + 1]


def _in_proj_kernel(x_ref, g_ref, mod_ref, w_ref, o_ref, h_ref):
    @pl.when(pl.program_id(1) == 0)
    def _():
        h_ref[...] = _norm_mod(x_ref[...], g_ref[...], mod_ref[...], SH1, SC1).astype(BF16)

    o_ref[...] = _dot(h_ref[...], w_ref[...])


def _in_proj(x, g, mod, w):
    m, n = x.shape[0], w.shape[1]
    tm, tn = 1024, 512
    return pl.pallas_call(
        _in_proj_kernel,
        grid=(m // tm, n // tn),
        in_specs=[
            pl.BlockSpec((tm, D_MODEL), lambda i, j: (i, 0)),
            pl.BlockSpec((1, D_MODEL), lambda i, j: (0, 0)),
            pl.BlockSpec((None, MOD_ROWS, D_MODEL), lambda i, j: (i * tm // GROUP_ROWS, 0, 0)),
            pl.BlockSpec((D_MODEL, tn), lambda i, j: (0, j)),
        ],
        out_specs=pl.BlockSpec((tm, tn), lambda i, j: (i, j)),
        out_shape=jax.ShapeDtypeStruct((m, n), F32),
        scratch_shapes=[pltpu.VMEM((tm, D_MODEL), BF16)],
        compiler_params=_params(("parallel", "arbitrary"), 40),
        name="in_proj",
    )(x, g, mod, w)


def _out_proj_kernel(*refs, n_in):
    a_refs, w_refs = refs[:n_in], refs[n_in:2 * n_in]
    x_ref, g_ref, mod_ref, o_ref = refs[2 * n_in:]
    y = _dot(a_refs[0][...], w_refs[0][...])
    for a_ref, w_ref in zip(a_refs[1:], w_refs[1:]):
        y = y + _dot(a_ref[...], w_ref[...])
    yn = y * lax.rsqrt(jnp.mean(y * y, axis=-1, keepdims=True) + EPS) * g_ref[...]
    o_ref[...] = x_ref[...] + mod_ref[GT1:GT1 + 1, :] * yn


def _out_proj(acts, w, x, g, mod):
    m = x.shape[0]
    tm = 1024
    n_in = len(acts)
    ks = [a.shape[1] for a in acts]
    assert sum(ks) == w.shape[0] and len(set(ks)) == 1
    in_specs = [pl.BlockSpec((tm, k), lambda i: (i, 0)) for k in ks]
    in_specs += [pl.BlockSpec((ks[0], D_MODEL), lambda i, p=p: (p, 0)) for p in range(n_in)]
    in_specs += [
        pl.BlockSpec((tm, D_MODEL), lambda i: (i, 0)),
        pl.BlockSpec((1, D_MODEL), lambda i: (0, 0)),
        pl.BlockSpec((None, MOD_ROWS, D_MODEL), lambda i: (i * tm // GROUP_ROWS, 0, 0)),
    ]
    return pl.pallas_call(
        functools.partial(_out_proj_kernel, n_in=n_in),
        grid=(m // tm,),
        in_specs=in_specs,
        out_specs=pl.BlockSpec((tm, D_MODEL), lambda i: (i, 0)),
        out_shape=jax.ShapeDtypeStruct((m, D_MODEL), F32),
        compiler_params=_params(("parallel",), 48),
        name="out_proj",
    )(*acts, *([w] * n_in), x, g, mod)


def _ffn_kernel(x_ref, g2_ref, g3_ref, mod_ref, wa_ref, wu_ref, wo_ref, o_ref, h_ref, acc_ref):
    f = pl.program_id(1)

    @pl.when(f == 0)
    def _():
        h_ref[...] = _norm_mod(x_ref[...], g2_ref[...], mod_ref[...], SH2, SC2).astype(BF16)
        acc_ref[...] = jnp.zeros_like(acc_ref)

    h = h_ref[...]
    a = _dot(h, wa_ref[...])
    u = _dot(h, wu_ref[...])
    acc_ref[...] += _dot((_silu(a) * u).astype(BF16), wo_ref[...])

    @pl.when(f == pl.num_programs(1) - 1)
    def _():
        y = acc_ref[...]
        yn = y * lax.rsqrt(jnp.mean(y * y, axis=-1, keepdims=True) + EPS) * g3_ref[...]
        o_ref[...] = x_ref[...] + mod_ref[GT2:GT2 + 1, :] * yn


def _ffn(x, g2, g3, mod, w_in, w_out):
    m = x.shape[0]
    tm, tf = 1024, 256
    nf = D_FF // tf
    return pl.pallas_call(
        _ffn_kernel,
        grid=(m // tm, nf),
        in_specs=[
            pl.BlockSpec((tm, D_MODEL), lambda i, f: (i, 0)),
            pl.BlockSpec((1, D_MODEL), lambda i, f: (0, 0)),
            pl.BlockSpec((1, D_MODEL), lambda i, f: (0, 0)),
            pl.BlockSpec((None, MOD_ROWS, D_MODEL), lambda i, f: (i * tm // GROUP_ROWS, 0, 0)),
            pl.BlockSpec((D_MODEL, tf), lambda i, f: (0, f)),
            pl.BlockSpec((D_MODEL, tf), lambda i, f: (0, nf + f)),
            pl.BlockSpec((tf, D_MODEL), lambda i, f: (f, 0)),
        ],
        out_specs=pl.BlockSpec((tm, D_MODEL), lambda i, f: (i, 0)),
        out_shape=jax.ShapeDtypeStruct((m, D_MODEL), F32),
        scratch_shapes=[pltpu.VMEM((tm, D_MODEL), BF16), pltpu.VMEM((tm, D_MODEL), F32)],
        compiler_params=_params(("parallel", "arbitrary"), 48),
        name="ffn",
    )(x, g2, g3, mod, w_in, w_in, w_out)


def _ctx_attn_kernel(q_ref, k_ref, v_ref, o_ref, kv_ref):
    q2, k2, v2 = q_ref[...], k_ref[...], v_ref[...]
    outs = []
    for hh in range(LANES // HEAD_DIM):
        hs = slice(hh * HEAD_DIM, (hh + 1) * HEAD_DIM)
        q, k, v = q2[:, hs], k2[:, hs], v2[:, hs]
        kv_ref[0, hh] = k
        kv_ref[1, hh] = v
        s = _dot_nt(q.astype(BF16), k.astype(BF16)) * (HEAD_DIM ** -0.5)
        p = jnp.exp(s - jnp.max(s, axis=-1, keepdims=True))
        l = jnp.sum(p, axis=-1, keepdims=True)
        outs.append(_dot(p.astype(BF16), v.astype(BF16)) / l)
    o_ref[...] = jnp.concatenate(outs, axis=-1).astype(o_ref.dtype)


def _ctx_attn(z, n_req, t, n_rows):
    npair = D_A // LANES
    blk = lambda p: pl.BlockSpec((t, LANES), lambda b, j, p=p: (b, p * npair + j))
    return pl.pallas_call(
        _ctx_attn_kernel,
        grid=(n_req, npair),
        in_specs=[blk(0), blk(1), blk(2)],
        out_specs=[
            pl.BlockSpec((t, LANES), lambda b, j: (b, j)),
            pl.BlockSpec((None, 2, LANES // HEAD_DIM, t, HEAD_DIM), lambda b, j: (b, 0, j, 0, 0)),
        ],
        out_shape=[
            jax.ShapeDtypeStruct((n_rows, D_A), BF16),
            jax.ShapeDtypeStruct((n_req, 2, N_HEADS_A, t, HEAD_DIM), F32),
        ],
        compiler_params=_params(("parallel", "parallel"), 32),
        name="ctx_attn",
    )(z, z, z)


def _na_bias_tables(rpb):
    rows = 16
    qc = np.arange(GRID_W)[:, None]
    kc = np.arange(GRID_W)[None, :]
    win0 = np.clip(qc - WIN_C // 2, 0, GRID_W - WIN_C)
    col_valid = (kc >= win0) & (kc < win0 + WIN_C)
    col_off = np.clip(kc - qc, 1 - WIN_C, WIN_C - 1) + (WIN_C - 1)
    n_ro = 2 * WIN_R - 1
    tiles = jnp.where(col_valid[None, None, None], rpb.astype(F32)[:, :, :, col_off], MASK_NEG)
    tiles = jnp.concatenate([tiles, jnp.full_like(tiles[:, :, :1], MASK_NEG)], axis=2)
    ngrp = rows // NA_QROWS
    idx = np.full((ngrp, NA_QROWS, NA_KROWS), n_ro, np.int32)
    for g in range(ngrp):
        for a in range(NA_QROWS):
            r = g * NA_QROWS + a
            row0 = min(max(r - WIN_R // 2, 0), rows - WIN_R)
            for w in range(NA_KROWS):
                kr = NA_KSTART[g] + w
                if row0 <= kr < row0 + WIN_R:
                    idx[g, a, w] = kr - r + (WIN_R - 1)
    big = jnp.take(tiles, jnp.asarray(idx.reshape(-1)), axis=2)
    e, h = rpb.shape[0], rpb.shape[1]
    big = big.reshape(e, h, ngrp, NA_QROWS, NA_KROWS, GRID_W, GRID_W)
    big = big.transpose(0, 1, 2, 3, 5, 4, 6)
    return big.reshape(e, h, ngrp, NA_QROWS * GRID_W, NA_KROWS * GRID_W)


def _na_kernel(q_ref, k_ref, v_ref, ckv_ref, bias_ref, prev_ref, o_ref):
    del prev_ref
    q2, k2, v2 = q_ref[...], k_ref[...], v_ref[...]
    scale = HEAD_DIM ** -0.5
    tq = NA_QROWS * GRID_W
    tk = NA_KROWS * GRID_W
    outs = []
    for hh in range(LANES // HEAD_DIM):
        hs = slice(hh * HEAD_DIM, (hh + 1) * HEAD_DIM)
        q = q2[:, hs].astype(BF16)
        k = k2[:, hs].astype(BF16)
        v = v2[:, hs].astype(BF16)
        kc = ckv_ref[0, hh].astype(BF16)
        vc = ckv_ref[1, hh].astype(BF16)
        rows = []
        for g in range(len(NA_KSTART)):
            qg = q[g * tq:(g + 1) * tq]
            k0 = NA_KSTART[g] * GRID_W
            s_loc = _dot_nt(qg, k[k0:k0 + tk]) * scale + bias_ref[hh, g]
            s_ctx = _dot_nt(qg, kc) * scale
            m = jnp.maximum(jnp.max(s_loc, axis=-1, keepdims=True), jnp.max(s_ctx, axis=-1, keepdims=True))
            p_loc = jnp.exp(s_loc - m)
            p_ctx = jnp.exp(s_ctx - m)
            l = jnp.sum(p_loc, axis=-1, keepdims=True) + jnp.sum(p_ctx, axis=-1, keepdims=True)
            o = _dot(p_loc.astype(BF16), v[k0:k0 + tk]) + _dot(p_ctx.astype(BF16), vc)
            rows.append(o / l)
        outs.append(jnp.concatenate(rows, axis=0))
    o_ref[...] = jnp.concatenate(outs, axis=-1).astype(o_ref.dtype)


def _na_attn(z, cache_kv, e, bias, prev, n_req, t, row_blk0):
    npair = D_A // LANES
    hp = LANES // HEAD_DIM
    past = cache_kv.shape[-2]
    blk = lambda p: pl.BlockSpec((t, LANES), lambda j, b, p=p: (row_blk0 + b, p * npair + j))
    return pl.pallas_call(
        _na_kernel,
        grid=(npair, n_req),
        in_specs=[
            blk(0), blk(1), blk(2),
            pl.BlockSpec((None, None, 2, hp, past, HEAD_DIM), lambda j, b: (b, e, 0, j, 0, 0)),
            pl.BlockSpec((None, hp) + bias.shape[2:], lambda j, b: (e, j, 0, 0, 0)),
            pl.BlockSpec(memory_space=pl.ANY),
        ],
        out_specs=pl.BlockSpec((t, LANES), lambda j, b: (row_blk0 + b, j)),
        out_shape=jax.ShapeDtypeStruct(prev.shape, prev.dtype),
        input_output_aliases={5: 0},
        compiler_params=_params(("parallel", "parallel"), 56),
        name="na_attn",
    )(z, z, z, cache_kv, bias, prev)


def _rope_tables(t):
    half = HEAD_DIM // 2
    nf = half // 2
    inv = ROPE_BASE ** (-np.arange(nf, dtype=np.float32) / nf)
    pos = np.arange(t)
    ang_r = (pos // GRID_W).astype(np.float32)[:, None] * inv[None, :]
    ang_c = (pos % GRID_W).astype(np.float32)[:, None] * inv[None, :]
    ang_r, ang_c = jnp.asarray(ang_r), jnp.asarray(ang_c)
    cr, sr, cc, sc = jnp.cos(ang_r), jnp.sin(ang_r), jnp.cos(ang_c), jnp.sin(ang_c)
    cos = jnp.concatenate([cr, cr, cc, cc], axis=-1)
    sin = jnp.concatenate([-sr, sr, -sc, sc], axis=-1)
    reps = LANES // HEAD_DIM
    return jnp.tile(cos, (1, reps)), jnp.tile(sin, (1, reps))


def _rope(x, cos, sin):
    nf = HEAD_DIM // 4
    lane = lax.broadcasted_iota(jnp.int32, x.shape, 1)
    partner = jnp.where(lane % (2 * nf) < nf, pltpu.roll(x, LANES - nf, axis=1), pltpu.roll(x, nf, axis=1))
    return x * cos + partner * sin


def _ret_kernel(*refs, t, rope, with_s0, emit_state):
    it = iter(refs)
    dec_ref, q_ref, k_ref, v_ref, g_ref = (next(it) for _ in range(5))
    cos_ref = sin_ref = s0_ref = st_ref = None
    if rope:
        cos_ref, sin_ref = next(it), next(it)
    if with_s0:
        s0_ref = next(it)
    if not emit_state:
        next(it)
    o_ref = next(it)
    if emit_state:
        st_ref = next(it)

    j = pl.program_id(1)
    q2, k2, v2, g2 = q_ref[...], k_ref[...] * (HEAD_DIM ** -0.5), v_ref[...], g_ref[...]
    if rope:
        q2 = _rope(q2, cos_ref[...], sin_ref[...])
        k2 = _rope(k2, cos_ref[...], sin_ref[...])
    n_i = lax.broadcasted_iota(jnp.int32, (t, t), 0)
    m_i = lax.broadcasted_iota(jnp.int32, (t, t), 1)
    diff = (n_i - m_i).astype(F32)
    pos = lax.broadcasted_iota(jnp.int32, (t, 1), 0).astype(F32)
    outs = []
    for hh in range(LANES // HEAD_DIM):
        hs = slice(hh * HEAD_DIM, (hh + 1) * HEAD_DIM)
        head = j * (LANES // HEAD_DIM) + hh
        lg_f = -jnp.exp(jnp.full((1, 1), dec_ref[0, head], F32))
        lg_b = -jnp.exp(jnp.full((1, 1), dec_ref[1, head], F32))
        q, k, v = q2[:, hs], k2[:, hs], v2[:, hs]
        qb, kb, vb = q.astype(BF16), k.astype(BF16), v.astype(BF16)
        dmat = (jnp.where(diff >= 0, jnp.exp(lg_f * jnp.maximum(diff, 0.0)), 0.0)
                + jnp.where(diff <= 0, jnp.exp(lg_b * jnp.maximum(-diff, 0.0)), 0.0))
        att = _dot_nt(qb, kb) * dmat
        o = _dot(att.astype(BF16), vb)
        if with_s0:
            qs = jnp.concatenate([q * jnp.exp(lg_f * (pos + 1.0)), q * jnp.exp(lg_b * (t - pos))], axis=-1)
            s0 = jnp.concatenate([s0_ref[0, hh], s0_ref[1, hh]], axis=0)
            o = o + _dot(qs.astype(BF16), s0.astype(BF16))
        if emit_state:
            kf = k * jnp.exp(lg_f * (t - 1.0 - pos))
            kr = k * jnp.exp(lg_b * pos)
            st_ref[0, hh] = _dot_tn(kf.astype(BF16), vb)
            st_ref[1, hh] = _dot_tn(kr.astype(BF16), vb)
        on = o * lax.rsqrt(jnp.mean(o * o, axis=-1, keepdims=True) + EPS)
        outs.append(on * _silu(g2[:, hs]))
    o_ref[...] = jnp.concatenate(outs, axis=-1).astype(o_ref.dtype)


def _retention(z, ret_decay, n_req, t, row_blk0, n_rows, *, rope_tabs=None, s0=None, e=0, prev=None):
    npair = D_B // LANES
    hp = LANES // HEAD_DIM
    sec0 = 3 * D_A // LANES
    blk = lambda p: pl.BlockSpec((t, LANES), lambda b, j, p=p: (row_blk0 + b, sec0 + p * npair + j))
    in_specs = [pl.BlockSpec(memory_space=pltpu.SMEM), blk(0), blk(1), blk(2), blk(3)]
    args = [ret_decay, z, z, z, z]
    if rope_tabs is not None:
        in_specs += [pl.BlockSpec((t, LANES), lambda b, j: (0, 0))] * 2
        args += list(rope_tabs)
    if s0 is not None:
        in_specs.append(pl.BlockSpec((None, None, 2, hp, HEAD_DIM, HEAD_DIM), lambda b, j: (b, e, 0, j, 0, 0)))
        args.append(s0)
    emit_state = prev is None
    o_spec = pl.BlockSpec((t, LANES), lambda b, j: (row_blk0 + b, j))
    o_shape = jax.ShapeDtypeStruct((n_rows, D_B), BF16)
    if emit_state:
        out_specs = [o_spec, pl.BlockSpec((None, 2, hp, HEAD_DIM, HEAD_DIM), lambda b, j: (b, 0, j, 0, 0))]
        out_shape = [o_shape, jax.ShapeDtypeStruct((n_req, 2, N_HEADS_B, HEAD_DIM, HEAD_DIM), F32)]
        aliases = {}
    else:
        in_specs.append(pl.BlockSpec(memory_space=pl.ANY))
        args.append(prev)
        out_specs, out_shape = o_spec, o_shape
        aliases = {len(args) - 1: 0}
    return pl.pallas_call(
        functools.partial(_ret_kernel, t=t, rope=rope_tabs is not None, with_s0=s0 is not None,
                          emit_state=emit_state),
        grid=(n_req, npair),
        in_specs=in_specs,
        out_specs=out_specs,
        out_shape=out_shape,
        input_output_aliases=aliases,
        compiler_params=_params(("parallel", "parallel"), 56),
        name="retention",
    )(*args)


def _seg_scan(x, reverse):
    t = x.shape[0]
    row = lax.broadcasted_iota(jnp.int32, (t, 1), 0) % HGRN_CHUNK
    sft = 1
    while sft < HGRN_CHUNK:
        if reverse:
            x = x + jnp.where(row < HGRN_CHUNK - sft, pltpu.roll(x, t - sft, axis=0), 0.0)
        else:
            x = x + jnp.where(row >= sft, pltpu.roll(x, sft, axis=0), 0.0)
        sft *= 2
    return x


def _hgrn_intra(q, k, v, c, reverse):
    cs = HGRN_CHUNK
    sub = 8
    row = lax.broadcasted_iota(jnp.int32, (cs, 1), 0)
    parts = [jnp.zeros((sub, v.shape[1]), F32) for _ in range(cs // sub)]
    for s in range(cs):
        blk_s = s // sub
        blks = range(0, blk_s + 1) if reverse else range(blk_s, cs // sub)
        c_s, k_s, v_s = c[s:s + 1], k[s:s + 1], v[s:s + 1]
        for bt in blks:
            rs = slice(bt * sub, (bt + 1) * sub)
            w = q[rs] * k_s * jnp.exp(jnp.minimum(c[rs] - c_s, 0.0))
            col = jnp.sum(w, axis=-1, keepdims=True)
            if bt == blk_s:
                keep = (row[rs] <= s) if reverse else (row[rs] >= s)
                col = jnp.where(keep, col, 0.0)
            parts[bt] = parts[bt] + col * v_s
    return jnp.concatenate(parts, axis=0)


def _hgrn_kernel(*refs, t, with_s0, emit_state):
    it = iter(refs)
    q_ref, ff_ref, fb_ref, i_ref, g_ref, lb_ref, gn_ref = (next(it) for _ in range(7))
    s0_ref = st_ref = None
    if with_s0:
        s0_ref = next(it)
    if not emit_state:
        next(it)
    o_ref = next(it)
    if emit_state:
        st_ref = next(it)
    qs_ref, k_ref, c_ref, of_ref, state_ref = (next(it) for _ in range(5))

    cs = HGRN_CHUNK
    nchunk = t // cs
    qs_ref[...] = _silu(q_ref[...]) * (DK_C ** -0.5)
    for d, f_ref in enumerate((ff_ref, fb_ref)):
        lb = lb_ref[d:d + 1, :]
        fr = f_ref[...]
        f = lb + (1.0 - lb) * _sigmoid(fr)
        k_ref[d] = (1.0 - lb) * _sigmoid(-fr)
        c_ref[d] = _seg_scan(jnp.log(jnp.maximum(f, F_MIN)), reverse=(d == 1))
        state_ref[d] = s0_ref[d].T if with_s0 else jnp.zeros((DK_C, DK_C), F32)

    def chunk_step(i, carry):
        for d in range(2):
            ci = i if d == 0 else nchunk - 1 - i
            rows = pl.ds(pl.multiple_of(ci * cs, cs), cs)
            q, k, c, v = qs_ref[rows, :], k_ref[d, rows, :], c_ref[d, rows, :], i_ref[rows, :]
            c_end = c[cs - 1:cs] if d == 0 else c[0:1]
            st = state_ref[d]
            o = _dot_nt((q * jnp.exp(c)).astype(BF16), st.astype(BF16))
            o = o + _hgrn_intra(q, k, v, c, reverse=(d == 1))
            kd = (k * jnp.exp(c_end - c)).astype(BF16)
            state_ref[d] = st * jnp.exp(c_end) + _dot_tn(v.astype(BF16), kd)
            of_ref[d, rows, :] = o
        return carry

    lax.fori_loop(0, nchunk, chunk_step, 0)

    o = of_ref[0] + of_ref[1]
    on = o * lax.rsqrt(jnp.mean(o * o, axis=-1, keepdims=True) + EPS) * gn_ref[...]
    o_ref[...] = (on * _silu(g_ref[...])).astype(o_ref.dtype)
    if emit_state:
        st_ref[0] = state_ref[0].T
        st_ref[1] = state_ref[1].T


def _hgrn(z, lower, gnorm, n_req, t, row_blk0, n_rows, *, s0=None, oi=0, prev=None):
    nh = N_HEADS_C
    blk = lambda p: pl.BlockSpec((t, DK_C), lambda b, h, p=p: (row_blk0 + b, p * nh + h))
    in_specs = [blk(0), blk(1), blk(2), blk(3), blk(4),
                pl.BlockSpec((2, DK_C), lambda b, h: (0, h)),
                pl.BlockSpec((1, DK_C), lambda b, h: (0, 0))]
    args = [z, z, z, z, z, lower, gnorm]
    if s0 is not None:
        in_specs.append(pl.BlockSpec((None, None, 2, None, DK_C, DK_C), lambda b, h: (b, oi, 0, h, 0, 0)))
        args.append(s0)
    emit_state = prev is None
    o_spec = pl.BlockSpec((t, DK_C), lambda b, h: (row_blk0 + b, h))
    o_shape = jax.ShapeDtypeStruct((n_rows, D_MODEL), BF16)
    if emit_state:
        out_specs = [o_spec, pl.BlockSpec((None, 2, None, DK_C, DK_C), lambda b, h: (b, 0, h, 0, 0))]
        out_shape = [o_shape, jax.ShapeDtypeStruct((n_req, 2, nh, DK_C, DK_C), F32)]
        aliases = {}
    else:
        in_specs.append(pl.BlockSpec(memory_space=pl.ANY))
        args.append(prev)
        out_specs, out_shape = o_spec, o_shape
        aliases = {len(args) - 1: 0}
    return pl.pallas_call(
        functools.partial(_hgrn_kernel, t=t, with_s0=s0 is not None, emit_state=emit_state),
        grid=(n_req, nh),
        in_specs=in_specs,
        out_specs=out_specs,
        out_shape=out_shape,
        input_output_aliases=aliases,
        scratch_shapes=[
            pltpu.VMEM((t, DK_C), F32),
            pltpu.VMEM((2, t, DK_C), F32),
            pltpu.VMEM((2, t, DK_C), F32),
            pltpu.VMEM((2, t, DK_C), F32),
            pltpu.VMEM((2, DK_C, DK_C), F32),
        ],
        compiler_params=_params(("parallel", "parallel"), 32),
        name="hgrn",
    )(*args)


def kernel(x_prompt, x_sample, cache_kv, state_ret, state_hgrn, c, c_ctx, w_mod, b_mod, norm_g,
           w_in_even, w_out_even, rpb, ret_decay, w_in_odd, w_out_odd, hgrn_lb, hgrn_gnorm,
           w_ffn_in, w_ffn_out):
    bp, tp, _ = x_prompt.shape
    bs, ts, _ = x_sample.shape
    np_rows, ns_rows = bp * tp, bs * ts
    n_rows = np_rows + ns_rows
    assert np_rows % GROUP_ROWS == 0 and ts == GROUP_ROWS
    x = jnp.concatenate([x_prompt.reshape(np_rows, D_MODEL), x_sample.reshape(ns_rows, D_MODEL)], axis=0)

    n_c = bs + 1
    pad = (-n_c) % 8
    cvec = jnp.concatenate([c, c_ctx[None], jnp.zeros((pad, D_MODEL), F32)], axis=0)
    mod_all = _modulation(cvec, w_mod, b_mod)
    grp = np.concatenate([np.full(np_rows // GROUP_ROWS, bs), np.arange(bs)]).astype(np.int32)
    mod_all = mod_all[:, grp].reshape(DEPTH, len(grp), 6, D_MODEL)
    mod_all = jnp.pad(mod_all, ((0, 0), (0, 0), (0, MOD_ROWS - 6), (0, 0)))

    p_lb = jax.nn.softmax(hgrn_lb.astype(F32), axis=0)
    lower = jnp.clip(jnp.cumsum(p_lb, axis=0) - p_lb[0], 0.0, 1.0)
    na_bias = _na_bias_tables(rpb)
    rope_tabs = _rope_tables(ts)

    w_in_even_b, w_out_even_b = w_in_even.astype(BF16), w_out_even.astype(BF16)
    w_in_odd_b, w_out_odd_b = w_in_odd.astype(BF16), w_out_odd.astype(BF16)
    w_ffn_in_b, w_ffn_out_b = w_ffn_in.astype(BF16), w_ffn_out.astype(BF16)

    kv_states, ret_states, hg_states = [], [], []
    for l in range(DEPTH):
        mod = mod_all[l]
        g = norm_g[l].reshape(4, 1, D_MODEL)
        if l % 2 == 0:
            e = l // 2
            z = _in_proj(x, g[0], mod, w_in_even_b[e])
            oa, kv = _ctx_attn(z, bp, tp, n_rows)
            oa = _na_attn(z, cache_kv, e, na_bias, oa, bs, ts, np_rows // ts)
            ob, s_ret = _retention(z, ret_decay[e], bp, tp, 0, n_rows)
            ob = _retention(z, ret_decay[e], bs, ts, np_rows // ts, n_rows,
                            rope_tabs=rope_tabs, s0=state_ret, e=e, prev=ob)
            kv_states.append(kv)
            ret_states.append(s_ret)
            x = _out_proj([oa, ob], w_out_even_b[e], x, g[1], mod)
        else:
            oi = l // 2
            z = _in_proj(x, g[0], mod, w_in_odd_b[oi])
            gn = hgrn_gnorm[oi].reshape(1, DK_C)
            o, s_hg = _hgrn(z, lower[oi], gn, bp, tp, 0, n_rows)
            o = _hgrn(z, lower[oi], gn, bs, ts, np_rows // ts, n_rows, s0=state_hgrn, oi=oi, prev=o)
            hg_states.append(s_hg)
            x = _out_proj([o], w_out_odd_b[oi], x, g[1], mod)
        x = _ffn(x, g[2], g[3], mod, w_ffn_in_b[l], w_ffn_out_b[l])

    y_prompt = x[:np_rows].reshape(bp, tp, D_MODEL)
    y_sample = x[np_rows:].reshape(bs, ts, D_MODEL)
    return (y_prompt, y_sample, jnp.stack(kv_states, axis=1), jnp.stack(ret_states, axis=1),
            jnp.stack(hg_states, axis=1))
```

```python
import functools

import numpy as np
import jax
import jax.numpy as jnp
from jax import lax
from jax.experimental import pallas as pl
from jax.experimental.pallas import tpu as pltpu

F32 = jnp.float32
BF16 = jnp.bfloat16

D_MODEL = 1024
DEPTH = 4
GRID_W = 64
HEAD_DIM = 64
N_HEADS_A = 8
N_HEADS_B = 8
D_A = N_HEADS_A * HEAD_DIM
D_B = N_HEADS_B * HEAD_DIM
WIN_R = 8
WIN_C = 16
N_HEADS_C = 8
DK_C = D_MODEL // N_HEADS_C
D_FF = ((8 * D_MODEL // 3 + 255) // 256) * 256
HGRN_CHUNK = 32
HGRN_SUPER = 256
HGRN_SAFE_LOG = 75.0
ROPE_BASE = 10000.0
EPS = 1e-6
MASK_NEG = -1e30
F_MIN = 1e-30

GROUP_ROWS = 1024
MOD_ROWS = 8
LANES = 128
MIB = 1024 * 1024

SH1, SC1, GT1, SH2, SC2, GT2 = range(6)

NA_QROWS = 4
NA_KROWS = 12
NA_KSTART = (0, 0, 4, 4)


def _params(sem, vmem_mib):
    return pltpu.CompilerParams(dimension_semantics=sem, vmem_limit_bytes=vmem_mib * MIB)


def _sigmoid(x):
    return 1.0 / (1.0 + jnp.exp(-x))


def _silu(x):
    return x * _sigmoid(x)


def _dot(a, b):
    return jnp.dot(a, b, preferred_element_type=F32)


def _dot_nt(a, b):
    return lax.dot_general(a, b, (((1,), (1,)), ((), ())), preferred_element_type=F32)


def _dot_tn(a, b):
    return lax.dot_general(a, b, (((0,), (0,)), ((), ())), preferred_element_type=F32)


def _mod_kernel(c_ref, w_ref, b_ref, o_ref):
    s = _silu(c_ref[...]).astype(BF16)
    o_ref[...] = _dot(s, w_ref[...].astype(BF16)) + b_ref[...]


def _modulation(cvec, w_mod, b_mod):
    rows = cvec.shape[0]
    n = w_mod.shape[-1]
    tn = 1536
    return pl.pallas_call(
        _mod_kernel,
        grid=(DEPTH, n // tn),
        in_specs=[
            pl.BlockSpec((rows, D_MODEL), lambda l, j: (0, 0)),
            pl.BlockSpec((None, D_MODEL, tn), lambda l, j: (l, 0, j)),
            pl.BlockSpec((None, 1, tn), lambda l, j: (l, 0, j)),
        ],
        out_specs=pl.BlockSpec((None, rows, tn), lambda l, j: (l, 0, j)),
        out_shape=jax.ShapeDtypeStruct((DEPTH, rows, n), F32),
        compiler_params=_params(("arbitrary", "arbitrary"), 32),
        name="modulation",
    )(cvec, w_mod, b_mod.reshape(DEPTH, 1, n))


def _norm_mod(x, g, mod, sh_row, sc_row):
    y = x * lax.rsqrt(jnp.mean(x * x, axis=-1, keepdims=True) + EPS) * g
    return y * (1.0 + mod[sc_row:sc_row + 1]) + mod[sh_row:sh_row + 1]


def _in_proj_kernel(x_ref, g_ref, mod_ref, w_ref, o_ref, h_ref):
    @pl.when(pl.program_id(1) == 0)
    def _():
        h_ref[...] = _norm_mod(x_ref[...], g_ref[...], mod_ref[...], SH1, SC1).astype(BF16)

    o_ref[...] = _dot(h_ref[...], w_ref[...])


def _in_proj(x, g, mod, w):
    m, n = x.shape[0], w.shape[1]
    tm, tn = 1024, 512
    return pl.pallas_call(
        _in_proj_kernel,
        grid=(m // tm, n // tn),
        in_specs=[
            pl.BlockSpec((tm, D_MODEL), lambda i, j: (i, 0)),
            pl.BlockSpec((1, D_MODEL), lambda i, j: (0, 0)),
            pl.BlockSpec((None, MOD_ROWS, D_MODEL), lambda i, j: (i * tm // GROUP_ROWS, 0, 0)),
            pl.BlockSpec((D_MODEL, tn), lambda i, j: (0, j)),
        ],
        out_specs=pl.BlockSpec((tm, tn), lambda i, j: (i, j)),
        out_shape=jax.ShapeDtypeStruct((m, n), F32),
        scratch_shapes=[pltpu.VMEM((tm, D_MODEL), BF16)],
        compiler_params=_params(("parallel", "arbitrary"), 40),
        name="in_proj",
    )(x, g, mod, w)


def _out_proj_kernel(*refs, n_in):
    a_refs, w_refs = refs[:n_in], refs[n_in:2 * n_in]
    x_ref, g_ref, mod_ref, o_ref = refs[2 * n_in:]
    y = _dot(a_refs[0][...], w_refs[0][...])
    for a_ref, w_ref in zip(a_refs[1:], w_refs[1:]):
        y = y + _dot(a_ref[...], w_ref[...])
    yn = y * lax.rsqrt(jnp.mean(y * y, axis=-1, keepdims=True) + EPS) * g_ref[...]
    o_ref[...] = x_ref[...] + mod_ref[GT1:GT1 + 1, :] * yn


def _out_proj(acts, w, x, g, mod):
    m = x.shape[0]
    tm = 1024
    n_in = len(acts)
    ks = [a.shape[1] for a in acts]
    assert sum(ks) == w.shape[0] and len(set(ks)) == 1
    in_specs = [pl.BlockSpec((tm, k), lambda i: (i, 0)) for k in ks]
    in_specs += [pl.BlockSpec((ks[0], D_MODEL), lambda i, p=p: (p, 0)) for p in range(n_in)]
    in_specs += [
        pl.BlockSpec((tm, D_MODEL), lambda i: (i, 0)),
        pl.BlockSpec((1, D_MODEL), lambda i: (0, 0)),
        pl.BlockSpec((None, MOD_ROWS, D_MODEL), lambda i: (i * tm // GROUP_ROWS, 0, 0)),
    ]
    return pl.pallas_call(
        functools.partial(_out_proj_kernel, n_in=n_in),
        grid=(m // tm,),
        in_specs=in_specs,
        out_specs=pl.BlockSpec((tm, D_MODEL), lambda i: (i, 0)),
        out_shape=jax.ShapeDtypeStruct((m, D_MODEL), F32),
        compiler_params=_params(("parallel",), 48),
        name="out_proj",
    )(*acts, *([w] * n_in), x, g, mod)


def _ffn_kernel(x_ref, g2_ref, g3_ref, mod_ref, wa_ref, wu_ref, wo_ref, o_ref, h_ref, acc_ref):
    f = pl.program_id(1)

    @pl.when(f == 0)
    def _():
        h_ref[...] = _norm_mod(x_ref[...], g2_ref[...], mod_ref[...], SH2, SC2).astype(BF16)
        acc_ref[...] = jnp.zeros_like(acc_ref)

    h = h_ref[...]
    a = _dot(h, wa_ref[...])
    u = _dot(h, wu_ref[...])
    acc_ref[...] += _dot((_silu(a) * u).astype(BF16), wo_ref[...])

    @pl.when(f == pl.num_programs(1) - 1)
    def _():
        y = acc_ref[...]
        yn = y * lax.rsqrt(jnp.mean(y * y, axis=-1, keepdims=True) + EPS) * g3_ref[...]
        o_ref[...] = x_ref[...] + mod_ref[GT2:GT2 + 1, :] * yn


def _ffn(x, g2, g3, mod, w_in, w_out):
    m = x.shape[0]
    tm, tf = 1024, 256
    nf = D_FF // tf
    return pl.pallas_call(
        _ffn_kernel,
        grid=(m // tm, nf),
        in_specs=[
            pl.BlockSpec((tm, D_MODEL), lambda i, f: (i, 0)),
            pl.BlockSpec((1, D_MODEL), lambda i, f: (0, 0)),
            pl.BlockSpec((1, D_MODEL), lambda i, f: (0, 0)),
            pl.BlockSpec((None, MOD_ROWS, D_MODEL), lambda i, f: (i * tm // GROUP_ROWS, 0, 0)),
            pl.BlockSpec((D_MODEL, tf), lambda i, f: (0, f)),
            pl.BlockSpec((D_MODEL, tf), lambda i, f: (0, nf + f)),
            pl.BlockSpec((tf, D_MODEL), lambda i, f: (f, 0)),
        ],
        out_specs=pl.BlockSpec((tm, D_MODEL), lambda i, f: (i, 0)),
        out_shape=jax.ShapeDtypeStruct((m, D_MODEL), F32),
        scratch_shapes=[pltpu.VMEM((tm, D_MODEL), BF16), pltpu.VMEM((tm, D_MODEL), F32)],
        compiler_params=_params(("parallel", "arbitrary"), 48),
        name="ffn",
    )(x, g2, g3, mod, w_in, w_in, w_out)


def _ctx_attn_kernel(q_ref, k_ref, v_ref, o_ref, kv_ref, *, t, nb):
    for r in range(nb):
        rs = slice(r * t, (r + 1) * t)
        q2, k2, v2 = q_ref[rs, :], k_ref[rs, :], v_ref[rs, :]
        outs = []
        for hh in range(LANES // HEAD_DIM):
            hs = slice(hh * HEAD_DIM, (hh + 1) * HEAD_DIM)
            q, k, v = q2[:, hs], k2[:, hs], v2[:, hs]
            kv_ref[r, 0, hh] = k
            kv_ref[r, 1, hh] = v
            s = _dot_nt(q.astype(BF16), k.astype(BF16)) * (HEAD_DIM ** -0.5)
            p = jnp.exp(s - jnp.max(s, axis=-1, keepdims=True))
            l = jnp.sum(p, axis=-1, keepdims=True)
            outs.append(_dot(p.astype(BF16), v.astype(BF16)) / l)
        o_ref[rs, :] = jnp.concatenate(outs, axis=-1).astype(o_ref.dtype)


def _ctx_attn(z, n_req, t, n_rows):
    npair = D_A // LANES
    nb = GROUP_ROWS // t
    blk = lambda p: pl.BlockSpec((nb * t, LANES), lambda b, j, p=p: (b, p * npair + j))
    return pl.pallas_call(
        functools.partial(_ctx_attn_kernel, t=t, nb=nb),
        grid=(n_req // nb, npair),
        in_specs=[blk(0), blk(1), blk(2)],
        out_specs=[
            pl.BlockSpec((nb * t, LANES), lambda b, j: (b, j)),
            pl.BlockSpec((nb, 2, LANES // HEAD_DIM, t, HEAD_DIM), lambda b, j: (b, 0, j, 0, 0)),
        ],
        out_shape=[
            jax.ShapeDtypeStruct((n_rows, D_A), BF16),
            jax.ShapeDtypeStruct((n_req, 2, N_HEADS_A, t, HEAD_DIM), F32),
        ],
        compiler_params=_params(("parallel", "parallel"), 32),
        name="ctx_attn",
    )(z, z, z)


def _na_bias_tables(rpb):
    qc = np.arange(GRID_W)[:, None]
    kc = np.arange(GRID_W)[None, :]
    win0 = np.clip(qc - WIN_C // 2, 0, GRID_W - WIN_C)
    col_valid = (kc >= win0) & (kc < win0 + WIN_C)
    col_off = np.clip(kc - qc, 1 - WIN_C, WIN_C - 1) + (WIN_C - 1)
    tiles = jnp.where(col_valid[None, None, None], rpb.astype(F32)[:, :, :, col_off], MASK_NEG)
    nxt = jnp.concatenate([tiles[:, :, 1:], tiles[:, :, -1:]], axis=2)
    return jnp.concatenate([tiles, nxt], axis=-1)


def _na_group_bias(pair_ref, hh, g):
    rows = 16
    n_ro = 2 * WIN_R - 1
    tk = NA_KROWS * GRID_W
    lane = lax.broadcasted_iota(jnp.int32, (GRID_W, tk), 1)
    bias, valid = [], []
    for a in range(NA_QROWS):
        r = g * NA_QROWS + a
        row0 = min(max(r - WIN_R // 2, 0), rows - WIN_R)
        pieces = []
        for w in range(0, NA_KROWS, 2):
            ro = NA_KSTART[g] + w - r + (WIN_R - 1)
            pieces.append(pair_ref[hh, min(max(ro, 0), n_ro - 1)])
        bias.append(jnp.concatenate(pieces, axis=1))
        lo = (row0 - NA_KSTART[g]) * GRID_W
        valid.append((lane >= lo) & (lane < lo + WIN_R * GRID_W))
    return jnp.concatenate(bias, axis=0), jnp.concatenate(valid, axis=0)


def _na_kernel(q_ref, k_ref, v_ref, ckv_ref, pair_ref, prev_ref, o_ref):
    del prev_ref
    q2, k2, v2 = q_ref[...], k_ref[...], v_ref[...]
    scale = HEAD_DIM ** -0.5
    tq = NA_QROWS * GRID_W
    tk = NA_KROWS * GRID_W
    outs = []
    for hh in range(LANES // HEAD_DIM):
        hs = slice(hh * HEAD_DIM, (hh + 1) * HEAD_DIM)
        q = q2[:, hs].astype(BF16)
        k = k2[:, hs].astype(BF16)
        v = v2[:, hs].astype(BF16)
        kc = ckv_ref[0, hh].astype(BF16)
        vc = ckv_ref[1, hh].astype(BF16)
        rows = []
        for g in range(len(NA_KSTART)):
            qg = q[g * tq:(g + 1) * tq]
            k0 = NA_KSTART[g] * GRID_W
            bias, valid = _na_group_bias(pair_ref, hh, g)
            s_loc = jnp.where(valid, _dot_nt(qg, k[k0:k0 + tk]) * scale + bias, MASK_NEG)
            s_ctx = _dot_nt(qg, kc) * scale
            m = jnp.maximum(jnp.max(s_loc, axis=-1, keepdims=True), jnp.max(s_ctx, axis=-1, keepdims=True))
            p_loc = jnp.exp(s_loc - m)
            p_ctx = jnp.exp(s_ctx - m)
            l = jnp.sum(p_loc, axis=-1, keepdims=True) + jnp.sum(p_ctx, axis=-1, keepdims=True)
            o = _dot(p_loc.astype(BF16), v[k0:k0 + tk]) + _dot(p_ctx.astype(BF16), vc)
            rows.append(o / l)
        outs.append(jnp.concatenate(rows, axis=0))
    o_ref[...] = jnp.concatenate(outs, axis=-1).astype(o_ref.dtype)


def _na_attn(z, cache_kv, e, bias, prev, n_req, t, row_blk0):
    npair = D_A // LANES
    hp = LANES // HEAD_DIM
    past = cache_kv.shape[-2]
    blk = lambda p: pl.BlockSpec((t, LANES), lambda j, b, p=p: (row_blk0 + b, p * npair + j))
    return pl.pallas_call(
        _na_kernel,
        grid=(npair, n_req),
        in_specs=[
            blk(0), blk(1), blk(2),
            pl.BlockSpec((None, None, 2, hp, past, HEAD_DIM), lambda j, b: (b, e, 0, j, 0, 0)),
            pl.BlockSpec((None, hp) + bias.shape[2:], lambda j, b: (e, j, 0, 0, 0)),
            pl.BlockSpec(memory_space=pl.ANY),
        ],
        out_specs=pl.BlockSpec((t, LANES), lambda j, b: (row_blk0 + b, j)),
        out_shape=jax.ShapeDtypeStruct(prev.shape, prev.dtype),
        input_output_aliases={5: 0},
        compiler_params=_params(("parallel", "parallel"), 56),
        name="na_attn",
    )(z, z, z, cache_kv, bias, prev)


def _rope_tables(t):
    half = HEAD_DIM // 2
    nf = half // 2
    inv = ROPE_BASE ** (-np.arange(nf, dtype=np.float32) / nf)
    pos = np.arange(t)
    ang_r = (pos // GRID_W).astype(np.float32)[:, None] * inv[None, :]
    ang_c = (pos % GRID_W).astype(np.float32)[:, None] * inv[None, :]
    ang_r, ang_c = jnp.asarray(ang_r), jnp.asarray(ang_c)
    cr, sr, cc, sc = jnp.cos(ang_r), jnp.sin(ang_r), jnp.cos(ang_c), jnp.sin(ang_c)
    cos = jnp.concatenate([cr, cr, cc, cc], axis=-1)
    sin = jnp.concatenate([-sr, sr, -sc, sc], axis=-1)
    reps = LANES // HEAD_DIM
    return jnp.tile(cos, (1, reps)), jnp.tile(sin, (1, reps))


def _rope(x, cos, sin):
    nf = HEAD_DIM // 4
    lane = lax.broadcasted_iota(jnp.int32, x.shape, 1)
    partner = jnp.where(lane % (2 * nf) < nf, pltpu.roll(x, LANES - nf, axis=1), pltpu.roll(x, nf, axis=1))
    return x * cos + partner * sin


def _ret_kernel(*refs, t, nb, rope, with_s0, emit_state):
    it = iter(refs)
    dec_ref, q_ref, k_ref, v_ref, g_ref = (next(it) for _ in range(5))
    cos_ref = sin_ref = s0_ref = st_ref = None
    if rope:
        cos_ref, sin_ref = next(it), next(it)
    if with_s0:
        s0_ref = next(it)
    if not emit_state:
        next(it)
    o_ref = next(it)
    if emit_state:
        st_ref = next(it)
    dm_ref = next(it)

    hp = LANES // HEAD_DIM
    j = pl.program_id(0)
    lg = [[-jnp.exp(jnp.full((1, 1), dec_ref[d, j * hp + hh], F32)) for d in range(2)] for hh in range(hp)]

    @pl.when(pl.program_id(1) == 0)
    def _():
        n_i = lax.broadcasted_iota(jnp.int32, (t, t), 0)
        m_i = lax.broadcasted_iota(jnp.int32, (t, t), 1)
        diff = (n_i - m_i).astype(F32)
        for hh in range(hp):
            dm_ref[hh] = (jnp.where(diff >= 0, jnp.exp(lg[hh][0] * jnp.maximum(diff, 0.0)), 0.0)
                          + jnp.where(diff <= 0, jnp.exp(lg[hh][1] * jnp.maximum(-diff, 0.0)), 0.0))

    pos = lax.broadcasted_iota(jnp.int32, (t, 1), 0).astype(F32)
    for r in range(nb):
        rs = slice(r * t, (r + 1) * t)
        q2, k2, v2, g2 = q_ref[rs, :], k_ref[rs, :] * (HEAD_DIM ** -0.5), v_ref[rs, :], g_ref[rs, :]
        if rope:
            q2 = _rope(q2, cos_ref[...], sin_ref[...])
            k2 = _rope(k2, cos_ref[...], sin_ref[...])
        outs = []
        for hh in range(hp):
            hs = slice(hh * HEAD_DIM, (hh + 1) * HEAD_DIM)
            lg_f, lg_b = lg[hh]
            q, k, v = q2[:, hs], k2[:, hs], v2[:, hs]
            qb, kb, vb = q.astype(BF16), k.astype(BF16), v.astype(BF16)
            att = _dot_nt(qb, kb) * dm_ref[hh]
            o = _dot(att.astype(BF16), vb)
            if with_s0:
                qs = jnp.concatenate([q * jnp.exp(lg_f * (pos + 1.0)), q * jnp.exp(lg_b * (t - pos))], axis=-1)
                s0 = jnp.concatenate([s0_ref[0, hh], s0_ref[1, hh]], axis=0)
                o = o + _dot(qs.astype(BF16), s0.astype(BF16))
            if emit_state:
                kf = k * jnp.exp(lg_f * (t - 1.0 - pos))
                kr = k * jnp.exp(lg_b * pos)
                st_ref[r, 0, hh] = _dot_tn(kf.astype(BF16), vb)
                st_ref[r, 1, hh] = _dot_tn(kr.astype(BF16), vb)
            on = o * lax.rsqrt(jnp.mean(o * o, axis=-1, keepdims=True) + EPS)
            outs.append(on * _silu(g2[:, hs]))
        o_ref[rs, :] = jnp.concatenate(outs, axis=-1).astype(o_ref.dtype)


def _retention(z, ret_decay, n_req, t, row_blk0, n_rows, *, rope_tabs=None, s0=None, e=0, prev=None):
    npair = D_B // LANES
    hp = LANES // HEAD_DIM
    sec0 = 3 * D_A // LANES
    nb = GROUP_ROWS // t
    blk = lambda p: pl.BlockSpec((nb * t, LANES), lambda j, b, p=p: (row_blk0 + b, sec0 + p * npair + j))
    in_specs = [pl.BlockSpec(memory_space=pltpu.SMEM), blk(0), blk(1), blk(2), blk(3)]
    args = [ret_decay, z, z, z, z]
    if rope_tabs is not None:
        in_specs += [pl.BlockSpec((t, LANES), lambda j, b: (0, 0))] * 2
        args += list(rope_tabs)
    if s0 is not None:
        assert nb == 1
        in_specs.append(pl.BlockSpec((None, None, 2, hp, HEAD_DIM, HEAD_DIM), lambda j, b: (b, e, 0, j, 0, 0)))
        args.append(s0)
    emit_state = prev is None
    o_spec = pl.BlockSpec((nb * t, LANES), lambda j, b: (row_blk0 + b, j))
    o_shape = jax.ShapeDtypeStruct((n_rows, D_B), BF16)
    if emit_state:
        out_specs = [o_spec, pl.BlockSpec((nb, 2, hp, HEAD_DIM, HEAD_DIM), lambda j, b: (b, 0, j, 0, 0))]
        out_shape = [o_shape, jax.ShapeDtypeStruct((n_req, 2, N_HEADS_B, HEAD_DIM, HEAD_DIM), F32)]
        aliases = {}
    else:
        in_specs.append(pl.BlockSpec(memory_space=pl.ANY))
        args.append(prev)
        out_specs, out_shape = o_spec, o_shape
        aliases = {len(args) - 1: 0}
    return pl.pallas_call(
        functools.partial(_ret_kernel, t=t, nb=nb, rope=rope_tabs is not None, with_s0=s0 is not None,
                          emit_state=emit_state),
        grid=(npair, n_req // nb),
        in_specs=in_specs,
        out_specs=out_specs,
        out_shape=out_shape,
        input_output_aliases=aliases,
        scratch_shapes=[pltpu.VMEM((hp, t, t), F32)],
        compiler_params=_params(("parallel", "arbitrary"), 56),
        name="retention",
    )(*args)


def _seg_scan(x, reverse):
    t = x.shape[0]
    row = lax.broadcasted_iota(jnp.int32, (t, 1), 0) % HGRN_CHUNK
    sft = 1
    while sft < HGRN_CHUNK:
        if reverse:
            x = x + jnp.where(row < HGRN_CHUNK - sft, pltpu.roll(x, t - sft, axis=0), 0.0)
        else:
            x = x + jnp.where(row >= sft, pltpu.roll(x, sft, axis=0), 0.0)
        sft *= 2
    return x


def _hgrn_intra(q, k, v, c, reverse):
    cs = HGRN_CHUNK
    sub = 8
    row = lax.broadcasted_iota(jnp.int32, (cs, 1), 0)
    parts = [jnp.zeros((sub, v.shape[1]), F32) for _ in range(cs // sub)]
    for s in range(cs):
        blk_s = s // sub
        blks = range(0, blk_s + 1) if reverse else range(blk_s, cs // sub)
        c_s, k_s, v_s = c[s:s + 1], k[s:s + 1], v[s:s + 1]
        for bt in blks:
            rs = slice(bt * sub, (bt + 1) * sub)
            w = q[rs] * k_s * jnp.exp(jnp.minimum(c[rs] - c_s, 0.0))
            col = jnp.sum(w, axis=-1, keepdims=True)
            if bt == blk_s:
                keep = (row[rs] <= s) if reverse else (row[rs] >= s)
                col = jnp.where(keep, col, 0.0)
            parts[bt] = parts[bt] + col * v_s
    return jnp.concatenate(parts, axis=0)


def _hgrn_kernel(*refs, t, nb, with_s0, emit_state):
    it = iter(refs)
    q_ref, ff_ref, fb_ref, i_ref, g_ref, lb_ref, gn_ref = (next(it) for _ in range(7))
    s0_ref = st_ref = None
    if with_s0:
        s0_ref = next(it)
    if not emit_state:
        next(it)
    o_ref = next(it)
    if emit_state:
        st_ref = next(it)
    qe_ref, ke_ref, vb_ref, ee_ref, oi_ref, sq_ref, sk_ref, sc_ref = (next(it) for _ in range(8))

    cs = HGRN_CHUNK
    sb = HGRN_SUPER
    n_sb = t // sb
    row = lax.broadcasted_iota(jnp.int32, (sb, 1), 0)
    ti = lax.broadcasted_iota(jnp.int32, (sb, sb), 0)
    si = lax.broadcasted_iota(jnp.int32, (sb, sb), 1)
    term = jnp.where((ti >> 5) == (si >> 5), 0,
                     jnp.where((ti >> 6) == (si >> 6), 1, jnp.where((ti >> 7) == (si >> 7), 2, 3)))
    term_dir = (jnp.where(si <= ti, term, 4), jnp.where(si >= ti, term, 4))

    def block_step(blk, carry):
        rows = pl.ds(pl.multiple_of(blk * sb, sb), sb)
        qs = _silu(q_ref[rows, :]) * (DK_C ** -0.5)
        v = i_ref[rows, :].astype(BF16)
        vb_ref[rows, :] = v
        per_dir = []
        c_min = None
        for d, f_ref in enumerate((ff_ref, fb_ref)):
            lb = lb_ref[d:d + 1, :]
            fr = f_ref[rows, :]
            e = jnp.exp(-jnp.abs(fr))
            r = 1.0 / (1.0 + e)
            sig_pos = jnp.where(fr >= 0, r, e * r)
            sig_neg = jnp.where(fr >= 0, e * r, r)
            f = lb + (1.0 - lb) * sig_pos
            k = (1.0 - lb) * sig_neg
            c = _seg_scan(jnp.log(jnp.maximum(f, F_MIN)), reverse=(d == 1))
            c3 = c.reshape(sb // cs, cs, DK_C)
            c_end = c3[:, cs - 1:cs, :] if d == 0 else c3[:, 0:1, :]
            tot = jnp.broadcast_to(c_end, c3.shape).reshape(sb, DK_C)
            per_dir.append((k, c, tot))
            m = jnp.min(c)
            c_min = m if c_min is None else jnp.minimum(c_min, m)

        safe = c_min >= -HGRN_SAFE_LOG
        att = None
        for d, (k, c, tot) in enumerate(per_dir):
            q_l = qs * jnp.exp(c)
            k_l = k * jnp.exp(tot - c)
            e_l = jnp.exp(tot)
            k_hat = jnp.where(safe, k * jnp.exp(-c), 0.0).astype(BF16)
            prods = [_dot_nt(q_l.astype(BF16), k_hat), _dot_nt(q_l.astype(BF16), k_l.astype(BF16))]
            size = cs
            while size < sb:
                prev_e = pltpu.roll(e_l, size, axis=0)
                next_e = pltpu.roll(e_l, sb - size, axis=0)
                second = (row % (2 * size)) >= size
                if d == 0:
                    q_l = q_l * jnp.where(second, prev_e, 1.0)
                    k_l = k_l * jnp.where(second, 1.0, next_e)
                else:
                    q_l = q_l * jnp.where(second, 1.0, next_e)
                    k_l = k_l * jnp.where(second, prev_e, 1.0)
                e_l = e_l * jnp.where(second, prev_e, next_e)
                size *= 2
                if size < sb:
                    prods.append(_dot_nt(q_l.astype(BF16), k_l.astype(BF16)))
            qe_ref[d, rows, :] = q_l.astype(BF16)
            ke_ref[d, rows, :] = k_l.astype(BF16)
            ee_ref[d, blk] = e_l[0:8]
            sel = jnp.zeros((sb, sb), F32)
            for i in reversed(range(len(prods))):
                sel = jnp.where(term_dir[d] == i, prods[i], sel)
            att = sel if att is None else att + sel
        oi_ref[rows, :] = _dot(att.astype(BF16), v)

        @pl.when(jnp.logical_not(safe))
        def _():
            sq_ref[...] = qs
            for d in range(2):
                sk_ref[d] = per_dir[d][0]
                sc_ref[d] = per_dir[d][1]

            def chunk_step(i, carry2):
                crow = pl.ds(pl.multiple_of(i * cs, cs), cs)
                orow = pl.ds(pl.multiple_of(blk * sb + i * cs, cs), cs)
                q, vv = sq_ref[crow, :], i_ref[orow, :]
                oi_ref[orow, :] += (_hgrn_intra(q, sk_ref[0, crow, :], vv, sc_ref[0, crow, :], reverse=False)
                                    + _hgrn_intra(q, sk_ref[1, crow, :], vv, sc_ref[1, crow, :], reverse=True))
                return carry2

            lax.fori_loop(0, sb // cs, chunk_step, 0)

        return carry

    lax.fori_loop(0, nb * n_sb, block_step, 0)

    for r in range(nb):
        for d in range(2):
            st = s0_ref[d].T if with_s0 else None
            order = range(n_sb) if d == 0 else reversed(range(n_sb))
            for j in order:
                blk = r * n_sb + j
                rows = slice(blk * sb, (blk + 1) * sb)
                upd = _dot_tn(vb_ref[rows, :], ke_ref[d, rows, :])
                if st is None:
                    st = upd
                else:
                    oi_ref[rows, :] += _dot_nt(qe_ref[d, rows, :], st.astype(BF16))
                    st = st * ee_ref[d, blk][0:1] + upd
            if emit_state:
                st_ref[r, d] = st.T

    o = oi_ref[...]
    on = o * lax.rsqrt(jnp.mean(o * o, axis=-1, keepdims=True) + EPS) * gn_ref[...]
    o_ref[...] = (on * _silu(g_ref[...])).astype(o_ref.dtype)


def _hgrn(z, lower, gnorm, n_req, t, row_blk0, n_rows, *, s0=None, oi=0, prev=None):
    nh = N_HEADS_C
    nb = GROUP_ROWS // t
    rows = nb * t
    blk = lambda p: pl.BlockSpec((rows, DK_C), lambda b, h, p=p: (row_blk0 + b, p * nh + h))
    in_specs = [blk(0), blk(1), blk(2), blk(3), blk(4),
                pl.BlockSpec((2, DK_C), lambda b, h: (0, h)),
                pl.BlockSpec((1, DK_C), lambda b, h: (0, 0))]
    args = [z, z, z, z, z, lower, gnorm]
    if s0 is not None:
        assert nb == 1
        in_specs.append(pl.BlockSpec((None, None, 2, None, DK_C, DK_C), lambda b, h: (b, oi, 0, h, 0, 0)))
        args.append(s0)
    emit_state = prev is None
    o_spec = pl.BlockSpec((rows, DK_C), lambda b, h: (row_blk0 + b, h))
    o_shape = jax.ShapeDtypeStruct((n_rows, D_MODEL), BF16)
    if emit_state:
        out_specs = [o_spec, pl.BlockSpec((nb, 2, None, DK_C, DK_C), lambda b, h: (b, 0, h, 0, 0))]
        out_shape = [o_shape, jax.ShapeDtypeStruct((n_req, 2, nh, DK_C, DK_C), F32)]
        aliases = {}
    else:
        in_specs.append(pl.BlockSpec(memory_space=pl.ANY))
        args.append(prev)
        out_specs, out_shape = o_spec, o_shape
        aliases = {len(args) - 1: 0}
    return pl.pallas_call(
        functools.partial(_hgrn_kernel, t=t, nb=nb, with_s0=s0 is not None, emit_state=emit_state),
        grid=(n_req // nb, nh),
        in_specs=in_specs,
        out_specs=out_specs,
        out_shape=out_shape,
        input_output_aliases=aliases,
        scratch_shapes=[
            pltpu.VMEM((2, rows, DK_C), BF16),
            pltpu.VMEM((2, rows, DK_C), BF16),
            pltpu.VMEM((rows, DK_C), BF16),
            pltpu.VMEM((2, rows // HGRN_SUPER, 8, DK_C), F32),
            pltpu.VMEM((rows, DK_C), F32),
            pltpu.VMEM((HGRN_SUPER, DK_C), F32),
            pltpu.VMEM((2, HGRN_SUPER, DK_C), F32),
            pltpu.VMEM((2, HGRN_SUPER, DK_C), F32),
        ],
        compiler_params=_params(("parallel", "parallel"), 32),
        name="hgrn",
    )(*args)


def kernel(x_prompt, x_sample, cache_kv, state_ret, state_hgrn, c, c_ctx, w_mod, b_mod, norm_g,
           w_in_even, w_out_even, rpb, ret_decay, w_in_odd, w_out_odd, hgrn_lb, hgrn_gnorm,
           w_ffn_in, w_ffn_out):
    bp, tp, _ = x_prompt.shape
    bs, ts, _ = x_sample.shape
    np_rows, ns_rows = bp * tp, bs * ts
    n_rows = np_rows + ns_rows
    assert np_rows % GROUP_ROWS == 0 and ts == GROUP_ROWS
    x = jnp.concatenate([x_prompt.reshape(np_rows, D_MODEL), x_sample.reshape(ns_rows, D_MODEL)], axis=0)

    n_c = bs + 1
    pad = (-n_c) % 8
    cvec = jnp.concatenate([c, c_ctx[None], jnp.zeros((pad, D_MODEL), F32)], axis=0)
    mod_all = _modulation(cvec, w_mod, b_mod)
    grp = np.concatenate([np.full(np_rows // GROUP_ROWS, bs), np.arange(bs)]).astype(np.int32)
    mod_all = mod_all[:, grp].reshape(DEPTH, len(grp), 6, D_MODEL)
    mod_all = jnp.pad(mod_all, ((0, 0), (0, 0), (0, MOD_ROWS - 6), (0, 0)))

    p_lb = jax.nn.softmax(hgrn_lb.astype(F32), axis=0)
    lower = jnp.clip(jnp.cumsum(p_lb, axis=0) - p_lb[0], 0.0, 1.0)
    na_bias = _na_bias_tables(rpb)
    rope_tabs = _rope_tables(ts)

    w_in_even_b, w_out_even_b = w_in_even.astype(BF16), w_out_even.astype(BF16)
    w_in_odd_b, w_out_odd_b = w_in_odd.astype(BF16), w_out_odd.astype(BF16)
    w_ffn_in_b, w_ffn_out_b = w_ffn_in.astype(BF16), w_ffn_out.astype(BF16)

    kv_states, ret_states, hg_states = [], [], []
    for l in range(DEPTH):
        mod = mod_all[l]
        g = norm_g[l].reshape(4, 1, D_MODEL)
        if l % 2 == 0:
            e = l // 2
            z = _in_proj(x, g[0], mod, w_in_even_b[e])
            oa, kv = _ctx_attn(z, bp, tp, n_rows)
            oa = _na_attn(z, cache_kv, e, na_bias, oa, bs, ts, np_rows // ts)
            ob, s_ret = _retention(z, ret_decay[e], bp, tp, 0, n_rows)
            ob = _retention(z, ret_decay[e], bs, ts, np_rows // ts, n_rows,
                            rope_tabs=rope_tabs, s0=state_ret, e=e, prev=ob)
            kv_states.append(kv)
            ret_states.append(s_ret)
            x = _out_proj([oa, ob], w_out_even_b[e], x, g[1], mod)
        else:
            oi = l // 2
            z = _in_proj(x, g[0], mod, w_in_odd_b[oi])
            gn = hgrn_gnorm[oi].reshape(1, DK_C)
            o, s_hg = _hgrn(z, lower[oi], gn, bp, tp, 0, n_rows)
            o = _hgrn(z, lower[oi], gn, bs, ts, np_rows // ts, n_rows, s0=state_hgrn, oi=oi, prev=o)
            hg_states.append(s_hg)
            x = _out_proj([o], w_out_odd_b[oi], x, g[1], mod)
        x = _ffn(x, g[2], g[3], mod, w_ffn_in_b[l], w_ffn_out_b[l])

    y_prompt = x[:np_rows].reshape(bp, tp, D_MODEL)
    y_sample = x[np_rows:].reshape(bs, ts, D_MODEL)
    return (y_prompt, y_sample, jnp.stack(kv_states, axis=1), jnp.stack(ret_states, axis=1),
            jnp.stack(hg_states, axis=1))
```

```python
import functools

import numpy as np
import jax
import jax.numpy as jnp
from jax import lax
from jax.experimental import pallas as pl
from jax.experimental.pallas import tpu as pltpu

F32 = jnp.float32
BF16 = jnp.bfloat16

D_MODEL = 1024
DEPTH = 4
GRID_W = 64
HEAD_DIM = 64
N_HEADS_A = 8
N_HEADS_B = 8
D_A = N_HEADS_A * HEAD_DIM
D_B = N_HEADS_B * HEAD_DIM
WIN_R = 8
WIN_C = 16
N_HEADS_C = 8
DK_C = D_MODEL // N_HEADS_C
D_C = N_HEADS_C * DK_C
D_FF = ((8 * D_MODEL // 3 + 255) // 256) * 256
HGRN_CHUNK = 32
HGRN_SUPER = 256
HGRN_SAFE_LOG = 75.0
ROPE_BASE = 10000.0
EPS = 1e-6
MASK_NEG = -1e30
F_MIN = 1e-30

GROUP_ROWS = 1024
MOD_ROWS = 8
LANES = 128
HEADS_PER_TILE = LANES // HEAD_DIM
PROJ_TN = 512
FFN_TF = 256
MOD_TN = 1536
MIB = 1024 * 1024

SH1, SC1, GT1, SH2, SC2, GT2 = range(6)

NA_QROWS = 4
NA_KROWS = 12
NA_KSTART = (0, 0, 4, 4)


def _params(sem, vmem_mib):
    return pltpu.CompilerParams(dimension_semantics=sem, vmem_limit_bytes=vmem_mib * MIB)


def _sigmoid(x):
    return 1.0 / (1.0 + jnp.exp(-x))


def _silu(x):
    return x * _sigmoid(x)


def _dot(a, b):
    return jnp.dot(a, b, preferred_element_type=F32)


def _dot_nt(a, b):
    return lax.dot_general(a, b, (((1,), (1,)), ((), ())), preferred_element_type=F32)


def _dot_tn(a, b):
    return lax.dot_general(a, b, (((0,), (0,)), ((), ())), preferred_element_type=F32)


def _mod_kernel(c_ref, w_ref, b_ref, o_ref):
    s = _silu(c_ref[...]).astype(BF16)
    o_ref[...] = _dot(s, w_ref[...].astype(BF16)) + b_ref[...]


def _modulation(cvec, w_mod, b_mod):
    rows = cvec.shape[0]
    n = w_mod.shape[-1]
    return pl.pallas_call(
        _mod_kernel,
        grid=(DEPTH, n // MOD_TN),
        in_specs=[
            pl.BlockSpec((rows, D_MODEL), lambda l, j: (0, 0)),
            pl.BlockSpec((None, D_MODEL, MOD_TN), lambda l, j: (l, 0, j)),
            pl.BlockSpec((None, 1, MOD_TN), lambda l, j: (l, 0, j)),
        ],
        out_specs=pl.BlockSpec((None, rows, MOD_TN), lambda l, j: (l, 0, j)),
        out_shape=jax.ShapeDtypeStruct((DEPTH, rows, n), F32),
        compiler_params=_params(("arbitrary", "arbitrary"), 32),
        name="modulation",
    )(cvec, w_mod, b_mod.reshape(DEPTH, 1, n))


def _norm_mod(x, g, mod, sh_row, sc_row):
    y = x * lax.rsqrt(jnp.mean(x * x, axis=-1, keepdims=True) + EPS) * g
    return y * (1.0 + mod[sc_row:sc_row + 1]) + mod[sh_row:sh_row + 1]


def _in_proj_kernel(x_ref, g_ref, mod_ref, w_ref, o_ref, h_ref):
    @pl.when(pl.program_id(1) == 0)
    def _():
        h_ref[...] = _norm_mod(x_ref[...], g_ref[...], mod_ref[...], SH1, SC1).astype(BF16)

    o_ref[...] = _dot(h_ref[...], w_ref[...]).astype(o_ref.dtype)


def _in_proj(x, g, mod, w, out_dtype):
    m, n = x.shape[0], w.shape[1]
    tm, tn = GROUP_ROWS, PROJ_TN
    return pl.pallas_call(
        _in_proj_kernel,
        grid=(m // tm, n // tn),
        in_specs=[
            pl.BlockSpec((tm, D_MODEL), lambda i, j: (i, 0)),
            pl.BlockSpec((1, D_MODEL), lambda i, j: (0, 0)),
            pl.BlockSpec((None, MOD_ROWS, D_MODEL), lambda i, j: (i, 0, 0)),
            pl.BlockSpec((D_MODEL, tn), lambda i, j: (0, j)),
        ],
        out_specs=pl.BlockSpec((tm, tn), lambda i, j: (i, j)),
        out_shape=jax.ShapeDtypeStruct((m, n), out_dtype),
        scratch_shapes=[pltpu.VMEM((tm, D_MODEL), BF16)],
        compiler_params=_params(("parallel", "arbitrary"), 40),
        name="in_proj",
    )(x, g, mod, w)


def _out_proj_kernel(*refs, n_in):
    a_refs, w_refs = refs[:n_in], refs[n_in:2 * n_in]
    x_ref, g_ref, mod_ref, o_ref = refs[2 * n_in:]
    y = _dot(a_refs[0][...], w_refs[0][...])
    for a_ref, w_ref in zip(a_refs[1:], w_refs[1:]):
        y = y + _dot(a_ref[...], w_ref[...])
    yn = y * lax.rsqrt(jnp.mean(y * y, axis=-1, keepdims=True) + EPS) * g_ref[...]
    o_ref[...] = x_ref[...] + mod_ref[GT1:GT1 + 1, :] * yn


def _out_proj(acts, w, x, g, mod):
    m = x.shape[0]
    tm = GROUP_ROWS
    n_in = len(acts)
    ks = [a.shape[1] for a in acts]
    assert sum(ks) == w.shape[0] and len(set(ks)) == 1
    in_specs = [pl.BlockSpec((tm, k), lambda i: (i, 0)) for k in ks]
    in_specs += [pl.BlockSpec((ks[0], D_MODEL), lambda i, p=p: (p, 0)) for p in range(n_in)]
    in_specs += [
        pl.BlockSpec((tm, D_MODEL), lambda i: (i, 0)),
        pl.BlockSpec((1, D_MODEL), lambda i: (0, 0)),
        pl.BlockSpec((None, MOD_ROWS, D_MODEL), lambda i: (i, 0, 0)),
    ]
    return pl.pallas_call(
        functools.partial(_out_proj_kernel, n_in=n_in),
        grid=(m // tm,),
        in_specs=in_specs,
        out_specs=pl.BlockSpec((tm, D_MODEL), lambda i: (i, 0)),
        out_shape=jax.ShapeDtypeStruct((m, D_MODEL), F32),
        compiler_params=_params(("parallel",), 48),
        name="out_proj",
    )(*acts, *([w] * n_in), x, g, mod)


def _ffn_kernel(x_ref, g2_ref, g3_ref, mod_ref, wa_ref, wu_ref, wo_ref, *rest, n_ctx_groups):
    o_refs, (h_ref, acc_ref) = rest[:-2], rest[-2:]
    f = pl.program_id(1)

    @pl.when(f == 0)
    def _():
        h_ref[...] = _norm_mod(x_ref[...], g2_ref[...], mod_ref[...], SH2, SC2).astype(BF16)
        acc_ref[...] = jnp.zeros_like(acc_ref)

    h = h_ref[...]
    a = _dot(h, wa_ref[...])
    u = _dot(h, wu_ref[...])
    acc_ref[...] += _dot((_silu(a) * u).astype(BF16), wo_ref[...])

    def finish(o_ref):
        y = acc_ref[...]
        yn = y * lax.rsqrt(jnp.mean(y * y, axis=-1, keepdims=True) + EPS) * g3_ref[...]
        o_ref[...] = (x_ref[...] + mod_ref[GT2:GT2 + 1, :] * yn).reshape(o_ref.shape)

    last = f == pl.num_programs(1) - 1
    if len(o_refs) == 1:
        pl.when(last)(functools.partial(finish, o_refs[0]))
    else:
        is_ctx = pl.program_id(0) < n_ctx_groups
        pl.when(last & is_ctx)(functools.partial(finish, o_refs[0]))
        pl.when(last & jnp.logical_not(is_ctx))(functools.partial(finish, o_refs[1]))


def _ffn(x, g2, g3, mod, w_in, w_out, split=None):
    m = x.shape[0]
    tm, tf = GROUP_ROWS, FFN_TF
    nf = D_FF // tf
    n_ctx = 0
    if split is None:
        out_specs = pl.BlockSpec((tm, D_MODEL), lambda i, f: (i, 0))
        out_shape = jax.ShapeDtypeStruct((m, D_MODEL), F32)
    else:
        (bp, tp, _), (bs, ts, _) = split
        n_ctx = bp * tp // tm
        assert ts == tm and (bp * tp) % tm == 0
        out_specs = [
            pl.BlockSpec((tm // tp, tp, D_MODEL), lambda i, f: (jnp.minimum(i, n_ctx - 1), 0, 0)),
            pl.BlockSpec((1, ts, D_MODEL), lambda i, f: (jnp.maximum(i - n_ctx, 0), 0, 0)),
        ]
        out_shape = [jax.ShapeDtypeStruct(s, F32) for s in split]
    return pl.pallas_call(
        functools.partial(_ffn_kernel, n_ctx_groups=n_ctx),
        grid=(m // tm, nf),
        in_specs=[
            pl.BlockSpec((tm, D_MODEL), lambda i, f: (i, 0)),
            pl.BlockSpec((1, D_MODEL), lambda i, f: (0, 0)),
            pl.BlockSpec((1, D_MODEL), lambda i, f: (0, 0)),
            pl.BlockSpec((None, MOD_ROWS, D_MODEL), lambda i, f: (i, 0, 0)),
            pl.BlockSpec((D_MODEL, tf), lambda i, f: (0, f)),
            pl.BlockSpec((D_MODEL, tf), lambda i, f: (0, nf + f)),
            pl.BlockSpec((tf, D_MODEL), lambda i, f: (f, 0)),
        ],
        out_specs=out_specs,
        out_shape=out_shape,
        scratch_shapes=[pltpu.VMEM((tm, D_MODEL), BF16), pltpu.VMEM((tm, D_MODEL), F32)],
        compiler_params=_params(("arbitrary", "arbitrary"), 48),
        name="ffn",
    )(x, g2, g3, mod, w_in, w_in, w_out)


def _state_spec(n_req, n_layers, layer, tail, tail_block, tail_index, n_ctx, nb):
    n_own = n_layers if layer == 0 else 1
    first = 0 if layer == 0 else layer

    def index(j, g):
        return (jnp.minimum(g, n_ctx - 1), first) + tail_index(j)

    spec = pl.BlockSpec((nb, n_own) + tail_block, index)
    shape = jax.ShapeDtypeStruct((n_req, n_layers) + tail, F32)
    return spec, shape


def _zero_other_layers(st_ref):
    if st_ref.shape[1] > 1:
        st_ref[:, 1:] = jnp.zeros((st_ref.shape[0], st_ref.shape[1] - 1) + st_ref.shape[2:], st_ref.dtype)


def _na_bias_tables(rpb):
    qc = np.arange(GRID_W)[:, None]
    kc = np.arange(GRID_W)[None, :]
    win0 = np.clip(qc - WIN_C // 2, 0, GRID_W - WIN_C)
    col_valid = (kc >= win0) & (kc < win0 + WIN_C)
    col_off = np.clip(kc - qc, 1 - WIN_C, WIN_C - 1) + (WIN_C - 1)
    tiles = jnp.where(col_valid[None, None, None], rpb.astype(F32)[:, :, :, col_off], MASK_NEG)
    nxt = jnp.concatenate([tiles[:, :, 1:], tiles[:, :, -1:]], axis=2)
    return jnp.concatenate([tiles, nxt], axis=-1)


def _na_group_bias(pair_ref, hh, g):
    rows = GROUP_ROWS // GRID_W
    n_ro = 2 * WIN_R - 1
    tk = NA_KROWS * GRID_W
    lane = lax.broadcasted_iota(jnp.int32, (GRID_W, tk), 1)
    bias, valid = [], []
    for a in range(NA_QROWS):
        r = g * NA_QROWS + a
        row0 = min(max(r - WIN_R // 2, 0), rows - WIN_R)
        pieces = []
        for w in range(0, NA_KROWS, 2):
            ro = NA_KSTART[g] + w - r + (WIN_R - 1)
            pieces.append(pair_ref[hh, min(max(ro, 0), n_ro - 1)])
        bias.append(jnp.concatenate(pieces, axis=1))
        lo = (row0 - NA_KSTART[g]) * GRID_W
        valid.append((lane >= lo) & (lane < lo + WIN_R * GRID_W))
    return jnp.concatenate(bias, axis=0), jnp.concatenate(valid, axis=0)


def _ctx_attend(q_ref, k_ref, v_ref, o_ref, kv_ref, t):
    scale = HEAD_DIM ** -0.5
    for r in range(GROUP_ROWS // t):
        rs = slice(r * t, (r + 1) * t)
        q2, k2, v2 = q_ref[rs, :], k_ref[rs, :], v_ref[rs, :]
        outs = []
        for hh in range(HEADS_PER_TILE):
            hs = slice(hh * HEAD_DIM, (hh + 1) * HEAD_DIM)
            q, k, v = q2[:, hs], k2[:, hs], v2[:, hs]
            kv_ref[r, 0, 0, hh] = k.astype(F32)
            kv_ref[r, 0, 1, hh] = v.astype(F32)
            s = _dot_nt(q, k) * scale
            p = jnp.exp(s - jnp.max(s, axis=-1, keepdims=True))
            l = jnp.sum(p, axis=-1, keepdims=True)
            outs.append(_dot(p.astype(BF16), v) / l)
        o_ref[rs, :] = jnp.concatenate(outs, axis=-1).astype(o_ref.dtype)
    _zero_other_layers(kv_ref)


def _na_attend(q_ref, k_ref, v_ref, ckv_ref, pair_ref, o_ref):
    q2, k2, v2 = q_ref[...], k_ref[...], v_ref[...]
    scale = HEAD_DIM ** -0.5
    tq = NA_QROWS * GRID_W
    tk = NA_KROWS * GRID_W
    outs = []
    for hh in range(HEADS_PER_TILE):
        hs = slice(hh * HEAD_DIM, (hh + 1) * HEAD_DIM)
        q, k, v = q2[:, hs], k2[:, hs], v2[:, hs]
        kc = ckv_ref[0, hh].astype(BF16)
        vc = ckv_ref[1, hh].astype(BF16)
        rows = []
        for g in range(len(NA_KSTART)):
            qg = q[g * tq:(g + 1) * tq]
            k0 = NA_KSTART[g] * GRID_W
            bias, valid = _na_group_bias(pair_ref, hh, g)
            s_loc = jnp.where(valid, _dot_nt(qg, k[k0:k0 + tk]) * scale + bias, MASK_NEG)
            s_ctx = _dot_nt(qg, kc) * scale
            m = jnp.maximum(jnp.max(s_loc, axis=-1, keepdims=True), jnp.max(s_ctx, axis=-1, keepdims=True))
            p_loc = jnp.exp(s_loc - m)
            p_ctx = jnp.exp(s_ctx - m)
            l = jnp.sum(p_loc, axis=-1, keepdims=True) + jnp.sum(p_ctx, axis=-1, keepdims=True)
            o = _dot(p_loc.astype(BF16), v[k0:k0 + tk]) + _dot(p_ctx.astype(BF16), vc)
            rows.append(o / l)
        outs.append(jnp.concatenate(rows, axis=0))
    o_ref[...] = jnp.concatenate(outs, axis=-1).astype(o_ref.dtype)


def _attn_kernel(q_ref, k_ref, v_ref, ckv_ref, pair_ref, *rest, t_ctx, n_ctx):
    o_ref, kv_ref = rest[-2:]
    is_ctx = pl.program_id(1) < n_ctx
    pl.when(is_ctx)(functools.partial(_ctx_attend, q_ref, k_ref, v_ref, o_ref, kv_ref, t_ctx))
    pl.when(jnp.logical_not(is_ctx))(
        functools.partial(_na_attend, q_ref, k_ref, v_ref, ckv_ref, pair_ref, o_ref))


def _attention(z, cache_kv, pair_bias, e, n_layers, prev_kv, n_ctx_req, t_ctx, n_rows):
    npair = D_A // LANES
    hp = HEADS_PER_TILE
    n_ctx = n_ctx_req * t_ctx // GROUP_ROWS
    n_grp = n_rows // GROUP_ROWS
    past = cache_kv.shape[-2]
    blk = lambda p: pl.BlockSpec((GROUP_ROWS, LANES), lambda j, g, p=p: (g, p * npair + j))
    kv_spec, kv_shape = _state_spec(n_ctx_req, n_layers, e, (2, N_HEADS_A, t_ctx, HEAD_DIM),
                                    (2, hp, t_ctx, HEAD_DIM), lambda j: (0, j, 0, 0), n_ctx, GROUP_ROWS // t_ctx)
    in_specs = [
        blk(0), blk(1), blk(2),
        pl.BlockSpec((None, None, 2, hp, past, HEAD_DIM), lambda j, g: (jnp.maximum(g - n_ctx, 0), e, 0, j, 0, 0)),
        pl.BlockSpec((None, hp) + pair_bias.shape[2:], lambda j, g: (e, j, 0, 0, 0)),
    ]
    args = [z, z, z, cache_kv, pair_bias]
    aliases = {}
    if prev_kv is not None:
        in_specs.append(pl.BlockSpec(memory_space=pl.ANY))
        args.append(prev_kv)
        aliases = {len(args) - 1: 1}
    return pl.pallas_call(
        functools.partial(_attn_kernel, t_ctx=t_ctx, n_ctx=n_ctx),
        grid=(npair, n_grp),
        in_specs=in_specs,
        out_specs=[pl.BlockSpec((GROUP_ROWS, LANES), lambda j, g: (g, j)), kv_spec],
        out_shape=[jax.ShapeDtypeStruct((n_rows, D_A), BF16), kv_shape],
        input_output_aliases=aliases,
        compiler_params=_params(("arbitrary", "arbitrary"), 56),
        name="attention",
    )(*args)


def _rope_tables(t):
    half = HEAD_DIM // 2
    nf = half // 2
    inv = ROPE_BASE ** (-np.arange(nf, dtype=np.float32) / nf)
    pos = np.arange(t)
    ang_r = (pos // GRID_W).astype(np.float32)[:, None] * inv[None, :]
    ang_c = (pos % GRID_W).astype(np.float32)[:, None] * inv[None, :]
    ang_r, ang_c = jnp.asarray(ang_r), jnp.asarray(ang_c)
    cr, sr, cc, sc = jnp.cos(ang_r), jnp.sin(ang_r), jnp.cos(ang_c), jnp.sin(ang_c)
    cos = jnp.concatenate([cr, cr, cc, cc], axis=-1)
    sin = jnp.concatenate([-sr, sr, -sc, sc], axis=-1)
    return jnp.tile(cos, (1, HEADS_PER_TILE)), jnp.tile(sin, (1, HEADS_PER_TILE))


def _rope(x, cos, sin):
    nf = HEAD_DIM // 4
    lane = lax.broadcasted_iota(jnp.int32, x.shape, 1)
    partner = jnp.where(lane % (2 * nf) < nf, pltpu.roll(x, LANES - nf, axis=1), pltpu.roll(x, nf, axis=1))
    return x * cos + partner * sin


def _decay_matrix(dm_ref, lg, t):
    n_i = lax.broadcasted_iota(jnp.int32, (t, t), 0)
    m_i = lax.broadcasted_iota(jnp.int32, (t, t), 1)
    diff = (n_i - m_i).astype(F32)
    for hh in range(HEADS_PER_TILE):
        dm_ref[hh] = (jnp.where(diff >= 0, jnp.exp(lg[hh][0] * jnp.maximum(diff, 0.0)), 0.0)
                      + jnp.where(diff <= 0, jnp.exp(lg[hh][1] * jnp.maximum(-diff, 0.0)), 0.0))


def _retain(q_ref, k_ref, v_ref, g_ref, o_ref, dm_ref, lg, t, *, rope=None, s0_ref=None, st_ref=None):
    pos = lax.broadcasted_iota(jnp.int32, (t, 1), 0).astype(F32)
    for r in range(GROUP_ROWS // t):
        rs = slice(r * t, (r + 1) * t)
        q2 = q_ref[rs, :].astype(F32)
        k2 = k_ref[rs, :].astype(F32) * (HEAD_DIM ** -0.5)
        v2, g2 = v_ref[rs, :], g_ref[rs, :].astype(F32)
        if rope is not None:
            q2 = _rope(q2, rope[0][...], rope[1][...])
            k2 = _rope(k2, rope[0][...], rope[1][...])
        outs = []
        for hh in range(HEADS_PER_TILE):
            hs = slice(hh * HEAD_DIM, (hh + 1) * HEAD_DIM)
            lg_f, lg_b = lg[hh]
            q, k, vb = q2[:, hs], k2[:, hs], v2[:, hs]
            qb, kb = q.astype(BF16), k.astype(BF16)
            att = _dot_nt(qb, kb) * dm_ref[hh]
            o = _dot(att.astype(BF16), vb)
            if s0_ref is not None:
                qs = jnp.concatenate([q * jnp.exp(lg_f * (pos + 1.0)), q * jnp.exp(lg_b * (t - pos))], axis=-1)
                s0 = jnp.concatenate([s0_ref[0, hh], s0_ref[1, hh]], axis=0)
                o = o + _dot(qs.astype(BF16), s0.astype(BF16))
            if st_ref is not None:
                kf = k * jnp.exp(lg_f * (t - 1.0 - pos))
                kr = k * jnp.exp(lg_b * pos)
                st_ref[r, 0, 0, hh] = _dot_tn(kf.astype(BF16), vb)
                st_ref[r, 0, 1, hh] = _dot_tn(kr.astype(BF16), vb)
            on = o * lax.rsqrt(jnp.mean(o * o, axis=-1, keepdims=True) + EPS)
            outs.append(on * _silu(g2[:, hs]))
        o_ref[rs, :] = jnp.concatenate(outs, axis=-1).astype(o_ref.dtype)
    if st_ref is not None:
        _zero_other_layers(st_ref)


def _ret_kernel(dec_ref, q_ref, k_ref, v_ref, g_ref, cos_ref, sin_ref, s0_ref, *rest, t_ctx, t_lat, n_ctx):
    o_ref, st_ref, dmc_ref, dml_ref = rest[-4:]
    j, g = pl.program_id(0), pl.program_id(1)
    lg = [[-jnp.exp(jnp.full((1, 1), dec_ref[d, j * HEADS_PER_TILE + hh], F32)) for d in range(2)]
          for hh in range(HEADS_PER_TILE)]
    pl.when(g == 0)(functools.partial(_decay_matrix, dmc_ref, lg, t_ctx))
    pl.when(g == n_ctx)(functools.partial(_decay_matrix, dml_ref, lg, t_lat))
    is_ctx = g < n_ctx
    pl.when(is_ctx)(functools.partial(
        _retain, q_ref, k_ref, v_ref, g_ref, o_ref, dmc_ref, lg, t_ctx, st_ref=st_ref))
    pl.when(jnp.logical_not(is_ctx))(functools.partial(
        _retain, q_ref, k_ref, v_ref, g_ref, o_ref, dml_ref, lg, t_lat, rope=(cos_ref, sin_ref), s0_ref=s0_ref))


def _retention(z, ret_decay, rope_tabs, state_ret, e, n_layers, prev_st, n_ctx_req, t_ctx, t_lat, n_rows):
    npair = D_B // LANES
    hp = HEADS_PER_TILE
    sec0 = 3 * D_A // LANES
    n_ctx = n_ctx_req * t_ctx // GROUP_ROWS
    n_grp = n_rows // GROUP_ROWS
    assert t_lat == GROUP_ROWS
    blk = lambda p: pl.BlockSpec((GROUP_ROWS, LANES), lambda j, g, p=p: (g, sec0 + p * npair + j))
    st_spec, st_shape = _state_spec(n_ctx_req, n_layers, e, (2, N_HEADS_B, HEAD_DIM, HEAD_DIM),
                                    (2, hp, HEAD_DIM, HEAD_DIM), lambda j: (0, j, 0, 0), n_ctx,
                                    GROUP_ROWS // t_ctx)
    in_specs = [
        pl.BlockSpec(memory_space=pltpu.SMEM), blk(0), blk(1), blk(2), blk(3),
        pl.BlockSpec((t_lat, LANES), lambda j, g: (0, 0)),
        pl.BlockSpec((t_lat, LANES), lambda j, g: (0, 0)),
        pl.BlockSpec((None, None, 2, hp, HEAD_DIM, HEAD_DIM),
                     lambda j, g: (jnp.maximum(g - n_ctx, 0), e, 0, j, 0, 0)),
    ]
    args = [ret_decay, z, z, z, z, rope_tabs[0], rope_tabs[1], state_ret]
    aliases = {}
    if prev_st is not None:
        in_specs.append(pl.BlockSpec(memory_space=pl.ANY))
        args.append(prev_st)
        aliases = {len(args) - 1: 1}
    return pl.pallas_call(
        functools.partial(_ret_kernel, t_ctx=t_ctx, t_lat=t_lat, n_ctx=n_ctx),
        grid=(npair, n_grp),
        in_specs=in_specs,
        out_specs=[pl.BlockSpec((GROUP_ROWS, LANES), lambda j, g: (g, j)), st_spec],
        out_shape=[jax.ShapeDtypeStruct((n_rows, D_B), BF16), st_shape],
        input_output_aliases=aliases,
        scratch_shapes=[pltpu.VMEM((hp, t_ctx, t_ctx), F32), pltpu.VMEM((hp, t_lat, t_lat), F32)],
        compiler_params=_params(("arbitrary", "arbitrary"), 56),
        name="retention",
    )(*args)


def _seg_scan(x, reverse):
    t = x.shape[0]
    row = lax.broadcasted_iota(jnp.int32, (t, 1), 0) % HGRN_CHUNK
    sft = 1
    while sft < HGRN_CHUNK:
        if reverse:
            x = x + jnp.where(row < HGRN_CHUNK - sft, pltpu.roll(x, t - sft, axis=0), 0.0)
        else:
            x = x + jnp.where(row >= sft, pltpu.roll(x, sft, axis=0), 0.0)
        sft *= 2
    return x


def _hgrn_intra(q, k, v, c, reverse):
    cs = HGRN_CHUNK
    sub = 8
    row = lax.broadcasted_iota(jnp.int32, (cs, 1), 0)
    parts = [jnp.zeros((sub, v.shape[1]), F32) for _ in range(cs // sub)]
    for s in range(cs):
        blk_s = s // sub
        blks = range(0, blk_s + 1) if reverse else range(blk_s, cs // sub)
        c_s, k_s, v_s = c[s:s + 1], k[s:s + 1], v[s:s + 1]
        for bt in blks:
            rs = slice(bt * sub, (bt + 1) * sub)
            w = q[rs] * k_s * jnp.exp(jnp.minimum(c[rs] - c_s, 0.0))
            col = jnp.sum(w, axis=-1, keepdims=True)
            if bt == blk_s:
                keep = (row[rs] <= s) if reverse else (row[rs] >= s)
                col = jnp.where(keep, col, 0.0)
            parts[bt] = parts[bt] + col * v_s
    return jnp.concatenate(parts, axis=0)


def _hgrn_states(qe_ref, ke_ref, vb_ref, ee_ref, oi_ref, t, s0_ref=None, st_ref=None):
    sb = HGRN_SUPER
    n_sb = t // sb
    for r in range(GROUP_ROWS // t):
        for d in range(2):
            st = s0_ref[d].T if s0_ref is not None else None
            order = range(n_sb) if d == 0 else reversed(range(n_sb))
            for j in order:
                blk = r * n_sb + j
                rows = slice(blk * sb, (blk + 1) * sb)
                upd = _dot_tn(vb_ref[rows, :], ke_ref[d, rows, :])
                if st is None:
                    st = upd
                else:
                    oi_ref[rows, :] += _dot_nt(qe_ref[d, rows, :], st.astype(BF16))
                    st = st * ee_ref[d, blk][0:1] + upd
            if st_ref is not None:
                st_ref[r, 0, d] = st.T
    if st_ref is not None:
        _zero_other_layers(st_ref)


def _hgrn_kernel(q_ref, i_ref, g_ref, ff_ref, fb_ref, lb_ref, gn_ref, s0_ref, *rest, t_ctx, t_lat, n_ctx):
    o_ref, st_ref, qe_ref, ke_ref, vb_ref, ee_ref, oi_ref, sq_ref, sk_ref, sc_ref = rest[-10:]
    cs = HGRN_CHUNK
    sb = HGRN_SUPER
    row = lax.broadcasted_iota(jnp.int32, (sb, 1), 0)
    ti = lax.broadcasted_iota(jnp.int32, (sb, sb), 0)
    si = lax.broadcasted_iota(jnp.int32, (sb, sb), 1)
    term = jnp.where((ti >> 5) == (si >> 5), 0,
                     jnp.where((ti >> 6) == (si >> 6), 1, jnp.where((ti >> 7) == (si >> 7), 2, 3)))
    term_dir = (jnp.where(si <= ti, term, 4), jnp.where(si >= ti, term, 4))

    def block_step(blk, carry):
        rows = pl.ds(pl.multiple_of(blk * sb, sb), sb)
        qs = _silu(q_ref[rows, :].astype(F32)) * (DK_C ** -0.5)
        v = i_ref[rows, :]
        vb_ref[rows, :] = v
        per_dir = []
        c_min = None
        for d, f_ref in enumerate((ff_ref, fb_ref)):
            lb = lb_ref[d:d + 1, :]
            fr = f_ref[rows, :]
            e = jnp.exp(-jnp.abs(fr))
            r = 1.0 / (1.0 + e)
            sig_pos = jnp.where(fr >= 0, r, e * r)
            sig_neg = jnp.where(fr >= 0, e * r, r)
            f = lb + (1.0 - lb) * sig_pos
            k = (1.0 - lb) * sig_neg
            c = _seg_scan(jnp.log(jnp.maximum(f, F_MIN)), reverse=(d == 1))
            c3 = c.reshape(sb // cs, cs, DK_C)
            c_end = c3[:, cs - 1:cs, :] if d == 0 else c3[:, 0:1, :]
            tot = jnp.broadcast_to(c_end, c3.shape).reshape(sb, DK_C)
            per_dir.append((k, c, tot))
            m = jnp.min(c)
            c_min = m if c_min is None else jnp.minimum(c_min, m)

        safe = c_min >= -HGRN_SAFE_LOG
        att = None
        for d, (k, c, tot) in enumerate(per_dir):
            q_l = qs * jnp.exp(c)
            k_l = k * jnp.exp(tot - c)
            e_l = jnp.exp(tot)
            k_hat = jnp.where(safe, k * jnp.exp(-c), 0.0).astype(BF16)
            prods = [_dot_nt(q_l.astype(BF16), k_hat), _dot_nt(q_l.astype(BF16), k_l.astype(BF16))]
            size = cs
            while size < sb:
                prev_e = pltpu.roll(e_l, size, axis=0)
                next_e = pltpu.roll(e_l, sb - size, axis=0)
                second = (row % (2 * size)) >= size
                if d == 0:
                    q_l = q_l * jnp.where(second, prev_e, 1.0)
                    k_l = k_l * jnp.where(second, 1.0, next_e)
                else:
                    q_l = q_l * jnp.where(second, 1.0, next_e)
                    k_l = k_l * jnp.where(second, prev_e, 1.0)
                e_l = e_l * jnp.where(second, prev_e, next_e)
                size *= 2
                if size < sb:
                    prods.append(_dot_nt(q_l.astype(BF16), k_l.astype(BF16)))
            qe_ref[d, rows, :] = q_l.astype(BF16)
            ke_ref[d, rows, :] = k_l.astype(BF16)
            ee_ref[d, blk] = e_l[0:8]
            sel = jnp.zeros((sb, sb), F32)
            for i in reversed(range(len(prods))):
                sel = jnp.where(term_dir[d] == i, prods[i], sel)
            att = sel if att is None else att + sel
        oi_ref[rows, :] = _dot(att.astype(BF16), v)

        @pl.when(jnp.logical_not(safe))
        def _():
            sq_ref[...] = qs
            for d in range(2):
                sk_ref[d] = per_dir[d][0]
                sc_ref[d] = per_dir[d][1]

            def chunk_step(i, carry2):
                crow = pl.ds(pl.multiple_of(i * cs, cs), cs)
                orow = pl.ds(pl.multiple_of(blk * sb + i * cs, cs), cs)
                q, vv = sq_ref[crow, :], i_ref[orow, :].astype(F32)
                oi_ref[orow, :] += (_hgrn_intra(q, sk_ref[0, crow, :], vv, sc_ref[0, crow, :], reverse=False)
                                    + _hgrn_intra(q, sk_ref[1, crow, :], vv, sc_ref[1, crow, :], reverse=True))
                return carry2

            lax.fori_loop(0, sb // cs, chunk_step, 0)

        return carry

    lax.fori_loop(0, GROUP_ROWS // sb, block_step, 0)

    is_ctx = pl.program_id(1) < n_ctx
    pl.when(is_ctx)(functools.partial(
        _hgrn_states, qe_ref, ke_ref, vb_ref, ee_ref, oi_ref, t_ctx, st_ref=st_ref))
    pl.when(jnp.logical_not(is_ctx))(functools.partial(
        _hgrn_states, qe_ref, ke_ref, vb_ref, ee_ref, oi_ref, t_lat, s0_ref=s0_ref))

    o = oi_ref[...]
    on = o * lax.rsqrt(jnp.mean(o * o, axis=-1, keepdims=True) + EPS) * gn_ref[...]
    o_ref[...] = (on * _silu(g_ref[...].astype(F32))).astype(o_ref.dtype)


def _hgrn(z16, z32, lower, gnorm, state_hgrn, oi, n_layers, prev_st, n_ctx_req, t_ctx, t_lat, n_rows):
    nh = N_HEADS_C
    n_ctx = n_ctx_req * t_ctx // GROUP_ROWS
    n_grp = n_rows // GROUP_ROWS
    rows = GROUP_ROWS
    assert t_lat == GROUP_ROWS and t_ctx % HGRN_SUPER == 0
    blk = lambda p: pl.BlockSpec((rows, DK_C), lambda h, g, p=p: (g, p * nh + h))
    st_spec, st_shape = _state_spec(n_ctx_req, n_layers, oi, (2, nh, DK_C, DK_C),
                                    (2, None, DK_C, DK_C), lambda h: (0, h, 0, 0), n_ctx, GROUP_ROWS // t_ctx)
    in_specs = [blk(0), blk(1), blk(2), blk(0), blk(1),
                pl.BlockSpec((2, DK_C), lambda h, g: (0, h)),
                pl.BlockSpec((1, DK_C), lambda h, g: (0, 0)),
                pl.BlockSpec((None, None, 2, None, DK_C, DK_C),
                             lambda h, g: (jnp.maximum(g - n_ctx, 0), oi, 0, h, 0, 0))]
    args = [z16, z16, z16, z32, z32, lower, gnorm, state_hgrn]
    aliases = {}
    if prev_st is not None:
        in_specs.append(pl.BlockSpec(memory_space=pl.ANY))
        args.append(prev_st)
        aliases = {len(args) - 1: 1}
    return pl.pallas_call(
        functools.partial(_hgrn_kernel, t_ctx=t_ctx, t_lat=t_lat, n_ctx=n_ctx),
        grid=(nh, n_grp),
        in_specs=in_specs,
        out_specs=[pl.BlockSpec((rows, DK_C), lambda h, g: (g, h)), st_spec],
        out_shape=[jax.ShapeDtypeStruct((n_rows, D_MODEL), BF16), st_shape],
        input_output_aliases=aliases,
        scratch_shapes=[
            pltpu.VMEM((2, rows, DK_C), BF16),
            pltpu.VMEM((2, rows, DK_C), BF16),
            pltpu.VMEM((rows, DK_C), BF16),
            pltpu.VMEM((2, rows // HGRN_SUPER, 8, DK_C), F32),
            pltpu.VMEM((rows, DK_C), F32),
            pltpu.VMEM((HGRN_SUPER, DK_C), F32),
            pltpu.VMEM((2, HGRN_SUPER, DK_C), F32),
            pltpu.VMEM((2, HGRN_SUPER, DK_C), F32),
        ],
        compiler_params=_params(("arbitrary", "arbitrary"), 32),
        name="hgrn",
    )(*args)


def kernel(x_prompt, x_sample, cache_kv, state_ret, state_hgrn, c, c_ctx, w_mod, b_mod, norm_g,
           w_in_even, w_out_even, rpb, ret_decay, w_in_odd, w_out_odd, hgrn_lb, hgrn_gnorm,
           w_ffn_in, w_ffn_out):
    bp, tp, _ = x_prompt.shape
    bs, ts, _ = x_sample.shape
    np_rows, ns_rows = bp * tp, bs * ts
    n_rows = np_rows + ns_rows
    n_even, n_odd = w_in_even.shape[0], w_in_odd.shape[0]
    assert np_rows % GROUP_ROWS == 0 and ts == GROUP_ROWS
    x = jnp.concatenate([x_prompt.reshape(np_rows, D_MODEL), x_sample.reshape(ns_rows, D_MODEL)], axis=0)

    n_c = bs + 1
    pad = (-n_c) % 8
    cvec = jnp.concatenate([c, c_ctx[None], jnp.zeros((pad, D_MODEL), F32)], axis=0)
    mod_all = _modulation(cvec, w_mod, b_mod)
    grp = np.concatenate([np.full(np_rows // GROUP_ROWS, bs), np.arange(bs)]).astype(np.int32)
    mod_all = mod_all[:, grp].reshape(DEPTH, len(grp), 6, D_MODEL)
    mod_all = jnp.pad(mod_all, ((0, 0), (0, 0), (0, MOD_ROWS - 6), (0, 0)))

    p_lb = jax.nn.softmax(hgrn_lb.astype(F32), axis=0)
    lower = jnp.clip(jnp.cumsum(p_lb, axis=0) - p_lb[0], 0.0, 1.0)
    pair_bias = _na_bias_tables(rpb)
    rope_tabs = _rope_tables(ts)

    w_in_even_b, w_out_even_b = w_in_even.astype(BF16), w_out_even.astype(BF16)
    w_out_odd_b = w_out_odd.astype(BF16)
    w_in_odd16 = jnp.concatenate([w_in_odd[:, :, :D_C], w_in_odd[:, :, 3 * D_C:]], axis=-1).astype(BF16)
    w_in_odd32 = w_in_odd[:, :, D_C:3 * D_C].astype(BF16)
    w_ffn_in_b, w_ffn_out_b = w_ffn_in.astype(BF16), w_ffn_out.astype(BF16)

    kv_out = ret_out = hg_out = None
    y_split = None
    for l in range(DEPTH):
        mod = mod_all[l]
        g = norm_g[l].reshape(4, 1, D_MODEL)
        if l % 2 == 0:
            e = l // 2
            z = _in_proj(x, g[0], mod, w_in_even_b[e], BF16)
            oa, kv_out = _attention(z, cache_kv, pair_bias, e, n_even, kv_out, bp, tp, n_rows)
            ob, ret_out = _retention(z, ret_decay[e], rope_tabs, state_ret, e, n_even, ret_out, bp, tp, ts, n_rows)
            x = _out_proj([oa, ob], w_out_even_b[e], x, g[1], mod)
        else:
            oi = l // 2
            z16 = _in_proj(x, g[0], mod, w_in_odd16[oi], BF16)
            z32 = _in_proj(x, g[0], mod, w_in_odd32[oi], F32)
            gn = hgrn_gnorm[oi].reshape(1, DK_C)
            o, hg_out = _hgrn(z16, z32, lower[oi], gn, state_hgrn, oi, n_odd, hg_out, bp, tp, ts, n_rows)
            x = _out_proj([o], w_out_odd_b[oi], x, g[1], mod)
        if l == DEPTH - 1:
            y_split = _ffn(x, g[2], g[3], mod, w_ffn_in_b[l], w_ffn_out_b[l],
                           split=(x_prompt.shape, x_sample.shape))
        else:
            x = _ffn(x, g[2], g[3], mod, w_ffn_in_b[l], w_ffn_out_b[l])

    return (y_split[0], y_split[1], kv_out, ret_out, hg_out)
```

```python
import functools

import numpy as np
import jax
import jax.numpy as jnp
from jax import lax
from jax.experimental import pallas as pl
from jax.experimental.pallas import tpu as pltpu

F32 = jnp.float32
BF16 = jnp.bfloat16

D_MODEL = 1024
DEPTH = 4
GRID_W = 64
HEAD_DIM = 64
N_HEADS_A = 8
N_HEADS_B = 8
D_A = N_HEADS_A * HEAD_DIM
D_B = N_HEADS_B * HEAD_DIM
WIN_R = 8
WIN_C = 16
N_HEADS_C = 8
DK_C = D_MODEL // N_HEADS_C
D_C = N_HEADS_C * DK_C
D_FF = ((8 * D_MODEL // 3 + 255) // 256) * 256
HGRN_CHUNK = 32
HGRN_SUPER = 256
HGRN_SAFE_LOG = 75.0
ROPE_BASE = 10000.0
EPS = 1e-6
MASK_NEG = -1e30
F_MIN = 1e-30

GROUP_ROWS = 1024
MOD_ROWS = 8
LANES = 128
HEADS_PER_TILE = LANES // HEAD_DIM
PROJ_TN = 512
FFN_TF = 256
MOD_TN = 1536
MIB = 1024 * 1024

SH1, SC1, GT1, SH2, SC2, GT2 = range(6)

NA_QROWS = 4
NA_KROWS = 12
NA_KSTART = (0, 0, 4, 4)


def _params(sem, vmem_mib):
    return pltpu.CompilerParams(dimension_semantics=sem, vmem_limit_bytes=vmem_mib * MIB)


def _sigmoid(x):
    return 1.0 / (1.0 + jnp.exp(-x))


def _silu(x):
    return x * _sigmoid(x)


def _dot(a, b):
    return jnp.dot(a, b, preferred_element_type=F32)


def _dot_nt(a, b):
    return lax.dot_general(a, b, (((1,), (1,)), ((), ())), preferred_element_type=F32)


def _dot_tn(a, b):
    return lax.dot_general(a, b, (((0,), (0,)), ((), ())), preferred_element_type=F32)


def _mod_kernel(c_ref, w_ref, b_ref, o_ref):
    s = _silu(c_ref[...]).astype(BF16)
    o_ref[...] = _dot(s, w_ref[...].astype(BF16)) + b_ref[...]


def _modulation(cvec, w_mod, b_mod):
    rows = cvec.shape[0]
    n = w_mod.shape[-1]
    return pl.pallas_call(
        _mod_kernel,
        grid=(DEPTH, n // MOD_TN),
        in_specs=[
            pl.BlockSpec((rows, D_MODEL), lambda l, j: (0, 0)),
            pl.BlockSpec((None, D_MODEL, MOD_TN), lambda l, j: (l, 0, j)),
            pl.BlockSpec((None, 1, MOD_TN), lambda l, j: (l, 0, j)),
        ],
        out_specs=pl.BlockSpec((None, rows, MOD_TN), lambda l, j: (l, 0, j)),
        out_shape=jax.ShapeDtypeStruct((DEPTH, rows, n), F32),
        compiler_params=_params(("arbitrary", "arbitrary"), 32),
        name="modulation",
    )(cvec, w_mod, b_mod.reshape(DEPTH, 1, n))


def _norm_mod(x, g, mod, sh_row, sc_row):
    y = x * lax.rsqrt(jnp.mean(x * x, axis=-1, keepdims=True) + EPS) * g
    return y * (1.0 + mod[sc_row:sc_row + 1]) + mod[sh_row:sh_row + 1]


def _in_proj_kernel(x_ref, g_ref, mod_ref, w_ref, o_ref):
    h = _norm_mod(x_ref[...], g_ref[...], mod_ref[...], SH1, SC1).astype(BF16)
    for j in range(o_ref.shape[1] // PROJ_TN):
        cols = slice(j * PROJ_TN, (j + 1) * PROJ_TN)
        o_ref[:, cols] = _dot(h, w_ref[:, cols]).astype(o_ref.dtype)


def _resident(block_shape):
    return pl.BlockSpec(block_shape, lambda *_: (0,) * len(block_shape), pipeline_mode=pl.Buffered(1))


def _in_proj(x, g, mod, w, out_dtype):
    m, n = x.shape[0], w.shape[1]
    tm = GROUP_ROWS
    assert n % PROJ_TN == 0
    return pl.pallas_call(
        _in_proj_kernel,
        grid=(m // tm,),
        in_specs=[
            pl.BlockSpec((tm, D_MODEL), lambda i: (i, 0)),
            pl.BlockSpec((1, D_MODEL), lambda i: (0, 0)),
            pl.BlockSpec((None, MOD_ROWS, D_MODEL), lambda i: (i, 0, 0)),
            _resident((D_MODEL, n)),
        ],
        out_specs=pl.BlockSpec((tm, n), lambda i: (i, 0)),
        out_shape=jax.ShapeDtypeStruct((m, n), out_dtype),
        compiler_params=_params(("parallel",), 48),
        name="in_proj",
    )(x, g, mod, w)


def _out_proj_kernel(*refs, n_in):
    a_refs, w_refs = refs[:n_in], refs[n_in:2 * n_in]
    x_ref, g_ref, mod_ref, o_ref = refs[2 * n_in:]
    y = _dot(a_refs[0][...], w_refs[0][...])
    for a_ref, w_ref in zip(a_refs[1:], w_refs[1:]):
        y = y + _dot(a_ref[...], w_ref[...])
    yn = y * lax.rsqrt(jnp.mean(y * y, axis=-1, keepdims=True) + EPS) * g_ref[...]
    o_ref[...] = x_ref[...] + mod_ref[GT1:GT1 + 1, :] * yn


def _out_proj(acts, w, x, g, mod):
    m = x.shape[0]
    tm = GROUP_ROWS
    n_in = len(acts)
    ks = [a.shape[1] for a in acts]
    assert sum(ks) == w.shape[0] and len(set(ks)) == 1
    in_specs = [pl.BlockSpec((tm, k), lambda i: (i, 0)) for k in ks]
    in_specs += [pl.BlockSpec((ks[0], D_MODEL), lambda i, p=p: (p, 0)) for p in range(n_in)]
    in_specs += [
        pl.BlockSpec((tm, D_MODEL), lambda i: (i, 0)),
        pl.BlockSpec((1, D_MODEL), lambda i: (0, 0)),
        pl.BlockSpec((None, MOD_ROWS, D_MODEL), lambda i: (i, 0, 0)),
    ]
    return pl.pallas_call(
        functools.partial(_out_proj_kernel, n_in=n_in),
        grid=(m // tm,),
        in_specs=in_specs,
        out_specs=pl.BlockSpec((tm, D_MODEL), lambda i: (i, 0)),
        out_shape=jax.ShapeDtypeStruct((m, D_MODEL), F32),
        compiler_params=_params(("parallel",), 48),
        name="out_proj",
    )(*acts, *([w] * n_in), x, g, mod)


def _ffn_kernel(x_ref, g2_ref, g3_ref, mod_ref, wa_ref, wu_ref, wo_ref, *rest, n_ctx_groups):
    o_refs, (h_ref, acc_ref) = rest[:-2], rest[-2:]
    h_ref[...] = _norm_mod(x_ref[...], g2_ref[...], mod_ref[...], SH2, SC2).astype(BF16)
    acc_ref[...] = jnp.zeros_like(acc_ref)

    def hidden_step(f, carry):
        h = h_ref[...]
        a = _dot(h, wa_ref[f])
        u = _dot(h, wu_ref[f])
        acc_ref[...] += _dot((_silu(a) * u).astype(BF16), wo_ref[f])
        return carry

    lax.fori_loop(0, wa_ref.shape[0], hidden_step, 0)

    def finish(o_ref):
        y = acc_ref[...]
        yn = y * lax.rsqrt(jnp.mean(y * y, axis=-1, keepdims=True) + EPS) * g3_ref[...]
        o_ref[...] = (x_ref[...] + mod_ref[GT2:GT2 + 1, :] * yn).reshape(o_ref.shape)

    if len(o_refs) == 1:
        finish(o_refs[0])
    else:
        is_ctx = pl.program_id(0) < n_ctx_groups
        pl.when(is_ctx)(functools.partial(finish, o_refs[0]))
        pl.when(jnp.logical_not(is_ctx))(functools.partial(finish, o_refs[1]))


def _ffn_weights(w_in, w_out):
    nf = D_FF // FFN_TF
    w_in = w_in.astype(BF16).reshape(D_MODEL, 2, nf, FFN_TF).transpose(1, 2, 0, 3)
    return w_in[0], w_in[1], w_out.astype(BF16).reshape(nf, FFN_TF, D_MODEL)


def _ffn(x, g2, g3, mod, wa, wu, wo, split=None):
    m = x.shape[0]
    tm = GROUP_ROWS
    n_ctx = 0
    if split is None:
        out_specs = pl.BlockSpec((tm, D_MODEL), lambda i: (i, 0))
        out_shape = jax.ShapeDtypeStruct((m, D_MODEL), F32)
    else:
        (bp, tp, _), (bs, ts, _) = split
        n_ctx = bp * tp // tm
        assert ts == tm and (bp * tp) % tm == 0
        out_specs = [
            pl.BlockSpec((tm // tp, tp, D_MODEL), lambda i: (jnp.minimum(i, n_ctx - 1), 0, 0)),
            pl.BlockSpec((1, ts, D_MODEL), lambda i: (jnp.maximum(i - n_ctx, 0), 0, 0)),
        ]
        out_shape = [jax.ShapeDtypeStruct(s, F32) for s in split]
    return pl.pallas_call(
        functools.partial(_ffn_kernel, n_ctx_groups=n_ctx),
        grid=(m // tm,),
        in_specs=[
            pl.BlockSpec((tm, D_MODEL), lambda i: (i, 0)),
            pl.BlockSpec((1, D_MODEL), lambda i: (0, 0)),
            pl.BlockSpec((1, D_MODEL), lambda i: (0, 0)),
            pl.BlockSpec((None, MOD_ROWS, D_MODEL), lambda i: (i, 0, 0)),
            _resident(wa.shape), _resident(wu.shape), _resident(wo.shape),
        ],
        out_specs=out_specs,
        out_shape=out_shape,
        scratch_shapes=[pltpu.VMEM((tm, D_MODEL), BF16), pltpu.VMEM((tm, D_MODEL), F32)],
        compiler_params=_params(("arbitrary",), 56),
        name="ffn",
    )(x, g2, g3, mod, wa, wu, wo)


def _state_spec(n_req, n_layers, layer, tail, tail_block, tail_index, n_ctx, nb):
    n_own = n_layers if layer == 0 else 1
    first = 0 if layer == 0 else layer

    def index(j, g):
        return (jnp.minimum(g, n_ctx - 1), first) + tail_index(j)

    spec = pl.BlockSpec((nb, n_own) + tail_block, index)
    shape = jax.ShapeDtypeStruct((n_req, n_layers) + tail, F32)
    return spec, shape


def _zero_other_layers(st_ref):
    if st_ref.shape[1] > 1:
        st_ref[:, 1:] = jnp.zeros((st_ref.shape[0], st_ref.shape[1] - 1) + st_ref.shape[2:], st_ref.dtype)


def _na_bias_tables(rpb):
    qc = np.arange(GRID_W)[:, None]
    kc = np.arange(GRID_W)[None, :]
    win0 = np.clip(qc - WIN_C // 2, 0, GRID_W - WIN_C)
    col_valid = (kc >= win0) & (kc < win0 + WIN_C)
    col_off = np.clip(kc - qc, 1 - WIN_C, WIN_C - 1) + (WIN_C - 1)
    tiles = jnp.where(col_valid[None, None, None], rpb.astype(F32)[:, :, :, col_off], MASK_NEG)
    nxt = jnp.concatenate([tiles[:, :, 1:], tiles[:, :, -1:]], axis=2)
    return jnp.concatenate([tiles, nxt], axis=-1)


def _na_group_bias(pair_ref, hh, g):
    rows = GROUP_ROWS // GRID_W
    n_ro = 2 * WIN_R - 1
    tk = NA_KROWS * GRID_W
    lane = lax.broadcasted_iota(jnp.int32, (GRID_W, tk), 1)
    bias, valid = [], []
    for a in range(NA_QROWS):
        r = g * NA_QROWS + a
        row0 = min(max(r - WIN_R // 2, 0), rows - WIN_R)
        pieces = []
        for w in range(0, NA_KROWS, 2):
            ro = NA_KSTART[g] + w - r + (WIN_R - 1)
            pieces.append(pair_ref[hh, min(max(ro, 0), n_ro - 1)])
        bias.append(jnp.concatenate(pieces, axis=1))
        lo = (row0 - NA_KSTART[g]) * GRID_W
        valid.append((lane >= lo) & (lane < lo + WIN_R * GRID_W))
    return jnp.concatenate(bias, axis=0), jnp.concatenate(valid, axis=0)


def _ctx_attend(q_ref, k_ref, v_ref, o_ref, kv_ref, t):
    scale = HEAD_DIM ** -0.5
    for r in range(GROUP_ROWS // t):
        rs = slice(r * t, (r + 1) * t)
        q2, k2, v2 = q_ref[rs, :], k_ref[rs, :], v_ref[rs, :]
        outs = []
        for hh in range(HEADS_PER_TILE):
            hs = slice(hh * HEAD_DIM, (hh + 1) * HEAD_DIM)
            q, k, v = q2[:, hs], k2[:, hs], v2[:, hs]
            kv_ref[r, 0, 0, hh] = k.astype(F32)
            kv_ref[r, 0, 1, hh] = v.astype(F32)
            s = _dot_nt(q, k) * scale
            p = jnp.exp(s - jnp.max(s, axis=-1, keepdims=True))
            l = jnp.sum(p, axis=-1, keepdims=True)
            outs.append(_dot(p.astype(BF16), v) / l)
        o_ref[rs, :] = jnp.concatenate(outs, axis=-1).astype(o_ref.dtype)
    _zero_other_layers(kv_ref)


def _na_attend(q_ref, k_ref, v_ref, ckv_ref, pair_ref, o_ref):
    q2, k2, v2 = q_ref[...], k_ref[...], v_ref[...]
    scale = HEAD_DIM ** -0.5
    tq = NA_QROWS * GRID_W
    tk = NA_KROWS * GRID_W
    outs = []
    for hh in range(HEADS_PER_TILE):
        hs = slice(hh * HEAD_DIM, (hh + 1) * HEAD_DIM)
        q, k, v = q2[:, hs], k2[:, hs], v2[:, hs]
        kc = ckv_ref[0, hh].astype(BF16)
        vc = ckv_ref[1, hh].astype(BF16)
        rows = []
        for g in range(len(NA_KSTART)):
            qg = q[g * tq:(g + 1) * tq]
            k0 = NA_KSTART[g] * GRID_W
            bias, valid = _na_group_bias(pair_ref, hh, g)
            s_loc = jnp.where(valid, _dot_nt(qg, k[k0:k0 + tk]) * scale + bias, MASK_NEG)
            s_ctx = _dot_nt(qg, kc) * scale
            m = jnp.maximum(jnp.max(s_loc, axis=-1, keepdims=True), jnp.max(s_ctx, axis=-1, keepdims=True))
            p_loc = jnp.exp(s_loc - m)
            p_ctx = jnp.exp(s_ctx - m)
            l = jnp.sum(p_loc, axis=-1, keepdims=True) + jnp.sum(p_ctx, axis=-1, keepdims=True)
            o = _dot(p_loc.astype(BF16), v[k0:k0 + tk]) + _dot(p_ctx.astype(BF16), vc)
            rows.append(o / l)
        outs.append(jnp.concatenate(rows, axis=0))
    o_ref[...] = jnp.concatenate(outs, axis=-1).astype(o_ref.dtype)


def _attn_kernel(q_ref, k_ref, v_ref, ckv_ref, pair_ref, *rest, t_ctx, n_ctx):
    o_ref, kv_ref = rest[-2:]
    is_ctx = pl.program_id(1) < n_ctx
    pl.when(is_ctx)(functools.partial(_ctx_attend, q_ref, k_ref, v_ref, o_ref, kv_ref, t_ctx))
    pl.when(jnp.logical_not(is_ctx))(
        functools.partial(_na_attend, q_ref, k_ref, v_ref, ckv_ref, pair_ref, o_ref))


def _attention(z, cache_kv, pair_bias, e, n_layers, prev_kv, n_ctx_req, t_ctx, n_rows):
    npair = D_A // LANES
    hp = HEADS_PER_TILE
    n_ctx = n_ctx_req * t_ctx // GROUP_ROWS
    n_grp = n_rows // GROUP_ROWS
    past = cache_kv.shape[-2]
    blk = lambda p: pl.BlockSpec((GROUP_ROWS, LANES), lambda j, g, p=p: (g, p * npair + j))
    kv_spec, kv_shape = _state_spec(n_ctx_req, n_layers, e, (2, N_HEADS_A, t_ctx, HEAD_DIM),
                                    (2, hp, t_ctx, HEAD_DIM), lambda j: (0, j, 0, 0), n_ctx, GROUP_ROWS // t_ctx)
    in_specs = [
        blk(0), blk(1), blk(2),
        pl.BlockSpec((None, None, 2, hp, past, HEAD_DIM), lambda j, g: (jnp.maximum(g - n_ctx, 0), e, 0, j, 0, 0)),
        pl.BlockSpec((None, hp) + pair_bias.shape[2:], lambda j, g: (e, j, 0, 0, 0)),
    ]
    args = [z, z, z, cache_kv, pair_bias]
    aliases = {}
    if prev_kv is not None:
        in_specs.append(pl.BlockSpec(memory_space=pl.ANY))
        args.append(prev_kv)
        aliases = {len(args) - 1: 1}
    return pl.pallas_call(
        functools.partial(_attn_kernel, t_ctx=t_ctx, n_ctx=n_ctx),
        grid=(npair, n_grp),
        in_specs=in_specs,
        out_specs=[pl.BlockSpec((GROUP_ROWS, LANES), lambda j, g: (g, j)), kv_spec],
        out_shape=[jax.ShapeDtypeStruct((n_rows, D_A), BF16), kv_shape],
        input_output_aliases=aliases,
        compiler_params=_params(("arbitrary", "arbitrary"), 56),
        name="attention",
    )(*args)


def _rope_tables(t):
    half = HEAD_DIM // 2
    nf = half // 2
    inv = ROPE_BASE ** (-np.arange(nf, dtype=np.float32) / nf)
    pos = np.arange(t)
    ang_r = (pos // GRID_W).astype(np.float32)[:, None] * inv[None, :]
    ang_c = (pos % GRID_W).astype(np.float32)[:, None] * inv[None, :]
    ang_r, ang_c = jnp.asarray(ang_r), jnp.asarray(ang_c)
    cr, sr, cc, sc = jnp.cos(ang_r), jnp.sin(ang_r), jnp.cos(ang_c), jnp.sin(ang_c)
    cos = jnp.concatenate([cr, cr, cc, cc], axis=-1)
    sin = jnp.concatenate([-sr, sr, -sc, sc], axis=-1)
    return jnp.tile(cos, (1, HEADS_PER_TILE)), jnp.tile(sin, (1, HEADS_PER_TILE))


def _rope(x, cos, sin):
    nf = HEAD_DIM // 4
    lane = lax.broadcasted_iota(jnp.int32, x.shape, 1)
    partner = jnp.where(lane % (2 * nf) < nf, pltpu.roll(x, LANES - nf, axis=1), pltpu.roll(x, nf, axis=1))
    return x * cos + partner * sin


def _decay_matrix(dm_ref, lg, t):
    n_i = lax.broadcasted_iota(jnp.int32, (t, t), 0)
    m_i = lax.broadcasted_iota(jnp.int32, (t, t), 1)
    diff = (n_i - m_i).astype(F32)
    for hh in range(HEADS_PER_TILE):
        dm_ref[hh] = (jnp.where(diff >= 0, jnp.exp(lg[hh][0] * jnp.maximum(diff, 0.0)), 0.0)
                      + jnp.where(diff <= 0, jnp.exp(lg[hh][1] * jnp.maximum(-diff, 0.0)), 0.0))


def _retain(q_ref, k_ref, v_ref, g_ref, o_ref, dm_ref, lg, t, *, rope=None, s0_ref=None, st_ref=None):
    pos = lax.broadcasted_iota(jnp.int32, (t, 1), 0).astype(F32)
    for r in range(GROUP_ROWS // t):
        rs = slice(r * t, (r + 1) * t)
        q2 = q_ref[rs, :].astype(F32)
        k2 = k_ref[rs, :].astype(F32) * (HEAD_DIM ** -0.5)
        v2, g2 = v_ref[rs, :], g_ref[rs, :].astype(F32)
        if rope is not None:
            q2 = _rope(q2, rope[0][...], rope[1][...])
            k2 = _rope(k2, rope[0][...], rope[1][...])
        outs = []
        for hh in range(HEADS_PER_TILE):
            hs = slice(hh * HEAD_DIM, (hh + 1) * HEAD_DIM)
            lg_f, lg_b = lg[hh]
            q, k, vb = q2[:, hs], k2[:, hs], v2[:, hs]
            qb, kb = q.astype(BF16), k.astype(BF16)
            att = _dot_nt(qb, kb) * dm_ref[hh]
            o = _dot(att.astype(BF16), vb)
            if s0_ref is not None:
                qs = jnp.concatenate([q * jnp.exp(lg_f * (pos + 1.0)), q * jnp.exp(lg_b * (t - pos))], axis=-1)
                s0 = jnp.concatenate([s0_ref[0, hh], s0_ref[1, hh]], axis=0)
                o = o + _dot(qs.astype(BF16), s0.astype(BF16))
            if st_ref is not None:
                kf = k * jnp.exp(lg_f * (t - 1.0 - pos))
                kr = k * jnp.exp(lg_b * pos)
                st_ref[r, 0, 0, hh] = _dot_tn(kf.astype(BF16), vb)
                st_ref[r, 0, 1, hh] = _dot_tn(kr.astype(BF16), vb)
            on = o * lax.rsqrt(jnp.mean(o * o, axis=-1, keepdims=True) + EPS)
            outs.append(on * _silu(g2[:, hs]))
        o_ref[rs, :] = jnp.concatenate(outs, axis=-1).astype(o_ref.dtype)
    if st_ref is not None:
        _zero_other_layers(st_ref)


def _ret_kernel(dec_ref, q_ref, k_ref, v_ref, g_ref, cos_ref, sin_ref, s0_ref, *rest, t_ctx, t_lat, n_ctx):
    o_ref, st_ref, dmc_ref, dml_ref = rest[-4:]
    j, g = pl.program_id(0), pl.program_id(1)
    lg = [[-jnp.exp(jnp.full((1, 1), dec_ref[d, j * HEADS_PER_TILE + hh], F32)) for d in range(2)]
          for hh in range(HEADS_PER_TILE)]
    pl.when(g == 0)(functools.partial(_decay_matrix, dmc_ref, lg, t_ctx))
    pl.when(g == n_ctx)(functools.partial(_decay_matrix, dml_ref, lg, t_lat))
    is_ctx = g < n_ctx
    pl.when(is_ctx)(functools.partial(
        _retain, q_ref, k_ref, v_ref, g_ref, o_ref, dmc_ref, lg, t_ctx, st_ref=st_ref))
    pl.when(jnp.logical_not(is_ctx))(functools.partial(
        _retain, q_ref, k_ref, v_ref, g_ref, o_ref, dml_ref, lg, t_lat, rope=(cos_ref, sin_ref), s0_ref=s0_ref))


def _retention(z, ret_decay, rope_tabs, state_ret, e, n_layers, prev_st, n_ctx_req, t_ctx, t_lat, n_rows):
    npair = D_B // LANES
    hp = HEADS_PER_TILE
    sec0 = 3 * D_A // LANES
    n_ctx = n_ctx_req * t_ctx // GROUP_ROWS
    n_grp = n_rows // GROUP_ROWS
    assert t_lat == GROUP_ROWS
    blk = lambda p: pl.BlockSpec((GROUP_ROWS, LANES), lambda j, g, p=p: (g, sec0 + p * npair + j))
    st_spec, st_shape = _state_spec(n_ctx_req, n_layers, e, (2, N_HEADS_B, HEAD_DIM, HEAD_DIM),
                                    (2, hp, HEAD_DIM, HEAD_DIM), lambda j: (0, j, 0, 0), n_ctx,
                                    GROUP_ROWS // t_ctx)
    in_specs = [
        pl.BlockSpec(memory_space=pltpu.SMEM), blk(0), blk(1), blk(2), blk(3),
        pl.BlockSpec((t_lat, LANES), lambda j, g: (0, 0)),
        pl.BlockSpec((t_lat, LANES), lambda j, g: (0, 0)),
        pl.BlockSpec((None, None, 2, hp, HEAD_DIM, HEAD_DIM),
                     lambda j, g: (jnp.maximum(g - n_ctx, 0), e, 0, j, 0, 0)),
    ]
    args = [ret_decay, z, z, z, z, rope_tabs[0], rope_tabs[1], state_ret]
    aliases = {}
    if prev_st is not None:
        in_specs.append(pl.BlockSpec(memory_space=pl.ANY))
        args.append(prev_st)
        aliases = {len(args) - 1: 1}
    return pl.pallas_call(
        functools.partial(_ret_kernel, t_ctx=t_ctx, t_lat=t_lat, n_ctx=n_ctx),
        grid=(npair, n_grp),
        in_specs=in_specs,
        out_specs=[pl.BlockSpec((GROUP_ROWS, LANES), lambda j, g: (g, j)), st_spec],
        out_shape=[jax.ShapeDtypeStruct((n_rows, D_B), BF16), st_shape],
        input_output_aliases=aliases,
        scratch_shapes=[pltpu.VMEM((hp, t_ctx, t_ctx), F32), pltpu.VMEM((hp, t_lat, t_lat), F32)],
        compiler_params=_params(("arbitrary", "arbitrary"), 56),
        name="retention",
    )(*args)


def _seg_scan(x, reverse):
    t = x.shape[0]
    row = lax.broadcasted_iota(jnp.int32, (t, 1), 0) % HGRN_CHUNK
    sft = 1
    while sft < HGRN_CHUNK:
        if reverse:
            x = x + jnp.where(row < HGRN_CHUNK - sft, pltpu.roll(x, t - sft, axis=0), 0.0)
        else:
            x = x + jnp.where(row >= sft, pltpu.roll(x, sft, axis=0), 0.0)
        sft *= 2
    return x


def _hgrn_intra(q, k, v, c, reverse):
    cs = HGRN_CHUNK
    sub = 8
    row = lax.broadcasted_iota(jnp.int32, (cs, 1), 0)
    parts = [jnp.zeros((sub, v.shape[1]), F32) for _ in range(cs // sub)]
    for s in range(cs):
        blk_s = s // sub
        blks = range(0, blk_s + 1) if reverse else range(blk_s, cs // sub)
        c_s, k_s, v_s = c[s:s + 1], k[s:s + 1], v[s:s + 1]
        for bt in blks:
            rs = slice(bt * sub, (bt + 1) * sub)
            w = q[rs] * k_s * jnp.exp(jnp.minimum(c[rs] - c_s, 0.0))
            col = jnp.sum(w, axis=-1, keepdims=True)
            if bt == blk_s:
                keep = (row[rs] <= s) if reverse else (row[rs] >= s)
                col = jnp.where(keep, col, 0.0)
            parts[bt] = parts[bt] + col * v_s
    return jnp.concatenate(parts, axis=0)


def _hgrn_states(qe_ref, ke_ref, vb_ref, ee_ref, oi_ref, t, s0_ref=None, st_ref=None):
    sb = HGRN_SUPER
    n_sb = t // sb
    for r in range(GROUP_ROWS // t):
        for d in range(2):
            st = s0_ref[d].T if s0_ref is not None else None
            order = range(n_sb) if d == 0 else reversed(range(n_sb))
            for j in order:
                blk = r * n_sb + j
                rows = slice(blk * sb, (blk + 1) * sb)
                upd = _dot_tn(vb_ref[rows, :], ke_ref[d, rows, :])
                if st is None:
                    st = upd
                else:
                    oi_ref[rows, :] += _dot_nt(qe_ref[d, rows, :], st.astype(BF16))
                    st = st * ee_ref[d, blk][0:1] + upd
            if st_ref is not None:
                st_ref[r, 0, d] = st.T
    if st_ref is not None:
        _zero_other_layers(st_ref)


def _hgrn_kernel(q_ref, i_ref, g_ref, ff_ref, fb_ref, lb_ref, gn_ref, s0_ref, *rest, t_ctx, t_lat, n_ctx):
    o_ref, st_ref, qe_ref, ke_ref, vb_ref, ee_ref, oi_ref, sq_ref, sk_ref, sc_ref = rest[-10:]
    cs = HGRN_CHUNK
    sb = HGRN_SUPER
    row = lax.broadcasted_iota(jnp.int32, (sb, 1), 0)
    ti = lax.broadcasted_iota(jnp.int32, (sb, sb), 0)
    si = lax.broadcasted_iota(jnp.int32, (sb, sb), 1)
    term = jnp.where((ti >> 5) == (si >> 5), 0,
                     jnp.where((ti >> 6) == (si >> 6), 1, jnp.where((ti >> 7) == (si >> 7), 2, 3)))
    term_dir = (jnp.where(si <= ti, term, 4), jnp.where(si >= ti, term, 4))

    def block_step(blk, carry):
        rows = pl.ds(pl.multiple_of(blk * sb, sb), sb)
        qs = _silu(q_ref[rows, :].astype(F32)) * (DK_C ** -0.5)
        v = i_ref[rows, :]
        vb_ref[rows, :] = v
        per_dir = []
        c_min = None
        for d, f_ref in enumerate((ff_ref, fb_ref)):
            lb = lb_ref[d:d + 1, :]
            fr = f_ref[rows, :]
            e = jnp.exp(-jnp.abs(fr))
            r = 1.0 / (1.0 + e)
            sig_pos = jnp.where(fr >= 0, r, e * r)
            sig_neg = jnp.where(fr >= 0, e * r, r)
            f = lb + (1.0 - lb) * sig_pos
            k = (1.0 - lb) * sig_neg
            c = _seg_scan(jnp.log(jnp.maximum(f, F_MIN)), reverse=(d == 1))
            c3 = c.reshape(sb // cs, cs, DK_C)
            c_end = c3[:, cs - 1:cs, :] if d == 0 else c3[:, 0:1, :]
            tot = jnp.broadcast_to(c_end, c3.shape).reshape(sb, DK_C)
            per_dir.append((k, c, tot))
            m = jnp.min(c)
            c_min = m if c_min is None else jnp.minimum(c_min, m)

        safe = c_min >= -HGRN_SAFE_LOG
        att = None
        for d, (k, c, tot) in enumerate(per_dir):
            q_l = qs * jnp.exp(c)
            k_l = k * jnp.exp(tot - c)
            e_l = jnp.exp(tot)
            k_hat = jnp.where(safe, k * jnp.exp(-c), 0.0).astype(BF16)
            prods = [_dot_nt(q_l.astype(BF16), k_hat), _dot_nt(q_l.astype(BF16), k_l.astype(BF16))]
            size = cs
            while size < sb:
                prev_e = pltpu.roll(e_l, size, axis=0)
                next_e = pltpu.roll(e_l, sb - size, axis=0)
                second = (row % (2 * size)) >= size
                if d == 0:
                    q_l = q_l * jnp.where(second, prev_e, 1.0)
                    k_l = k_l * jnp.where(second, 1.0, next_e)
                else:
                    q_l = q_l * jnp.where(second, 1.0, next_e)
                    k_l = k_l * jnp.where(second, prev_e, 1.0)
                e_l = e_l * jnp.where(second, prev_e, next_e)
                size *= 2
                if size < sb:
                    prods.append(_dot_nt(q_l.astype(BF16), k_l.astype(BF16)))
            qe_ref[d, rows, :] = q_l.astype(BF16)
            ke_ref[d, rows, :] = k_l.astype(BF16)
            ee_ref[d, blk] = e_l[0:8]
            sel = jnp.zeros((sb, sb), F32)
            for i in reversed(range(len(prods))):
                sel = jnp.where(term_dir[d] == i, prods[i], sel)
            att = sel if att is None else att + sel
        oi_ref[rows, :] = _dot(att.astype(BF16), v)

        @pl.when(jnp.logical_not(safe))
        def _():
            sq_ref[...] = qs
            for d in range(2):
                sk_ref[d] = per_dir[d][0]
                sc_ref[d] = per_dir[d][1]

            def chunk_step(i, carry2):
                crow = pl.ds(pl.multiple_of(i * cs, cs), cs)
                orow = pl.ds(pl.multiple_of(blk * sb + i * cs, cs), cs)
                q, vv = sq_ref[crow, :], i_ref[orow, :].astype(F32)
                oi_ref[orow, :] += (_hgrn_intra(q, sk_ref[0, crow, :], vv, sc_ref[0, crow, :], reverse=False)
                                    + _hgrn_intra(q, sk_ref[1, crow, :], vv, sc_ref[1, crow, :], reverse=True))
                return carry2

            lax.fori_loop(0, sb // cs, chunk_step, 0)

        return carry

    lax.fori_loop(0, GROUP_ROWS // sb, block_step, 0)

    is_ctx = pl.program_id(1) < n_ctx
    pl.when(is_ctx)(functools.partial(
        _hgrn_states, qe_ref, ke_ref, vb_ref, ee_ref, oi_ref, t_ctx, st_ref=st_ref))
    pl.when(jnp.logical_not(is_ctx))(functools.partial(
        _hgrn_states, qe_ref, ke_ref, vb_ref, ee_ref, oi_ref, t_lat, s0_ref=s0_ref))

    o = oi_ref[...]
    on = o * lax.rsqrt(jnp.mean(o * o, axis=-1, keepdims=True) + EPS) * gn_ref[...]
    o_ref[...] = (on * _silu(g_ref[...].astype(F32))).astype(o_ref.dtype)


def _hgrn(z16, z32, lower, gnorm, state_hgrn, oi, n_layers, prev_st, n_ctx_req, t_ctx, t_lat, n_rows):
    nh = N_HEADS_C
    n_ctx = n_ctx_req * t_ctx // GROUP_ROWS
    n_grp = n_rows // GROUP_ROWS
    rows = GROUP_ROWS
    assert t_lat == GROUP_ROWS and t_ctx % HGRN_SUPER == 0
    blk = lambda p: pl.BlockSpec((rows, DK_C), lambda h, g, p=p: (g, p * nh + h))
    st_spec, st_shape = _state_spec(n_ctx_req, n_layers, oi, (2, nh, DK_C, DK_C),
                                    (2, None, DK_C, DK_C), lambda h: (0, h, 0, 0), n_ctx, GROUP_ROWS // t_ctx)
    in_specs = [blk(0), blk(1), blk(2), blk(0), blk(1),
                pl.BlockSpec((2, DK_C), lambda h, g: (0, h)),
                pl.BlockSpec((1, DK_C), lambda h, g: (0, 0)),
                pl.BlockSpec((None, None, 2, None, DK_C, DK_C),
                             lambda h, g: (jnp.maximum(g - n_ctx, 0), oi, 0, h, 0, 0))]
    args = [z16, z16, z16, z32, z32, lower, gnorm, state_hgrn]
    aliases = {}
    if prev_st is not None:
        in_specs.append(pl.BlockSpec(memory_space=pl.ANY))
        args.append(prev_st)
        aliases = {len(args) - 1: 1}
    return pl.pallas_call(
        functools.partial(_hgrn_kernel, t_ctx=t_ctx, t_lat=t_lat, n_ctx=n_ctx),
        grid=(nh, n_grp),
        in_specs=in_specs,
        out_specs=[pl.BlockSpec((rows, DK_C), lambda h, g: (g, h)), st_spec],
        out_shape=[jax.ShapeDtypeStruct((n_rows, D_MODEL), BF16), st_shape],
        input_output_aliases=aliases,
        scratch_shapes=[
            pltpu.VMEM((2, rows, DK_C), BF16),
            pltpu.VMEM((2, rows, DK_C), BF16),
            pltpu.VMEM((rows, DK_C), BF16),
            pltpu.VMEM((2, rows // HGRN_SUPER, 8, DK_C), F32),
            pltpu.VMEM((rows, DK_C), F32),
            pltpu.VMEM((HGRN_SUPER, DK_C), F32),
            pltpu.VMEM((2, HGRN_SUPER, DK_C), F32),
            pltpu.VMEM((2, HGRN_SUPER, DK_C), F32),
        ],
        compiler_params=_params(("arbitrary", "arbitrary"), 32),
        name="hgrn",
    )(*args)


def kernel(x_prompt, x_sample, cache_kv, state_ret, state_hgrn, c, c_ctx, w_mod, b_mod, norm_g,
           w_in_even, w_out_even, rpb, ret_decay, w_in_odd, w_out_odd, hgrn_lb, hgrn_gnorm,
           w_ffn_in, w_ffn_out):
    bp, tp, _ = x_prompt.shape
    bs, ts, _ = x_sample.shape
    np_rows, ns_rows = bp * tp, bs * ts
    n_rows = np_rows + ns_rows
    n_even, n_odd = w_in_even.shape[0], w_in_odd.shape[0]
    assert np_rows % GROUP_ROWS == 0 and ts == GROUP_ROWS
    x = jnp.concatenate([x_prompt.reshape(np_rows, D_MODEL), x_sample.reshape(ns_rows, D_MODEL)], axis=0)

    n_c = bs + 1
    pad = (-n_c) % 8
    cvec = jnp.concatenate([c, c_ctx[None], jnp.zeros((pad, D_MODEL), F32)], axis=0)
    mod_all = _modulation(cvec, w_mod, b_mod)
    grp = np.concatenate([np.full(np_rows // GROUP_ROWS, bs), np.arange(bs)]).astype(np.int32)
    mod_all = mod_all[:, grp].reshape(DEPTH, len(grp), 6, D_MODEL)
    mod_all = jnp.pad(mod_all, ((0, 0), (0, 0), (0, MOD_ROWS - 6), (0, 0)))

    p_lb = jax.nn.softmax(hgrn_lb.astype(F32), axis=0)
    lower = jnp.clip(jnp.cumsum(p_lb, axis=0) - p_lb[0], 0.0, 1.0)
    pair_bias = _na_bias_tables(rpb)
    rope_tabs = _rope_tables(ts)

    w_in_even_b, w_out_even_b = w_in_even.astype(BF16), w_out_even.astype(BF16)
    w_out_odd_b = w_out_odd.astype(BF16)
    w_in_odd16 = jnp.concatenate([w_in_odd[:, :, :D_C], w_in_odd[:, :, 3 * D_C:]], axis=-1).astype(BF16)
    w_in_odd32 = w_in_odd[:, :, D_C:3 * D_C].astype(BF16)
    ffn_w = [_ffn_weights(w_ffn_in[l], w_ffn_out[l]) for l in range(DEPTH)]

    kv_out = ret_out = hg_out = None
    y_split = None
    for l in range(DEPTH):
        mod = mod_all[l]
        g = norm_g[l].reshape(4, 1, D_MODEL)
        if l % 2 == 0:
            e = l // 2
            z = _in_proj(x, g[0], mod, w_in_even_b[e], BF16)
            oa, kv_out = _attention(z, cache_kv, pair_bias, e, n_even, kv_out, bp, tp, n_rows)
            ob, ret_out = _retention(z, ret_decay[e], rope_tabs, state_ret, e, n_even, ret_out, bp, tp, ts, n_rows)
            x = _out_proj([oa, ob], w_out_even_b[e], x, g[1], mod)
        else:
            oi = l // 2
            z16 = _in_proj(x, g[0], mod, w_in_odd16[oi], BF16)
            z32 = _in_proj(x, g[0], mod, w_in_odd32[oi], F32)
            gn = hgrn_gnorm[oi].reshape(1, DK_C)
            o, hg_out = _hgrn(z16, z32, lower[oi], gn, state_hgrn, oi, n_odd, hg_out, bp, tp, ts, n_rows)
            x = _out_proj([o], w_out_odd_b[oi], x, g[1], mod)
        if l == DEPTH - 1:
            y_split = _ffn(x, g[2], g[3], mod, *ffn_w[l], split=(x_prompt.shape, x_sample.shape))
        else:
            x = _ffn(x, g[2], g[3], mod, *ffn_w[l])

    return (y_split[0], y_split[1], kv_out, ret_out, hg_out)
```

```python
import functools

import numpy as np
import jax
import jax.numpy as jnp
from jax import lax
from jax.experimental import pallas as pl
from jax.experimental.pallas import tpu as pltpu

F32 = jnp.float32
BF16 = jnp.bfloat16

D_MODEL = 1024
DEPTH = 4
GRID_W = 64
HEAD_DIM = 64
N_HEADS_A = 8
N_HEADS_B = 8
D_A = N_HEADS_A * HEAD_DIM
D_B = N_HEADS_B * HEAD_DIM
WIN_R = 8
WIN_C = 16
N_HEADS_C = 8
DK_C = D_MODEL // N_HEADS_C
D_C = N_HEADS_C * DK_C
D_FF = ((8 * D_MODEL // 3 + 255) // 256) * 256
ATTN_SCALE = HEAD_DIM ** -0.5
HGRN_CHUNK = 32
HGRN_SUPER = 256
HGRN_SAFE_LOG = 75.0
ROPE_BASE = 10000.0
EPS = 1e-6
MASK_NEG = -1e30
F_MIN = 1e-30

GROUP_ROWS = 1024
MOD_ROWS = 8
LANES = 128
HEADS_PER_TILE = LANES // HEAD_DIM
PROJ_TN = 512
FFN_TF = 256
MOD_TN = 1536
MIB = 1024 * 1024

SH1, SC1, GT1, SH2, SC2, GT2 = range(6)

NA_QROWS = 4
NA_KROWS = 12
NA_KSTART = (0, 0, 4, 4)


def _params(sem, vmem_mib):
    return pltpu.CompilerParams(dimension_semantics=sem, vmem_limit_bytes=vmem_mib * MIB)


def _sigmoid(x):
    return 1.0 / (1.0 + jnp.exp(-x))


def _silu(x):
    return x * _sigmoid(x)


def _dot(a, b):
    return jnp.dot(a, b, preferred_element_type=F32)


def _dot_nt(a, b):
    return lax.dot_general(a, b, (((1,), (1,)), ((), ())), preferred_element_type=F32)


def _dot_tn(a, b):
    return lax.dot_general(a, b, (((0,), (0,)), ((), ())), preferred_element_type=F32)


def _mod_kernel(c_ref, w_ref, b_ref, o_ref):
    s = _silu(c_ref[...]).astype(BF16)
    o_ref[...] = _dot(s, w_ref[...].astype(BF16)) + b_ref[...]


def _modulation(cvec, w_mod, b_mod):
    rows = cvec.shape[0]
    n = w_mod.shape[-1]
    return pl.pallas_call(
        _mod_kernel,
        grid=(DEPTH, n // MOD_TN),
        in_specs=[
            pl.BlockSpec((rows, D_MODEL), lambda l, j: (0, 0)),
            pl.BlockSpec((None, D_MODEL, MOD_TN), lambda l, j: (l, 0, j)),
            pl.BlockSpec((None, 1, MOD_TN), lambda l, j: (l, 0, j)),
        ],
        out_specs=pl.BlockSpec((None, rows, MOD_TN), lambda l, j: (l, 0, j)),
        out_shape=jax.ShapeDtypeStruct((DEPTH, rows, n), F32),
        compiler_params=_params(("arbitrary", "arbitrary"), 32),
        name="modulation",
    )(cvec, w_mod, b_mod.reshape(DEPTH, 1, n))


def _norm_mod(x, g, mod, sh_row, sc_row):
    y = x * lax.rsqrt(jnp.mean(x * x, axis=-1, keepdims=True) + EPS) * g
    return y * (1.0 + mod[sc_row:sc_row + 1]) + mod[sh_row:sh_row + 1]


def _stream_specs(xs):
    tm = GROUP_ROWS
    if len(xs) == 1:
        return [pl.BlockSpec((tm, D_MODEL), lambda i: (i, 0))]
    n_ctx = xs[0].shape[0] // tm
    return [pl.BlockSpec((tm, D_MODEL), lambda i: (jnp.minimum(i, n_ctx - 1), 0)),
            pl.BlockSpec((tm, D_MODEL), lambda i: (jnp.maximum(i - n_ctx, 0), 0))]


def _stream_tile(x_refs, n_ctx):
    if len(x_refs) == 1:
        return x_refs[0][...]
    return jnp.where(pl.program_id(0) < n_ctx, x_refs[0][...], x_refs[1][...])


def _in_proj_kernel(*refs, n_x, n_ctx):
    x_refs, (g_ref, mod_ref), w_refs, o_ref = refs[:n_x], refs[n_x:n_x + 2], refs[n_x + 2:-1], refs[-1]
    h = _norm_mod(_stream_tile(x_refs, n_ctx), g_ref[...], mod_ref[...], SH1, SC1).astype(BF16)
    col0 = 0
    for w_ref in w_refs:
        for j in range(w_ref.shape[1] // PROJ_TN):
            cols = slice(j * PROJ_TN, (j + 1) * PROJ_TN)
            o_ref[:, col0 + j * PROJ_TN:col0 + (j + 1) * PROJ_TN] = _dot(h, w_ref[:, cols]).astype(o_ref.dtype)
        col0 += w_ref.shape[1]


def _resident(block_shape):
    return pl.BlockSpec(block_shape, lambda *_: (0,) * len(block_shape), pipeline_mode=pl.Buffered(1))


def _in_proj(xs, g, mod, ws, out_dtype):
    m = sum(x.shape[0] for x in xs)
    n = sum(w.shape[1] for w in ws)
    tm = GROUP_ROWS
    assert all(w.shape[1] % PROJ_TN == 0 for w in ws)
    return pl.pallas_call(
        functools.partial(_in_proj_kernel, n_x=len(xs), n_ctx=xs[0].shape[0] // tm),
        grid=(m // tm,),
        in_specs=_stream_specs(xs) + [
            pl.BlockSpec((1, D_MODEL), lambda i: (0, 0)),
            pl.BlockSpec((None, MOD_ROWS, D_MODEL), lambda i: (i, 0, 0)),
        ] + [_resident(w.shape) for w in ws],
        out_specs=pl.BlockSpec((tm, n), lambda i: (i, 0)),
        out_shape=jax.ShapeDtypeStruct((m, n), out_dtype),
        compiler_params=_params(("arbitrary",), 48),
        name="in_proj",
    )(*xs, g, mod, *ws)


def _out_proj_kernel(*refs, n_in, n_x, n_ctx):
    a_refs, w_refs = refs[:n_in], refs[n_in:2 * n_in]
    x_refs = refs[2 * n_in:2 * n_in + n_x]
    g_ref, mod_ref, o_ref = refs[2 * n_in + n_x:]
    y = _dot(a_refs[0][...], w_refs[0][...])
    for a_ref, w_ref in zip(a_refs[1:], w_refs[1:]):
        y = y + _dot(a_ref[...], w_ref[...])
    yn = y * lax.rsqrt(jnp.mean(y * y, axis=-1, keepdims=True) + EPS) * g_ref[...]
    o_ref[...] = _stream_tile(x_refs, n_ctx) + mod_ref[GT1:GT1 + 1, :] * yn


def _out_proj(acts, w, xs, g, mod):
    m = sum(x.shape[0] for x in xs)
    tm = GROUP_ROWS
    n_in = len(acts)
    ks = [a.shape[1] for a in acts]
    assert sum(ks) == w.shape[0] and len(set(ks)) == 1
    in_specs = [pl.BlockSpec((tm, k), lambda i: (i, 0)) for k in ks]
    in_specs += [pl.BlockSpec((ks[0], D_MODEL), lambda i, p=p: (p, 0)) for p in range(n_in)]
    in_specs += _stream_specs(xs) + [
        pl.BlockSpec((1, D_MODEL), lambda i: (0, 0)),
        pl.BlockSpec((None, MOD_ROWS, D_MODEL), lambda i: (i, 0, 0)),
    ]
    return pl.pallas_call(
        functools.partial(_out_proj_kernel, n_in=n_in, n_x=len(xs), n_ctx=xs[0].shape[0] // tm),
        grid=(m // tm,),
        in_specs=in_specs,
        out_specs=pl.BlockSpec((tm, D_MODEL), lambda i: (i, 0)),
        out_shape=jax.ShapeDtypeStruct((m, D_MODEL), F32),
        compiler_params=_params(("arbitrary",), 48),
        name="out_proj",
    )(*acts, *([w] * n_in), *xs, g, mod)


def _ffn_kernel(x_ref, g2_ref, g3_ref, mod_ref, wa_ref, wu_ref, wo_ref, *rest, n_ctx_groups):
    o_refs, (h_ref, acc_ref) = rest[:-2], rest[-2:]
    h_ref[...] = _norm_mod(x_ref[...], g2_ref[...], mod_ref[...], SH2, SC2).astype(BF16)
    acc_ref[...] = jnp.zeros_like(acc_ref)

    def hidden_step(f, carry):
        h = h_ref[...]
        a = _dot(h, wa_ref[f])
        u = _dot(h, wu_ref[f])
        acc_ref[...] += _dot((_silu(a) * u).astype(BF16), wo_ref[f])
        return carry

    lax.fori_loop(0, wa_ref.shape[0], hidden_step, 0)

    def finish(o_ref):
        y = acc_ref[...]
        yn = y * lax.rsqrt(jnp.mean(y * y, axis=-1, keepdims=True) + EPS) * g3_ref[...]
        o_ref[...] = (x_ref[...] + mod_ref[GT2:GT2 + 1, :] * yn).reshape(o_ref.shape)

    if len(o_refs) == 1:
        finish(o_refs[0])
    else:
        is_ctx = pl.program_id(0) < n_ctx_groups
        pl.when(is_ctx)(functools.partial(finish, o_refs[0]))
        pl.when(jnp.logical_not(is_ctx))(functools.partial(finish, o_refs[1]))


def _ffn_weights(w_in, w_out):
    nf = D_FF // FFN_TF
    w_in = w_in.reshape(D_MODEL, 2, nf, FFN_TF)
    wa = jnp.transpose(w_in[:, 0], (1, 0, 2)).astype(BF16)
    wu = jnp.transpose(w_in[:, 1], (1, 0, 2)).astype(BF16)
    return wa, wu, w_out.astype(BF16).reshape(nf, FFN_TF, D_MODEL)


def _ffn(x, g2, g3, mod, wa, wu, wo, split=None):
    m = x.shape[0]
    tm = GROUP_ROWS
    n_ctx = 0
    if split is None:
        out_specs = pl.BlockSpec((tm, D_MODEL), lambda i: (i, 0))
        out_shape = jax.ShapeDtypeStruct((m, D_MODEL), F32)
    else:
        (bp, tp, _), (bs, ts, _) = split
        n_ctx = bp * tp // tm
        assert ts == tm and (bp * tp) % tm == 0
        out_specs = [
            pl.BlockSpec((tm // tp, tp, D_MODEL), lambda i: (jnp.minimum(i, n_ctx - 1), 0, 0)),
            pl.BlockSpec((1, ts, D_MODEL), lambda i: (jnp.maximum(i - n_ctx, 0), 0, 0)),
        ]
        out_shape = [jax.ShapeDtypeStruct(s, F32) for s in split]
    return pl.pallas_call(
        functools.partial(_ffn_kernel, n_ctx_groups=n_ctx),
        grid=(m // tm,),
        in_specs=[
            pl.BlockSpec((tm, D_MODEL), lambda i: (i, 0)),
            pl.BlockSpec((1, D_MODEL), lambda i: (0, 0)),
            pl.BlockSpec((1, D_MODEL), lambda i: (0, 0)),
            pl.BlockSpec((None, MOD_ROWS, D_MODEL), lambda i: (i, 0, 0)),
            _resident(wa.shape), _resident(wu.shape), _resident(wo.shape),
        ],
        out_specs=out_specs,
        out_shape=out_shape,
        scratch_shapes=[pltpu.VMEM((tm, D_MODEL), BF16), pltpu.VMEM((tm, D_MODEL), F32)],
        compiler_params=_params(("arbitrary",), 56),
        name="ffn",
    )(x, g2, g3, mod, wa, wu, wo)


def _state_spec(n_req, n_layers, layer, tail, tail_block, tail_index, n_ctx, nb):
    n_own = n_layers if layer == 0 else 1
    first = 0 if layer == 0 else layer

    def index(j, g):
        return (jnp.minimum(g, n_ctx - 1), first) + tail_index(j)

    spec = pl.BlockSpec((nb, n_own) + tail_block, index)
    shape = jax.ShapeDtypeStruct((n_req, n_layers) + tail, F32)
    return spec, shape


def _zero_other_layers(st_ref):
    if st_ref.shape[1] > 1:
        st_ref[:, 1:] = jnp.zeros((st_ref.shape[0], st_ref.shape[1] - 1) + st_ref.shape[2:], st_ref.dtype)


NA_NRO = 2 * WIN_R - 1


def _na_bias_tables(rpb):
    qc = np.arange(GRID_W)[:, None]
    kc = np.arange(GRID_W)[None, :]
    win0 = np.clip(qc - WIN_C // 2, 0, GRID_W - WIN_C)
    col_valid = (kc >= win0) & (kc < win0 + WIN_C)
    col_off = np.clip(kc - qc, 1 - WIN_C, WIN_C - 1) + (WIN_C - 1)
    onehot = (col_off[None] == np.arange(2 * WIN_C - 1)[:, None, None]).astype(np.float32)
    tiles = jnp.einsum('ehrc,cqk->ehrqk', rpb.astype(F32), jnp.asarray(onehot), precision=lax.Precision.HIGHEST)
    tiles = jnp.where(col_valid[None, None, None], tiles, MASK_NEG)
    neg = jnp.full_like(tiles, MASK_NEG)
    nxt = jnp.concatenate([tiles[:, :, 1:], neg[:, :, :1]], axis=2)
    both = jnp.concatenate([tiles, nxt], axis=-1)
    first = jnp.concatenate([tiles, neg], axis=-1)
    second = jnp.concatenate([neg, tiles], axis=-1)
    none = jnp.concatenate([neg[:, :, :1], neg[:, :, :1]], axis=-1)
    return jnp.concatenate([both, first, second, none], axis=2)


def _na_group_bias(pair_ref, hh, g):
    rows = GROUP_ROWS // GRID_W
    bias = []
    for a in range(NA_QROWS):
        r = g * NA_QROWS + a
        row0 = min(max(r - WIN_R // 2, 0), rows - WIN_R)
        pieces = []
        for w in range(0, NA_KROWS, 2):
            kr = NA_KSTART[g] + w
            ro = kr - r + (WIN_R - 1)
            in0 = row0 <= kr < row0 + WIN_R
            in1 = row0 <= kr + 1 < row0 + WIN_R
            slot = ro if in0 and in1 else NA_NRO + ro if in0 else 2 * NA_NRO + ro + 1 if in1 else 3 * NA_NRO
            pieces.append(pair_ref[hh, slot])
        bias.append(jnp.concatenate(pieces, axis=1))
    return jnp.concatenate(bias, axis=0)


def _ctx_attend(q_ref, k_ref, v_ref, o_ref, kv_ref, t):
    for r in range(GROUP_ROWS // t):
        rs = slice(r * t, (r + 1) * t)
        q2, k2, v2 = q_ref[rs, :] * ATTN_SCALE, k_ref[rs, :], v_ref[rs, :]
        outs = []
        for hh in range(HEADS_PER_TILE):
            hs = slice(hh * HEAD_DIM, (hh + 1) * HEAD_DIM)
            q, k, v = q2[:, hs], k2[:, hs], v2[:, hs]
            kv_ref[r, 0, 0, hh] = k.astype(F32)
            kv_ref[r, 0, 1, hh] = v.astype(F32)
            s = _dot_nt(q, k)
            p = jnp.exp(s - jnp.max(s, axis=-1, keepdims=True))
            l = jnp.sum(p, axis=-1, keepdims=True)
            outs.append(_dot(p.astype(BF16), v) / l)
        o_ref[rs, :] = jnp.concatenate(outs, axis=-1).astype(o_ref.dtype)
    _zero_other_layers(kv_ref)


def _na_attend(q_ref, k_ref, v_ref, ckv_ref, pair_ref, o_ref):
    q2, k2, v2 = q_ref[...] * ATTN_SCALE, k_ref[...], v_ref[...]
    tq = NA_QROWS * GRID_W
    tk = NA_KROWS * GRID_W
    outs = []
    for hh in range(HEADS_PER_TILE):
        hs = slice(hh * HEAD_DIM, (hh + 1) * HEAD_DIM)
        q, k, v = q2[:, hs], k2[:, hs], v2[:, hs]
        kc = ckv_ref[0, hh].astype(BF16)
        vc = ckv_ref[1, hh].astype(BF16)
        rows = []
        for g in range(len(NA_KSTART)):
            qg = q[g * tq:(g + 1) * tq]
            k0 = NA_KSTART[g] * GRID_W
            s_loc = _dot_nt(qg, k[k0:k0 + tk]) + _na_group_bias(pair_ref, hh, g)
            s_ctx = _dot_nt(qg, kc)
            m = jnp.maximum(jnp.max(s_loc, axis=-1, keepdims=True), jnp.max(s_ctx, axis=-1, keepdims=True))
            p_loc = jnp.exp(s_loc - m)
            p_ctx = jnp.exp(s_ctx - m)
            l = jnp.sum(p_loc, axis=-1, keepdims=True) + jnp.sum(p_ctx, axis=-1, keepdims=True)
            o = _dot(p_loc.astype(BF16), v[k0:k0 + tk]) + _dot(p_ctx.astype(BF16), vc)
            rows.append(o / l)
        outs.append(jnp.concatenate(rows, axis=0))
    o_ref[...] = jnp.concatenate(outs, axis=-1).astype(o_ref.dtype)


def _attn_kernel(q_ref, k_ref, v_ref, ckv_ref, pair_ref, *rest, t_ctx, n_ctx):
    o_ref, kv_ref = rest[-2:]
    is_ctx = pl.program_id(1) < n_ctx
    pl.when(is_ctx)(functools.partial(_ctx_attend, q_ref, k_ref, v_ref, o_ref, kv_ref, t_ctx))
    pl.when(jnp.logical_not(is_ctx))(
        functools.partial(_na_attend, q_ref, k_ref, v_ref, ckv_ref, pair_ref, o_ref))


def _attention(z, cache_kv, pair_bias, e, n_layers, prev_kv, n_ctx_req, t_ctx, n_rows):
    npair = D_A // LANES
    hp = HEADS_PER_TILE
    n_ctx = n_ctx_req * t_ctx // GROUP_ROWS
    n_grp = n_rows // GROUP_ROWS
    past = cache_kv.shape[-2]
    blk = lambda p: pl.BlockSpec((GROUP_ROWS, LANES), lambda j, g, p=p: (g, p * npair + j))
    kv_spec, kv_shape = _state_spec(n_ctx_req, n_layers, e, (2, N_HEADS_A, t_ctx, HEAD_DIM),
                                    (2, hp, t_ctx, HEAD_DIM), lambda j: (0, j, 0, 0), n_ctx, GROUP_ROWS // t_ctx)
    in_specs = [
        blk(0), blk(1), blk(2),
        pl.BlockSpec((None, None, 2, hp, past, HEAD_DIM), lambda j, g: (jnp.maximum(g - n_ctx, 0), e, 0, j, 0, 0)),
        pl.BlockSpec((None, hp) + pair_bias.shape[2:], lambda j, g: (e, j, 0, 0, 0)),
    ]
    args = [z, z, z, cache_kv, pair_bias]
    aliases = {}
    if prev_kv is not None:
        in_specs.append(pl.BlockSpec(memory_space=pl.ANY))
        args.append(prev_kv)
        aliases = {len(args) - 1: 1}
    return pl.pallas_call(
        functools.partial(_attn_kernel, t_ctx=t_ctx, n_ctx=n_ctx),
        grid=(npair, n_grp),
        in_specs=in_specs,
        out_specs=[pl.BlockSpec((GROUP_ROWS, LANES), lambda j, g: (g, j)), kv_spec],
        out_shape=[jax.ShapeDtypeStruct((n_rows, D_A), BF16), kv_shape],
        input_output_aliases=aliases,
        compiler_params=_params(("arbitrary", "arbitrary"), 56),
        name="attention",
    )(*args)


def _rope_tables(t):
    half = HEAD_DIM // 2
    nf = half // 2
    inv = ROPE_BASE ** (-np.arange(nf, dtype=np.float32) / nf)
    pos = np.arange(t)
    ang_r = (pos // GRID_W).astype(np.float32)[:, None] * inv[None, :]
    ang_c = (pos % GRID_W).astype(np.float32)[:, None] * inv[None, :]
    ang_r, ang_c = jnp.asarray(ang_r), jnp.asarray(ang_c)
    cr, sr, cc, sc = jnp.cos(ang_r), jnp.sin(ang_r), jnp.cos(ang_c), jnp.sin(ang_c)
    cos = jnp.concatenate([cr, cr, cc, cc], axis=-1)
    sin = jnp.concatenate([-sr, sr, -sc, sc], axis=-1)
    return jnp.tile(cos, (1, HEADS_PER_TILE)), jnp.tile(sin, (1, HEADS_PER_TILE))


def _rope(x, cos, sin):
    nf = HEAD_DIM // 4
    lane = lax.broadcasted_iota(jnp.int32, x.shape, 1)
    partner = jnp.where(lane % (2 * nf) < nf, pltpu.roll(x, LANES - nf, axis=1), pltpu.roll(x, nf, axis=1))
    return x * cos + partner * sin


def _decay_matrix(dm_ref, lg, t):
    n_i = lax.broadcasted_iota(jnp.int32, (t, t), 0)
    m_i = lax.broadcasted_iota(jnp.int32, (t, t), 1)
    diff = (n_i - m_i).astype(F32)
    for hh in range(HEADS_PER_TILE):
        dm_ref[hh] = (jnp.where(diff >= 0, jnp.exp(lg[hh][0] * jnp.maximum(diff, 0.0)), 0.0)
                      + jnp.where(diff <= 0, jnp.exp(lg[hh][1] * jnp.maximum(-diff, 0.0)), 0.0))


def _retain(q_ref, k_ref, v_ref, g_ref, o_ref, dm_ref, lg, t, *, rope=None, s0_ref=None, st_ref=None):
    pos = lax.broadcasted_iota(jnp.int32, (t, 1), 0).astype(F32)
    for r in range(GROUP_ROWS // t):
        rs = slice(r * t, (r + 1) * t)
        q2 = q_ref[rs, :].astype(F32)
        k2 = k_ref[rs, :].astype(F32) * (HEAD_DIM ** -0.5)
        v2, g2 = v_ref[rs, :], g_ref[rs, :].astype(F32)
        if rope is not None:
            q2 = _rope(q2, rope[0][...], rope[1][...])
            k2 = _rope(k2, rope[0][...], rope[1][...])
        outs = []
        for hh in range(HEADS_PER_TILE):
            hs = slice(hh * HEAD_DIM, (hh + 1) * HEAD_DIM)
            lg_f, lg_b = lg[hh]
            q, k, vb = q2[:, hs], k2[:, hs], v2[:, hs]
            qb, kb = q.astype(BF16), k.astype(BF16)
            att = _dot_nt(qb, kb) * dm_ref[hh]
            o = _dot(att.astype(BF16), vb)
            if s0_ref is not None:
                qs = jnp.concatenate([q * jnp.exp(lg_f * (pos + 1.0)), q * jnp.exp(lg_b * (t - pos))], axis=-1)
                s0 = jnp.concatenate([s0_ref[0, hh], s0_ref[1, hh]], axis=0)
                o = o + _dot(qs.astype(BF16), s0.astype(BF16))
            if st_ref is not None:
                kf = k * jnp.exp(lg_f * (t - 1.0 - pos))
                kr = k * jnp.exp(lg_b * pos)
                st_ref[r, 0, 0, hh] = _dot_tn(kf.astype(BF16), vb)
                st_ref[r, 0, 1, hh] = _dot_tn(kr.astype(BF16), vb)
            on = o * lax.rsqrt(jnp.mean(o * o, axis=-1, keepdims=True) + EPS)
            outs.append(on * _silu(g2[:, hs]))
        o_ref[rs, :] = jnp.concatenate(outs, axis=-1).astype(o_ref.dtype)
    if st_ref is not None:
        _zero_other_layers(st_ref)


def _ret_kernel(dec_ref, q_ref, k_ref, v_ref, g_ref, cos_ref, sin_ref, s0_ref, *rest, t_ctx, t_lat, n_ctx):
    o_ref, st_ref, dmc_ref, dml_ref = rest[-4:]
    j, g = pl.program_id(0), pl.program_id(1)
    lg = [[-jnp.exp(jnp.full((1, 1), dec_ref[d, j * HEADS_PER_TILE + hh], F32)) for d in range(2)]
          for hh in range(HEADS_PER_TILE)]
    pl.when(g == 0)(functools.partial(_decay_matrix, dmc_ref, lg, t_ctx))
    pl.when(g == n_ctx)(functools.partial(_decay_matrix, dml_ref, lg, t_lat))
    is_ctx = g < n_ctx
    pl.when(is_ctx)(functools.partial(
        _retain, q_ref, k_ref, v_ref, g_ref, o_ref, dmc_ref, lg, t_ctx, st_ref=st_ref))
    pl.when(jnp.logical_not(is_ctx))(functools.partial(
        _retain, q_ref, k_ref, v_ref, g_ref, o_ref, dml_ref, lg, t_lat, rope=(cos_ref, sin_ref), s0_ref=s0_ref))


def _retention(z, ret_decay, rope_tabs, state_ret, e, n_layers, prev_st, n_ctx_req, t_ctx, t_lat, n_rows):
    npair = D_B // LANES
    hp = HEADS_PER_TILE
    sec0 = 3 * D_A // LANES
    n_ctx = n_ctx_req * t_ctx // GROUP_ROWS
    n_grp = n_rows // GROUP_ROWS
    assert t_lat == GROUP_ROWS
    blk = lambda p: pl.BlockSpec((GROUP_ROWS, LANES), lambda j, g, p=p: (g, sec0 + p * npair + j))
    st_spec, st_shape = _state_spec(n_ctx_req, n_layers, e, (2, N_HEADS_B, HEAD_DIM, HEAD_DIM),
                                    (2, hp, HEAD_DIM, HEAD_DIM), lambda j: (0, j, 0, 0), n_ctx,
                                    GROUP_ROWS // t_ctx)
    in_specs = [
        pl.BlockSpec(memory_space=pltpu.SMEM), blk(0), blk(1), blk(2), blk(3),
        pl.BlockSpec((t_lat, LANES), lambda j, g: (0, 0)),
        pl.BlockSpec((t_lat, LANES), lambda j, g: (0, 0)),
        pl.BlockSpec((None, None, 2, hp, HEAD_DIM, HEAD_DIM),
                     lambda j, g: (jnp.maximum(g - n_ctx, 0), e, 0, j, 0, 0)),
    ]
    args = [ret_decay, z, z, z, z, rope_tabs[0], rope_tabs[1], state_ret]
    aliases = {}
    if prev_st is not None:
        in_specs.append(pl.BlockSpec(memory_space=pl.ANY))
        args.append(prev_st)
        aliases = {len(args) - 1: 1}
    return pl.pallas_call(
        functools.partial(_ret_kernel, t_ctx=t_ctx, t_lat=t_lat, n_ctx=n_ctx),
        grid=(npair, n_grp),
        in_specs=in_specs,
        out_specs=[pl.BlockSpec((GROUP_ROWS, LANES), lambda j, g: (g, j)), st_spec],
        out_shape=[jax.ShapeDtypeStruct((n_rows, D_B), BF16), st_shape],
        input_output_aliases=aliases,
        scratch_shapes=[pltpu.VMEM((hp, t_ctx, t_ctx), F32), pltpu.VMEM((hp, t_lat, t_lat), F32)],
        compiler_params=_params(("arbitrary", "arbitrary"), 56),
        name="retention",
    )(*args)


def _seg_scan(x, reverse):
    t = x.shape[0]
    row = lax.broadcasted_iota(jnp.int32, (t, 1), 0) % HGRN_CHUNK
    sft = 1
    while sft < HGRN_CHUNK:
        if reverse:
            x = x + jnp.where(row < HGRN_CHUNK - sft, pltpu.roll(x, t - sft, axis=0), 0.0)
        else:
            x = x + jnp.where(row >= sft, pltpu.roll(x, sft, axis=0), 0.0)
        sft *= 2
    return x


def _hgrn_intra(q, k, v, c, reverse):
    cs = HGRN_CHUNK
    sub = 8
    row = lax.broadcasted_iota(jnp.int32, (cs, 1), 0)
    parts = [jnp.zeros((sub, v.shape[1]), F32) for _ in range(cs // sub)]
    for s in range(cs):
        blk_s = s // sub
        blks = range(0, blk_s + 1) if reverse else range(blk_s, cs // sub)
        c_s, k_s, v_s = c[s:s + 1], k[s:s + 1], v[s:s + 1]
        for bt in blks:
            rs = slice(bt * sub, (bt + 1) * sub)
            w = q[rs] * k_s * jnp.exp(jnp.minimum(c[rs] - c_s, 0.0))
            col = jnp.sum(w, axis=-1, keepdims=True)
            if bt == blk_s:
                keep = (row[rs] <= s) if reverse else (row[rs] >= s)
                col = jnp.where(keep, col, 0.0)
            parts[bt] = parts[bt] + col * v_s
    return jnp.concatenate(parts, axis=0)


def _hgrn_states(qe_ref, ke_ref, vb_ref, ee_ref, oi_ref, t, s0_ref=None, st_ref=None):
    sb = HGRN_SUPER
    n_sb = t // sb
    for r in range(GROUP_ROWS // t):
        incr = []
        for j in range(n_sb):
            rows = slice((r * n_sb + j) * sb, (r * n_sb + j + 1) * sb)
            keys = jnp.concatenate([ke_ref[0, rows, :], ke_ref[1, rows, :]], axis=1)
            incr.append(_dot_tn(keys, vb_ref[rows, :]))
        for d in range(2):
            st = s0_ref[d] if s0_ref is not None else None
            order = range(n_sb) if d == 0 else reversed(range(n_sb))
            for j in order:
                blk = r * n_sb + j
                rows = slice(blk * sb, (blk + 1) * sb)
                upd = incr[j][d * DK_C:(d + 1) * DK_C]
                if st is None:
                    st = upd
                else:
                    oi_ref[rows, :] += _dot(qe_ref[d, rows, :], st.astype(BF16))
                    decay = jnp.broadcast_to(ee_ref[d, blk][0:1], (DK_C, DK_C)).T
                    st = st * decay + upd
            if st_ref is not None:
                st_ref[r, 0, d] = st
    if st_ref is not None:
        _zero_other_layers(st_ref)


def _hgrn_kernel(q_ref, i_ref, g_ref, ff_ref, fb_ref, lb_ref, gn_ref, s0_ref, *rest, t_ctx, t_lat, n_ctx):
    o_ref, st_ref, qe_ref, ke_ref, vb_ref, ee_ref, oi_ref, sq_ref, sk_ref, sc_ref = rest[-10:]
    cs = HGRN_CHUNK
    sb = HGRN_SUPER
    row = lax.broadcasted_iota(jnp.int32, (sb, 1), 0)
    ti = lax.broadcasted_iota(jnp.int32, (sb, sb), 0)
    si = lax.broadcasted_iota(jnp.int32, (sb, sb), 1)
    term = jnp.where((ti >> 5) == (si >> 5), 0,
                     jnp.where((ti >> 6) == (si >> 6), 1, jnp.where((ti >> 7) == (si >> 7), 2, 3)))
    term_dir = (jnp.where(si <= ti, term, 4), jnp.where(si >= ti, term, 4))

    def block_step(blk, carry):
        rows = pl.ds(pl.multiple_of(blk * sb, sb), sb)
        qs = _silu(q_ref[rows, :].astype(F32)) * (DK_C ** -0.5)
        v = i_ref[rows, :]
        vb_ref[rows, :] = v
        per_dir = []
        c_min = None
        for d, f_ref in enumerate((ff_ref, fb_ref)):
            lb = lb_ref[d:d + 1, :]
            fr = f_ref[rows, :]
            e = jnp.exp(-jnp.abs(fr))
            r = 1.0 / (1.0 + e)
            sig_pos = jnp.where(fr >= 0, r, e * r)
            sig_neg = jnp.where(fr >= 0, e * r, r)
            f = lb + (1.0 - lb) * sig_pos
            k = (1.0 - lb) * sig_neg
            c = _seg_scan(jnp.log(jnp.maximum(f, F_MIN)), reverse=(d == 1))
            c3 = c.reshape(sb // cs, cs, DK_C)
            c_end = c3[:, cs - 1:cs, :] if d == 0 else c3[:, 0:1, :]
            tot = jnp.broadcast_to(c_end, c3.shape).reshape(sb, DK_C)
            per_dir.append((k, c, tot))
            m = jnp.min(c)
            c_min = m if c_min is None else jnp.minimum(c_min, m)

        safe = c_min >= -HGRN_SAFE_LOG
        att = None
        for d, (k, c, tot) in enumerate(per_dir):
            q_l = qs * jnp.exp(c)
            k_l = k * jnp.exp(tot - c)
            e_l = jnp.exp(tot)
            k_hat = jnp.where(safe, k * jnp.exp(-c), 0.0).astype(BF16)
            prods = [_dot_nt(q_l.astype(BF16), k_hat), _dot_nt(q_l.astype(BF16), k_l.astype(BF16))]
            size = cs
            while size < sb:
                prev_e = pltpu.roll(e_l, size, axis=0)
                next_e = pltpu.roll(e_l, sb - size, axis=0)
                second = (row % (2 * size)) >= size
                if d == 0:
                    q_l = q_l * jnp.where(second, prev_e, 1.0)
                    k_l = k_l * jnp.where(second, 1.0, next_e)
                else:
                    q_l = q_l * jnp.where(second, 1.0, next_e)
                    k_l = k_l * jnp.where(second, prev_e, 1.0)
                e_l = e_l * jnp.where(second, prev_e, next_e)
                size *= 2
                if size < sb:
                    prods.append(_dot_nt(q_l.astype(BF16), k_l.astype(BF16)))
            qe_ref[d, rows, :] = q_l.astype(BF16)
            ke_ref[d, rows, :] = k_l.astype(BF16)
            ee_ref[d, blk] = e_l[0:8]
            sel = jnp.zeros((sb, sb), F32)
            for i in reversed(range(len(prods))):
                sel = jnp.where(term_dir[d] == i, prods[i], sel)
            att = sel if att is None else att + sel
        oi_ref[rows, :] = _dot(att.astype(BF16), v)

        @pl.when(jnp.logical_not(safe))
        def _():
            sq_ref[...] = qs
            for d in range(2):
                sk_ref[d] = per_dir[d][0]
                sc_ref[d] = per_dir[d][1]

            def chunk_step(i, carry2):
                crow = pl.ds(pl.multiple_of(i * cs, cs), cs)
                orow = pl.ds(pl.multiple_of(blk * sb + i * cs, cs), cs)
                q, vv = sq_ref[crow, :], i_ref[orow, :].astype(F32)
                oi_ref[orow, :] += (_hgrn_intra(q, sk_ref[0, crow, :], vv, sc_ref[0, crow, :], reverse=False)
                                    + _hgrn_intra(q, sk_ref[1, crow, :], vv, sc_ref[1, crow, :], reverse=True))
                return carry2

            lax.fori_loop(0, sb // cs, chunk_step, 0)

        return carry

    lax.fori_loop(0, GROUP_ROWS // sb, block_step, 0)

    is_ctx = pl.program_id(1) < n_ctx
    pl.when(is_ctx)(functools.partial(
        _hgrn_states, qe_ref, ke_ref, vb_ref, ee_ref, oi_ref, t_ctx, st_ref=st_ref))
    pl.when(jnp.logical_not(is_ctx))(functools.partial(
        _hgrn_states, qe_ref, ke_ref, vb_ref, ee_ref, oi_ref, t_lat, s0_ref=s0_ref))

    o = oi_ref[...]
    on = o * lax.rsqrt(jnp.mean(o * o, axis=-1, keepdims=True) + EPS) * gn_ref[...]
    o_ref[...] = (on * _silu(g_ref[...].astype(F32))).astype(o_ref.dtype)


def _hgrn(z16, z32, lower, gnorm, state_hgrn, oi, n_layers, prev_st, n_ctx_req, t_ctx, t_lat, n_rows):
    nh = N_HEADS_C
    n_ctx = n_ctx_req * t_ctx // GROUP_ROWS
    n_grp = n_rows // GROUP_ROWS
    rows = GROUP_ROWS
    assert t_lat == GROUP_ROWS and t_ctx % HGRN_SUPER == 0
    blk = lambda p: pl.BlockSpec((rows, DK_C), lambda h, g, p=p: (g, p * nh + h))
    st_spec, st_shape = _state_spec(n_ctx_req, n_layers, oi, (2, nh, DK_C, DK_C),
                                    (2, None, DK_C, DK_C), lambda h: (0, h, 0, 0), n_ctx, GROUP_ROWS // t_ctx)
    in_specs = [blk(0), blk(1), blk(2), blk(0), blk(1),
                pl.BlockSpec((2, DK_C), lambda h, g: (0, h)),
                pl.BlockSpec((1, DK_C), lambda h, g: (0, 0)),
                pl.BlockSpec((None, None, 2, None, DK_C, DK_C),
                             lambda h, g: (jnp.maximum(g - n_ctx, 0), oi, 0, h, 0, 0))]
    args = [z16, z16, z16, z32, z32, lower, gnorm, state_hgrn]
    aliases = {}
    if prev_st is not None:
        in_specs.append(pl.BlockSpec(memory_space=pl.ANY))
        args.append(prev_st)
        aliases = {len(args) - 1: 1}
    return pl.pallas_call(
        functools.partial(_hgrn_kernel, t_ctx=t_ctx, t_lat=t_lat, n_ctx=n_ctx),
        grid=(nh, n_grp),
        in_specs=in_specs,
        out_specs=[pl.BlockSpec((rows, DK_C), lambda h, g: (g, h)), st_spec],
        out_shape=[jax.ShapeDtypeStruct((n_rows, D_MODEL), BF16), st_shape],
        input_output_aliases=aliases,
        scratch_shapes=[
            pltpu.VMEM((2, rows, DK_C), BF16),
            pltpu.VMEM((2, rows, DK_C), BF16),
            pltpu.VMEM((rows, DK_C), BF16),
            pltpu.VMEM((2, rows // HGRN_SUPER, 8, DK_C), F32),
            pltpu.VMEM((rows, DK_C), F32),
            pltpu.VMEM((HGRN_SUPER, DK_C), F32),
            pltpu.VMEM((2, HGRN_SUPER, DK_C), F32),
            pltpu.VMEM((2, HGRN_SUPER, DK_C), F32),
        ],
        compiler_params=_params(("arbitrary", "arbitrary"), 32),
        name="hgrn",
    )(*args)


def kernel(x_prompt, x_sample, cache_kv, state_ret, state_hgrn, c, c_ctx, w_mod, b_mod, norm_g,
           w_in_even, w_out_even, rpb, ret_decay, w_in_odd, w_out_odd, hgrn_lb, hgrn_gnorm,
           w_ffn_in, w_ffn_out):
    bp, tp, _ = x_prompt.shape
    bs, ts, _ = x_sample.shape
    np_rows, ns_rows = bp * tp, bs * ts
    n_rows = np_rows + ns_rows
    n_even, n_odd = w_in_even.shape[0], w_in_odd.shape[0]
    assert np_rows % GROUP_ROWS == 0 and ts == GROUP_ROWS
    xs = [x_prompt.reshape(np_rows, D_MODEL), x_sample.reshape(ns_rows, D_MODEL)]

    n_c = bs + 1
    pad = (-n_c) % 8
    cvec = jnp.concatenate([c, c_ctx[None], jnp.zeros((pad, D_MODEL), F32)], axis=0)
    mod_all = _modulation(cvec, w_mod, b_mod)
    grp = np.concatenate([np.full(np_rows // GROUP_ROWS, bs), np.arange(bs)]).astype(np.int32)
    mod_all = mod_all[:, grp].reshape(DEPTH, len(grp), 6, D_MODEL)
    mod_all = jnp.pad(mod_all, ((0, 0), (0, 0), (0, MOD_ROWS - 6), (0, 0)))

    p_lb = jax.nn.softmax(hgrn_lb.astype(F32), axis=0)
    lower = jnp.clip(jnp.cumsum(p_lb, axis=0) - p_lb[0], 0.0, 1.0)
    pair_bias = _na_bias_tables(rpb)
    rope_tabs = _rope_tables(ts)

    w_in_even_b, w_out_even_b = w_in_even.astype(BF16), w_out_even.astype(BF16)
    w_out_odd_b = w_out_odd.astype(BF16)
    w_in_odd_q = w_in_odd[:, :, :D_C].astype(BF16)
    w_in_odd_f = w_in_odd[:, :, D_C:3 * D_C].astype(BF16)
    w_in_odd_ig = w_in_odd[:, :, 3 * D_C:].astype(BF16)
    ffn_w = [_ffn_weights(w_ffn_in[l], w_ffn_out[l]) for l in range(DEPTH)]

    kv_out = ret_out = hg_out = None
    y_split = None
    for l in range(DEPTH):
        mod = mod_all[l]
        g = norm_g[l].reshape(4, 1, D_MODEL)
        if l % 2 == 0:
            e = l // 2
            z = _in_proj(xs, g[0], mod, [w_in_even_b[e]], BF16)
            oa, kv_out = _attention(z, cache_kv, pair_bias, e, n_even, kv_out, bp, tp, n_rows)
            ob, ret_out = _retention(z, ret_decay[e], rope_tabs, state_ret, e, n_even, ret_out, bp, tp, ts, n_rows)
            x = _out_proj([oa, ob], w_out_even_b[e], xs, g[1], mod)
        else:
            oi = l // 2
            z16 = _in_proj(xs, g[0], mod, [w_in_odd_q[oi], w_in_odd_ig[oi]], BF16)
            z32 = _in_proj(xs, g[0], mod, [w_in_odd_f[oi]], F32)
            gn = hgrn_gnorm[oi].reshape(1, DK_C)
            o, hg_out = _hgrn(z16, z32, lower[oi], gn, state_hgrn, oi, n_odd, hg_out, bp, tp, ts, n_rows)
            x = _out_proj([o], w_out_odd_b[oi], xs, g[1], mod)
        if l == DEPTH - 1:
            y_split = _ffn(x, g[2], g[3], mod, *ffn_w[l], split=(x_prompt.shape, x_sample.shape))
        else:
            xs = [_ffn(x, g[2], g[3], mod, *ffn_w[l])]

    return (y_split[0], y_split[1], kv_out, ret_out, hg_out)
```

```python
import functools

import numpy as np
import jax
import jax.numpy as jnp
from jax import lax
from jax.experimental import pallas as pl
from jax.experimental.pallas import tpu as pltpu

F32 = jnp.float32
BF16 = jnp.bfloat16

D_MODEL = 1024
DEPTH = 4
GRID_W = 64
HEAD_DIM = 64
N_HEADS_A = 8
N_HEADS_B = 8
D_A = N_HEADS_A * HEAD_DIM
D_B = N_HEADS_B * HEAD_DIM
WIN_R = 8
WIN_C = 16
N_HEADS_C = 8
DK_C = D_MODEL // N_HEADS_C
D_C = N_HEADS_C * DK_C
D_FF = ((8 * D_MODEL // 3 + 255) // 256) * 256
ATTN_SCALE = HEAD_DIM ** -0.5
HGRN_CHUNK = 32
HGRN_SUPER = 256
HGRN_SAFE_LOG = 75.0
ROPE_BASE = 10000.0
EPS = 1e-6
MASK_NEG = -1e30
F_MIN = 1e-30

GROUP_ROWS = 1024
MOD_ROWS = 8
LANES = 128
HEADS_PER_TILE = LANES // HEAD_DIM
PROJ_TN = 512
FFN_TF = 256
MOD_TN = 1536
MIB = 1024 * 1024

SH1, SC1, GT1, SH2, SC2, GT2 = range(6)

NA_QROWS = 4
NA_KROWS = 12
NA_KSTART = (0, 0, 4, 4)


def _params(sem, vmem_mib):
    return pltpu.CompilerParams(dimension_semantics=sem, vmem_limit_bytes=vmem_mib * MIB)


def _sigmoid(x):
    return 1.0 / (1.0 + jnp.exp(-x))


def _silu(x):
    return x * _sigmoid(x)


def _dot(a, b):
    return jnp.dot(a, b, preferred_element_type=F32)


def _dot_nt(a, b):
    return lax.dot_general(a, b, (((1,), (1,)), ((), ())), preferred_element_type=F32)


def _dot_tn(a, b):
    return lax.dot_general(a, b, (((0,), (0,)), ((), ())), preferred_element_type=F32)


def _mod_kernel(c_ref, w_ref, b_ref, o_ref):
    s = _silu(c_ref[...]).astype(BF16)
    o_ref[...] = _dot(s, w_ref[...].astype(BF16)) + b_ref[...]


def _modulation(cvec, w_mod, b_mod):
    rows = cvec.shape[0]
    n = w_mod.shape[-1]
    return pl.pallas_call(
        _mod_kernel,
        grid=(DEPTH, n // MOD_TN),
        in_specs=[
            pl.BlockSpec((rows, D_MODEL), lambda l, j: (0, 0)),
            pl.BlockSpec((None, D_MODEL, MOD_TN), lambda l, j: (l, 0, j)),
            pl.BlockSpec((None, 1, MOD_TN), lambda l, j: (l, 0, j)),
        ],
        out_specs=pl.BlockSpec((None, rows, MOD_TN), lambda l, j: (l, 0, j)),
        out_shape=jax.ShapeDtypeStruct((DEPTH, rows, n), F32),
        compiler_params=_params(("arbitrary", "arbitrary"), 32),
        name="modulation",
    )(cvec, w_mod, b_mod.reshape(DEPTH, 1, n))


def _norm_mod(x, g, mod, sh_row, sc_row):
    y = x * lax.rsqrt(jnp.mean(x * x, axis=-1, keepdims=True) + EPS) * g
    return y * (1.0 + mod[sc_row:sc_row + 1]) + mod[sh_row:sh_row + 1]


def _stream_specs(xs):
    tm = GROUP_ROWS
    if len(xs) == 1:
        return [pl.BlockSpec((tm, D_MODEL), lambda i: (i, 0))]
    n_ctx = xs[0].shape[0] // tm
    return [pl.BlockSpec((tm, D_MODEL), lambda i: (jnp.minimum(i, n_ctx - 1), 0)),
            pl.BlockSpec((tm, D_MODEL), lambda i: (jnp.maximum(i - n_ctx, 0), 0))]


def _stream_tile(x_refs, n_ctx):
    if len(x_refs) == 1:
        return x_refs[0][...]
    return jnp.where(pl.program_id(0) < n_ctx, x_refs[0][...], x_refs[1][...])


def _in_proj_kernel(*refs, n_x, n_ctx):
    x_refs, (g_ref, mod_ref), w_refs, o_ref = refs[:n_x], refs[n_x:n_x + 2], refs[n_x + 2:-1], refs[-1]
    h = _norm_mod(_stream_tile(x_refs, n_ctx), g_ref[...], mod_ref[...], SH1, SC1).astype(BF16)
    col0 = 0
    for w_ref in w_refs:
        for j in range(w_ref.shape[1] // PROJ_TN):
            cols = slice(j * PROJ_TN, (j + 1) * PROJ_TN)
            o_ref[:, col0 + j * PROJ_TN:col0 + (j + 1) * PROJ_TN] = _dot(h, w_ref[:, cols]).astype(o_ref.dtype)
        col0 += w_ref.shape[1]


def _resident(block_shape):
    return pl.BlockSpec(block_shape, lambda *_: (0,) * len(block_shape), pipeline_mode=pl.Buffered(1))


def _in_proj(xs, g, mod, ws, out_dtype):
    m = sum(x.shape[0] for x in xs)
    n = sum(w.shape[1] for w in ws)
    tm = GROUP_ROWS
    assert all(w.shape[1] % PROJ_TN == 0 for w in ws)
    return pl.pallas_call(
        functools.partial(_in_proj_kernel, n_x=len(xs), n_ctx=xs[0].shape[0] // tm),
        grid=(m // tm,),
        in_specs=_stream_specs(xs) + [
            pl.BlockSpec((1, D_MODEL), lambda i: (0, 0)),
            pl.BlockSpec((None, MOD_ROWS, D_MODEL), lambda i: (i, 0, 0)),
        ] + [_resident(w.shape) for w in ws],
        out_specs=pl.BlockSpec((tm, n), lambda i: (i, 0)),
        out_shape=jax.ShapeDtypeStruct((m, n), out_dtype),
        compiler_params=_params(("arbitrary",), 48),
        name="in_proj",
    )(*xs, g, mod, *ws)


def _out_proj_kernel(*refs, n_in, n_x, n_ctx):
    a_refs, w_refs = refs[:n_in], refs[n_in:2 * n_in]
    x_refs = refs[2 * n_in:2 * n_in + n_x]
    g_ref, mod_ref, o_ref = refs[2 * n_in + n_x:]
    y = _dot(a_refs[0][...], w_refs[0][...])
    for a_ref, w_ref in zip(a_refs[1:], w_refs[1:]):
        y = y + _dot(a_ref[...], w_ref[...])
    yn = y * lax.rsqrt(jnp.mean(y * y, axis=-1, keepdims=True) + EPS) * g_ref[...]
    o_ref[...] = _stream_tile(x_refs, n_ctx) + mod_ref[GT1:GT1 + 1, :] * yn


def _out_proj(acts, w, xs, g, mod):
    m = sum(x.shape[0] for x in xs)
    tm = GROUP_ROWS
    n_in = len(acts)
    ks = [a.shape[1] for a in acts]
    assert sum(ks) == w.shape[0] and len(set(ks)) == 1
    in_specs = [pl.BlockSpec((tm, k), lambda i: (i, 0)) for k in ks]
    in_specs += [pl.BlockSpec((ks[0], D_MODEL), lambda i, p=p: (p, 0)) for p in range(n_in)]
    in_specs += _stream_specs(xs) + [
        pl.BlockSpec((1, D_MODEL), lambda i: (0, 0)),
        pl.BlockSpec((None, MOD_ROWS, D_MODEL), lambda i: (i, 0, 0)),
    ]
    return pl.pallas_call(
        functools.partial(_out_proj_kernel, n_in=n_in, n_x=len(xs), n_ctx=xs[0].shape[0] // tm),
        grid=(m // tm,),
        in_specs=in_specs,
        out_specs=pl.BlockSpec((tm, D_MODEL), lambda i: (i, 0)),
        out_shape=jax.ShapeDtypeStruct((m, D_MODEL), F32),
        compiler_params=_params(("arbitrary",), 48),
        name="out_proj",
    )(*acts, *([w] * n_in), *xs, g, mod)


def _ffn_kernel(x_ref, xn_ref, g2_ref, g3_ref, mod_ref, modn_ref, wa_ref, wu_ref, wo_ref, *rest, n_ctx_groups):
    o_refs, (h_ref, acc_ref) = rest[:-2], rest[-2:]
    tm = x_ref.shape[0]
    nf = wa_ref.shape[0]
    ahead = -(-tm // (nf * 16)) * 16
    i = pl.program_id(0)
    cur, nxt = i % 2, (i + 1) % 2

    @pl.when(i == 0)
    def _():
        h_ref[0] = _norm_mod(x_ref[...], g2_ref[...], mod_ref[...], SH2, SC2).astype(BF16)

    acc_ref[...] = jnp.zeros_like(acc_ref)

    def hidden_step(f, carry):
        h = h_ref[cur]
        a = _dot(h, wa_ref[f])
        u = _dot(h, wu_ref[f])
        acc_ref[...] += _dot((_silu(a) * u).astype(BF16), wo_ref[f])
        r0 = pl.multiple_of(jnp.minimum(f * ahead, tm - ahead), 16)
        rows = pl.ds(r0, ahead)
        h_ref[nxt, rows, :] = _norm_mod(xn_ref[rows, :], g2_ref[...], modn_ref[...], SH2, SC2).astype(BF16)
        return carry

    lax.fori_loop(0, nf, hidden_step, 0)

    def finish(o_ref):
        y = acc_ref[...]
        yn = y * lax.rsqrt(jnp.mean(y * y, axis=-1, keepdims=True) + EPS) * g3_ref[...]
        o_ref[...] = (x_ref[...] + mod_ref[GT2:GT2 + 1, :] * yn).reshape(o_ref.shape)

    if len(o_refs) == 1:
        finish(o_refs[0])
    else:
        is_ctx = pl.program_id(0) < n_ctx_groups
        pl.when(is_ctx)(functools.partial(finish, o_refs[0]))
        pl.when(jnp.logical_not(is_ctx))(functools.partial(finish, o_refs[1]))


def _ffn_weights(w_in, w_out):
    nf = D_FF // FFN_TF
    w_in = w_in.reshape(D_MODEL, 2, nf, FFN_TF)
    wa = jnp.transpose(w_in[:, 0], (1, 0, 2)).astype(BF16)
    wu = jnp.transpose(w_in[:, 1], (1, 0, 2)).astype(BF16)
    return wa, wu, w_out.astype(BF16).reshape(nf, FFN_TF, D_MODEL)


def _ffn(x, g2, g3, mod, wa, wu, wo, split=None):
    m = x.shape[0]
    tm = GROUP_ROWS
    n_ctx = 0
    if split is None:
        out_specs = pl.BlockSpec((tm, D_MODEL), lambda i: (i, 0))
        out_shape = jax.ShapeDtypeStruct((m, D_MODEL), F32)
    else:
        (bp, tp, _), (bs, ts, _) = split
        n_ctx = bp * tp // tm
        assert ts == tm and (bp * tp) % tm == 0
        out_specs = [
            pl.BlockSpec((tm // tp, tp, D_MODEL), lambda i: (jnp.minimum(i, n_ctx - 1), 0, 0)),
            pl.BlockSpec((1, ts, D_MODEL), lambda i: (jnp.maximum(i - n_ctx, 0), 0, 0)),
        ]
        out_shape = [jax.ShapeDtypeStruct(s, F32) for s in split]
    last = m // tm - 1
    return pl.pallas_call(
        functools.partial(_ffn_kernel, n_ctx_groups=n_ctx),
        grid=(m // tm,),
        in_specs=[
            pl.BlockSpec((tm, D_MODEL), lambda i: (i, 0)),
            pl.BlockSpec((tm, D_MODEL), lambda i: (jnp.minimum(i + 1, last), 0)),
            pl.BlockSpec((1, D_MODEL), lambda i: (0, 0)),
            pl.BlockSpec((1, D_MODEL), lambda i: (0, 0)),
            pl.BlockSpec((None, MOD_ROWS, D_MODEL), lambda i: (i, 0, 0)),
            pl.BlockSpec((None, MOD_ROWS, D_MODEL), lambda i: (jnp.minimum(i + 1, last), 0, 0)),
            _resident(wa.shape), _resident(wu.shape), _resident(wo.shape),
        ],
        out_specs=out_specs,
        out_shape=out_shape,
        scratch_shapes=[pltpu.VMEM((2, tm, D_MODEL), BF16), pltpu.VMEM((tm, D_MODEL), F32)],
        compiler_params=_params(("arbitrary",), 60),
        name="ffn",
    )(x, x, g2, g3, mod, mod, wa, wu, wo)


def _state_spec(n_req, n_layers, layer, tail, tail_block, tail_index, n_ctx, nb):
    n_own = n_layers if layer == 0 else 1
    first = 0 if layer == 0 else layer

    def index(j, g):
        return (jnp.minimum(g, n_ctx - 1), first) + tail_index(j)

    spec = pl.BlockSpec((nb, n_own) + tail_block, index)
    shape = jax.ShapeDtypeStruct((n_req, n_layers) + tail, F32)
    return spec, shape


def _zero_other_layers(st_ref):
    if st_ref.shape[1] > 1:
        st_ref[:, 1:] = jnp.zeros((st_ref.shape[0], st_ref.shape[1] - 1) + st_ref.shape[2:], st_ref.dtype)


NA_NRO = 2 * WIN_R - 1


def _na_bias_tables(rpb):
    qc = np.arange(GRID_W)[:, None]
    kc = np.arange(GRID_W)[None, :]
    win0 = np.clip(qc - WIN_C // 2, 0, GRID_W - WIN_C)
    col_valid = (kc >= win0) & (kc < win0 + WIN_C)
    col_off = np.clip(kc - qc, 1 - WIN_C, WIN_C - 1) + (WIN_C - 1)
    onehot = (col_off[None] == np.arange(2 * WIN_C - 1)[:, None, None]).astype(np.float32)
    tiles = jnp.einsum('ehrc,cqk->ehrqk', rpb.astype(F32), jnp.asarray(onehot), precision=lax.Precision.HIGHEST)
    tiles = jnp.where(col_valid[None, None, None], tiles, MASK_NEG)
    neg = jnp.full_like(tiles, MASK_NEG)
    nxt = jnp.concatenate([tiles[:, :, 1:], neg[:, :, :1]], axis=2)
    both = jnp.concatenate([tiles, nxt], axis=-1)
    first = jnp.concatenate([tiles, neg], axis=-1)
    second = jnp.concatenate([neg, tiles], axis=-1)
    none = jnp.concatenate([neg[:, :, :1], neg[:, :, :1]], axis=-1)
    return jnp.concatenate([both, first, second, none], axis=2)


def _na_group_bias(pair_ref, hh, g):
    rows = GROUP_ROWS // GRID_W
    bias = []
    for a in range(NA_QROWS):
        r = g * NA_QROWS + a
        row0 = min(max(r - WIN_R // 2, 0), rows - WIN_R)
        pieces = []
        for w in range(0, NA_KROWS, 2):
            kr = NA_KSTART[g] + w
            ro = kr - r + (WIN_R - 1)
            in0 = row0 <= kr < row0 + WIN_R
            in1 = row0 <= kr + 1 < row0 + WIN_R
            slot = ro if in0 and in1 else NA_NRO + ro if in0 else 2 * NA_NRO + ro + 1 if in1 else 3 * NA_NRO
            pieces.append(pair_ref[hh, slot])
        bias.append(jnp.concatenate(pieces, axis=1))
    return jnp.concatenate(bias, axis=0)


def _ctx_attend(q_ref, k_ref, v_ref, o_ref, kv_ref, t):
    for r in range(GROUP_ROWS // t):
        rs = slice(r * t, (r + 1) * t)
        q2, k2, v2 = q_ref[rs, :] * ATTN_SCALE, k_ref[rs, :], v_ref[rs, :]
        outs = []
        for hh in range(HEADS_PER_TILE):
            hs = slice(hh * HEAD_DIM, (hh + 1) * HEAD_DIM)
            q, k, v = q2[:, hs], k2[:, hs], v2[:, hs]
            kv_ref[r, 0, 0, hh] = k.astype(F32)
            kv_ref[r, 0, 1, hh] = v.astype(F32)
            s = _dot_nt(q, k)
            p = jnp.exp(s - jnp.max(s, axis=-1, keepdims=True))
            l = jnp.sum(p, axis=-1, keepdims=True)
            outs.append(_dot(p.astype(BF16), v) / l)
        o_ref[rs, :] = jnp.concatenate(outs, axis=-1).astype(o_ref.dtype)
    _zero_other_layers(kv_ref)


def _na_attend(q_ref, k_ref, v_ref, ckv_ref, pair_ref, o_ref):
    q2, k2, v2 = q_ref[...] * ATTN_SCALE, k_ref[...], v_ref[...]
    tq = NA_QROWS * GRID_W
    tk = NA_KROWS * GRID_W
    outs = []
    for hh in range(HEADS_PER_TILE):
        hs = slice(hh * HEAD_DIM, (hh + 1) * HEAD_DIM)
        q, k, v = q2[:, hs], k2[:, hs], v2[:, hs]
        kc = ckv_ref[0, hh].astype(BF16)
        vc = ckv_ref[1, hh].astype(BF16)
        rows = []
        for g in range(len(NA_KSTART)):
            qg = q[g * tq:(g + 1) * tq]
            k0 = NA_KSTART[g] * GRID_W
            s_loc = _dot_nt(qg, k[k0:k0 + tk]) + _na_group_bias(pair_ref, hh, g)
            s_ctx = _dot_nt(qg, kc)
            m = jnp.maximum(jnp.max(s_loc, axis=-1, keepdims=True), jnp.max(s_ctx, axis=-1, keepdims=True))
            p_loc = jnp.exp(s_loc - m)
            p_ctx = jnp.exp(s_ctx - m)
            l = jnp.sum(p_loc, axis=-1, keepdims=True) + jnp.sum(p_ctx, axis=-1, keepdims=True)
            o = _dot(p_loc.astype(BF16), v[k0:k0 + tk]) + _dot(p_ctx.astype(BF16), vc)
            rows.append(o / l)
        outs.append(jnp.concatenate(rows, axis=0))
    o_ref[...] = jnp.concatenate(outs, axis=-1).astype(o_ref.dtype)


def _attn_kernel(q_ref, k_ref, v_ref, ckv_ref, pair_ref, *rest, t_ctx, n_ctx):
    o_ref, kv_ref = rest[-2:]
    is_ctx = pl.program_id(1) < n_ctx
    pl.when(is_ctx)(functools.partial(_ctx_attend, q_ref, k_ref, v_ref, o_ref, kv_ref, t_ctx))
    pl.when(jnp.logical_not(is_ctx))(
        functools.partial(_na_attend, q_ref, k_ref, v_ref, ckv_ref, pair_ref, o_ref))


def _attention(z, cache_kv, pair_bias, e, n_layers, prev_kv, n_ctx_req, t_ctx, n_rows):
    npair = D_A // LANES
    hp = HEADS_PER_TILE
    n_ctx = n_ctx_req * t_ctx // GROUP_ROWS
    n_grp = n_rows // GROUP_ROWS
    past = cache_kv.shape[-2]
    blk = lambda p: pl.BlockSpec((GROUP_ROWS, LANES), lambda j, g, p=p: (g, p * npair + j))
    kv_spec, kv_shape = _state_spec(n_ctx_req, n_layers, e, (2, N_HEADS_A, t_ctx, HEAD_DIM),
                                    (2, hp, t_ctx, HEAD_DIM), lambda j: (0, j, 0, 0), n_ctx, GROUP_ROWS // t_ctx)
    in_specs = [
        blk(0), blk(1), blk(2),
        pl.BlockSpec((None, None, 2, hp, past, HEAD_DIM), lambda j, g: (jnp.maximum(g - n_ctx, 0), e, 0, j, 0, 0)),
        pl.BlockSpec((None, hp) + pair_bias.shape[2:], lambda j, g: (e, j, 0, 0, 0)),
    ]
    args = [z, z, z, cache_kv, pair_bias]
    aliases = {}
    if prev_kv is not None:
        in_specs.append(pl.BlockSpec(memory_space=pl.ANY))
        args.append(prev_kv)
        aliases = {len(args) - 1: 1}
    return pl.pallas_call(
        functools.partial(_attn_kernel, t_ctx=t_ctx, n_ctx=n_ctx),
        grid=(npair, n_grp),
        in_specs=in_specs,
        out_specs=[pl.BlockSpec((GROUP_ROWS, LANES), lambda j, g: (g, j)), kv_spec],
        out_shape=[jax.ShapeDtypeStruct((n_rows, D_A), BF16), kv_shape],
        input_output_aliases=aliases,
        compiler_params=_params(("arbitrary", "arbitrary"), 56),
        name="attention",
    )(*args)


def _rope_tables(t):
    half = HEAD_DIM // 2
    nf = half // 2
    inv = ROPE_BASE ** (-np.arange(nf, dtype=np.float32) / nf)
    pos = np.arange(t)
    ang_r = (pos // GRID_W).astype(np.float32)[:, None] * inv[None, :]
    ang_c = (pos % GRID_W).astype(np.float32)[:, None] * inv[None, :]
    ang_r, ang_c = jnp.asarray(ang_r), jnp.asarray(ang_c)
    cr, sr, cc, sc = jnp.cos(ang_r), jnp.sin(ang_r), jnp.cos(ang_c), jnp.sin(ang_c)
    cos = jnp.concatenate([cr, cr, cc, cc], axis=-1)
    sin = jnp.concatenate([-sr, sr, -sc, sc], axis=-1)
    return jnp.tile(cos, (1, HEADS_PER_TILE)), jnp.tile(sin, (1, HEADS_PER_TILE))


def _rope(x, cos, sin):
    nf = HEAD_DIM // 4
    lane = lax.broadcasted_iota(jnp.int32, x.shape, 1)
    partner = jnp.where(lane % (2 * nf) < nf, pltpu.roll(x, LANES - nf, axis=1), pltpu.roll(x, nf, axis=1))
    return x * cos + partner * sin


def _decay_matrix(dm_ref, lg, t):
    n_i = lax.broadcasted_iota(jnp.int32, (t, t), 0)
    m_i = lax.broadcasted_iota(jnp.int32, (t, t), 1)
    diff = (n_i - m_i).astype(F32)
    for hh in range(HEADS_PER_TILE):
        dm_ref[hh] = (jnp.where(diff >= 0, jnp.exp(lg[hh][0] * jnp.maximum(diff, 0.0)), 0.0)
                      + jnp.where(diff <= 0, jnp.exp(lg[hh][1] * jnp.maximum(-diff, 0.0)), 0.0))


def _retain(q_ref, k_ref, v_ref, g_ref, o_ref, dm_ref, lg, t, *, rope=None, s0_ref=None, st_ref=None):
    pos = lax.broadcasted_iota(jnp.int32, (t, 1), 0).astype(F32)
    for r in range(GROUP_ROWS // t):
        rs = slice(r * t, (r + 1) * t)
        q2 = q_ref[rs, :].astype(F32)
        k2 = k_ref[rs, :].astype(F32) * (HEAD_DIM ** -0.5)
        v2, g2 = v_ref[rs, :], g_ref[rs, :].astype(F32)
        if rope is not None:
            q2 = _rope(q2, rope[0][...], rope[1][...])
            k2 = _rope(k2, rope[0][...], rope[1][...])
        outs = []
        for hh in range(HEADS_PER_TILE):
            hs = slice(hh * HEAD_DIM, (hh + 1) * HEAD_DIM)
            lg_f, lg_b = lg[hh]
            q, k, vb = q2[:, hs], k2[:, hs], v2[:, hs]
            qb, kb = q.astype(BF16), k.astype(BF16)
            att = _dot_nt(qb, kb) * dm_ref[hh]
            o = _dot(att.astype(BF16), vb)
            if s0_ref is not None:
                qs = jnp.concatenate([q * jnp.exp(lg_f * (pos + 1.0)), q * jnp.exp(lg_b * (t - pos))], axis=-1)
                s0 = jnp.concatenate([s0_ref[0, hh], s0_ref[1, hh]], axis=0)
                o = o + _dot(qs.astype(BF16), s0.astype(BF16))
            if st_ref is not None:
                kf = k * jnp.exp(lg_f * (t - 1.0 - pos))
                kr = k * jnp.exp(lg_b * pos)
                st_ref[r, 0, 0, hh] = _dot_tn(kf.astype(BF16), vb)
                st_ref[r, 0, 1, hh] = _dot_tn(kr.astype(BF16), vb)
            on = o * lax.rsqrt(jnp.mean(o * o, axis=-1, keepdims=True) + EPS)
            outs.append(on * _silu(g2[:, hs]))
        o_ref[rs, :] = jnp.concatenate(outs, axis=-1).astype(o_ref.dtype)
    if st_ref is not None:
        _zero_other_layers(st_ref)


def _ret_kernel(dec_ref, q_ref, k_ref, v_ref, g_ref, cos_ref, sin_ref, s0_ref, *rest, t_ctx, t_lat, n_ctx):
    o_ref, st_ref, dmc_ref, dml_ref = rest[-4:]
    j, g = pl.program_id(0), pl.program_id(1)
    lg = [[-jnp.exp(jnp.full((1, 1), dec_ref[d, j * HEADS_PER_TILE + hh], F32)) for d in range(2)]
          for hh in range(HEADS_PER_TILE)]
    pl.when(g == 0)(functools.partial(_decay_matrix, dmc_ref, lg, t_ctx))
    pl.when(g == n_ctx)(functools.partial(_decay_matrix, dml_ref, lg, t_lat))
    is_ctx = g < n_ctx
    pl.when(is_ctx)(functools.partial(
        _retain, q_ref, k_ref, v_ref, g_ref, o_ref, dmc_ref, lg, t_ctx, st_ref=st_ref))
    pl.when(jnp.logical_not(is_ctx))(functools.partial(
        _retain, q_ref, k_ref, v_ref, g_ref, o_ref, dml_ref, lg, t_lat, rope=(cos_ref, sin_ref), s0_ref=s0_ref))


def _retention(z, ret_decay, rope_tabs, state_ret, e, n_layers, prev_st, n_ctx_req, t_ctx, t_lat, n_rows):
    npair = D_B // LANES
    hp = HEADS_PER_TILE
    sec0 = 3 * D_A // LANES
    n_ctx = n_ctx_req * t_ctx // GROUP_ROWS
    n_grp = n_rows // GROUP_ROWS
    assert t_lat == GROUP_ROWS
    blk = lambda p: pl.BlockSpec((GROUP_ROWS, LANES), lambda j, g, p=p: (g, sec0 + p * npair + j))
    st_spec, st_shape = _state_spec(n_ctx_req, n_layers, e, (2, N_HEADS_B, HEAD_DIM, HEAD_DIM),
                                    (2, hp, HEAD_DIM, HEAD_DIM), lambda j: (0, j, 0, 0), n_ctx,
                                    GROUP_ROWS // t_ctx)
    in_specs = [
        pl.BlockSpec(memory_space=pltpu.SMEM), blk(0), blk(1), blk(2), blk(3),
        pl.BlockSpec((t_lat, LANES), lambda j, g: (0, 0)),
        pl.BlockSpec((t_lat, LANES), lambda j, g: (0, 0)),
        pl.BlockSpec((None, None, 2, hp, HEAD_DIM, HEAD_DIM),
                     lambda j, g: (jnp.maximum(g - n_ctx, 0), e, 0, j, 0, 0)),
    ]
    args = [ret_decay, z, z, z, z, rope_tabs[0], rope_tabs[1], state_ret]
    aliases = {}
    if prev_st is not None:
        in_specs.append(pl.BlockSpec(memory_space=pl.ANY))
        args.append(prev_st)
        aliases = {len(args) - 1: 1}
    return pl.pallas_call(
        functools.partial(_ret_kernel, t_ctx=t_ctx, t_lat=t_lat, n_ctx=n_ctx),
        grid=(npair, n_grp),
        in_specs=in_specs,
        out_specs=[pl.BlockSpec((GROUP_ROWS, LANES), lambda j, g: (g, j)), st_spec],
        out_shape=[jax.ShapeDtypeStruct((n_rows, D_B), BF16), st_shape],
        input_output_aliases=aliases,
        scratch_shapes=[pltpu.VMEM((hp, t_ctx, t_ctx), F32), pltpu.VMEM((hp, t_lat, t_lat), F32)],
        compiler_params=_params(("arbitrary", "arbitrary"), 56),
        name="retention",
    )(*args)


def _seg_scan(x, reverse):
    t = x.shape[0]
    row = lax.broadcasted_iota(jnp.int32, (t, 1), 0) % HGRN_CHUNK
    sft = 1
    while sft < HGRN_CHUNK:
        if reverse:
            x = x + jnp.where(row < HGRN_CHUNK - sft, pltpu.roll(x, t - sft, axis=0), 0.0)
        else:
            x = x + jnp.where(row >= sft, pltpu.roll(x, sft, axis=0), 0.0)
        sft *= 2
    return x


def _hgrn_intra(q, k, v, c, reverse):
    cs = HGRN_CHUNK
    sub = 8
    row = lax.broadcasted_iota(jnp.int32, (cs, 1), 0)
    parts = [jnp.zeros((sub, v.shape[1]), F32) for _ in range(cs // sub)]
    for s in range(cs):
        blk_s = s // sub
        blks = range(0, blk_s + 1) if reverse else range(blk_s, cs // sub)
        c_s, k_s, v_s = c[s:s + 1], k[s:s + 1], v[s:s + 1]
        for bt in blks:
            rs = slice(bt * sub, (bt + 1) * sub)
            w = q[rs] * k_s * jnp.exp(jnp.minimum(c[rs] - c_s, 0.0))
            col = jnp.sum(w, axis=-1, keepdims=True)
            if bt == blk_s:
                keep = (row[rs] <= s) if reverse else (row[rs] >= s)
                col = jnp.where(keep, col, 0.0)
            parts[bt] = parts[bt] + col * v_s
    return jnp.concatenate(parts, axis=0)


def _hgrn_states(qe_ref, ke_ref, vb_ref, ee_ref, oi_ref, t, s0_ref=None, st_ref=None):
    sb = HGRN_SUPER
    n_sb = t // sb
    for r in range(GROUP_ROWS // t):
        incr = []
        for j in range(n_sb):
            rows = slice((r * n_sb + j) * sb, (r * n_sb + j + 1) * sb)
            keys = jnp.concatenate([ke_ref[0, rows, :], ke_ref[1, rows, :]], axis=1)
            incr.append(_dot_tn(keys, vb_ref[rows, :]))
        for d in range(2):
            st = s0_ref[d] if s0_ref is not None else None
            order = range(n_sb) if d == 0 else reversed(range(n_sb))
            for j in order:
                blk = r * n_sb + j
                rows = slice(blk * sb, (blk + 1) * sb)
                upd = incr[j][d * DK_C:(d + 1) * DK_C]
                if st is None:
                    st = upd
                else:
                    oi_ref[rows, :] += _dot(qe_ref[d, rows, :], st.astype(BF16))
                    decay = jnp.broadcast_to(ee_ref[d, blk][0:1], (DK_C, DK_C)).T
                    st = st * decay + upd
            if st_ref is not None:
                st_ref[r, 0, d] = st
    if st_ref is not None:
        _zero_other_layers(st_ref)


def _hgrn_kernel(q_ref, i_ref, g_ref, ff_ref, fb_ref, lb_ref, gn_ref, s0_ref, *rest, t_ctx, t_lat, n_ctx):
    (o_ref, st_ref, qe_ref, ke_ref, vb_ref, ee_ref, oi_ref,
     sq_ref, sk_ref, sc_ref, mask_ref, unsafe_ref) = rest[-12:]
    cs = HGRN_CHUNK
    sb = HGRN_SUPER
    half = sb // 2
    nch = sb // cs

    @pl.when((pl.program_id(0) == 0) & (pl.program_id(1) == 0))
    def _():
        t = lax.broadcasted_iota(jnp.int32, (sb, half), 0)
        s = lax.broadcasted_iota(jnp.int32, (sb, half), 1) + ((t >> 7) << 7)
        same32 = (t >> 5) == (s >> 5)
        same64 = (t >> 6) == (s >> 6)
        for d, (incl, strict) in enumerate(((s <= t, s < t), (s >= t, s > t))):
            for i, m in enumerate((same32 & incl, same64 & jnp.logical_not(same32) & strict,
                                   jnp.logical_not(same64) & strict)):
                mask_ref[d, i] = jnp.where(m, 1.0, 0.0)

    def block_step(blk, carry):
        rows = pl.ds(pl.multiple_of(blk * sb, sb), sb)
        qs = _silu(q_ref[rows, :].astype(F32)) * (DK_C ** -0.5)
        v = i_ref[rows, :]
        vb_ref[rows, :] = v
        per_dir = []
        c_min = None
        for d, f_ref in enumerate((ff_ref, fb_ref)):
            lb = lb_ref[d:d + 1, :]
            fr = f_ref[rows, :]
            e = jnp.exp(-jnp.abs(fr))
            r = 1.0 / (1.0 + e)
            sig_pos = jnp.where(fr >= 0, r, e * r)
            sig_neg = jnp.where(fr >= 0, e * r, r)
            f = lb + (1.0 - lb) * sig_pos
            k = (1.0 - lb) * sig_neg
            c = _seg_scan(jnp.log(jnp.maximum(f, F_MIN)), reverse=(d == 1))
            c3 = c.reshape(sb // cs, cs, DK_C)
            c_end = c3[:, cs - 1:cs, :] if d == 0 else c3[:, 0:1, :]
            tot = jnp.broadcast_to(c_end, c3.shape).reshape(sb, DK_C)
            per_dir.append((k, c, tot))
            m = jnp.min(c)
            c_min = m if c_min is None else jnp.minimum(c_min, m)

        safe = c_min >= -HGRN_SAFE_LOG
        diag = []
        quad = []
        for d, (k, c, tot) in enumerate(per_dir):
            q32 = qs * jnp.exp(c)
            k32 = k * jnp.exp(tot - c)
            k_hat = jnp.where(safe, k * jnp.exp(-c), 0.0).astype(BF16)
            e32 = [jnp.exp(tot[j * cs:j * cs + 1]) for j in range(nch)]

            def span(lo, hi):
                out = None
                for j in range(lo, hi):
                    out = e32[j] if out is None else out * e32[j]
                return out

            def rescaled(x, facs):
                parts = [x[j * cs:(j + 1) * cs] if f is None else x[j * cs:(j + 1) * cs] * f
                         for j, f in enumerate(facs)]
                return jnp.concatenate(parts, axis=0).astype(BF16)

            def level(n):
                before = [span((j // n) * n, j) for j in range(nch)]
                after = [span(j + 1, (j // n + 1) * n) for j in range(nch)]
                qf, kf = (before, after) if d == 0 else (after, before)
                return rescaled(q32, qf), rescaled(k32, kf)

            q32b, k32b = q32.astype(BF16), k32.astype(BF16)
            q64, k64 = level(2)
            q128, k128 = level(4)
            q256, k256 = level(8)
            diag.append((_dot_nt(q32b, k_hat), _dot_nt(q32b, k32b), _dot_nt(q64, k64)))
            if d == 0:
                quad.append(_dot_nt(q128[half:], k128[:half]))
            else:
                quad.append(_dot_nt(q128[:half], k128[half:]))
            qe_ref[d, rows, :] = q256
            ke_ref[d, rows, :] = k256
            ee_ref[d, blk] = jnp.broadcast_to(span(0, nch), (8, DK_C))

        pieces = []
        for j in range(nch):
            rs = slice(j * cs, (j + 1) * cs)
            own = j * cs // half
            cols = slice(own * half, (own + 1) * half)
            acc = None
            for d in range(2):
                for i in range(3):
                    term = diag[d][i][rs, cols] * mask_ref[d, i, rs, :]
                    acc = term if acc is None else acc + term
            other = quad[0][rs.start - half:rs.stop - half] if own == 1 else quad[1][rs]
            pieces.append(jnp.concatenate([other, acc] if own == 1 else [acc, other], axis=1))
        att = jnp.concatenate(pieces, axis=0)
        oi_ref[rows, :] = _dot(att.astype(BF16), v)

        unsafe_ref[blk] = jnp.logical_not(safe).astype(jnp.int32)
        sq_ref[rows, :] = qs
        for d in range(2):
            sk_ref[d, rows, :] = per_dir[d][0]
            sc_ref[d, rows, :] = per_dir[d][1]
        return carry

    n_blk = GROUP_ROWS // sb
    lax.fori_loop(0, n_blk, block_step, 0, unroll=True)

    def vpu_block(blk, carry):
        @pl.when(unsafe_ref[blk] != 0)
        def _():
            def chunk_step(i, carry2):
                crow = pl.ds(pl.multiple_of(blk * sb + i * cs, cs), cs)
                q, vv = sq_ref[crow, :], i_ref[crow, :].astype(F32)
                oi_ref[crow, :] += (_hgrn_intra(q, sk_ref[0, crow, :], vv, sc_ref[0, crow, :], reverse=False)
                                    + _hgrn_intra(q, sk_ref[1, crow, :], vv, sc_ref[1, crow, :], reverse=True))
                return carry2

            lax.fori_loop(0, sb // cs, chunk_step, 0)

        return carry

    lax.fori_loop(0, n_blk, vpu_block, 0)

    is_ctx = pl.program_id(1) < n_ctx
    pl.when(is_ctx)(functools.partial(
        _hgrn_states, qe_ref, ke_ref, vb_ref, ee_ref, oi_ref, t_ctx, st_ref=st_ref))
    pl.when(jnp.logical_not(is_ctx))(functools.partial(
        _hgrn_states, qe_ref, ke_ref, vb_ref, ee_ref, oi_ref, t_lat, s0_ref=s0_ref))

    o = oi_ref[...]
    on = o * lax.rsqrt(jnp.mean(o * o, axis=-1, keepdims=True) + EPS) * gn_ref[...]
    o_ref[...] = (on * _silu(g_ref[...].astype(F32))).astype(o_ref.dtype)


def _hgrn(z16, z32, lower, gnorm, state_hgrn, oi, n_layers, prev_st, n_ctx_req, t_ctx, t_lat, n_rows):
    nh = N_HEADS_C
    n_ctx = n_ctx_req * t_ctx // GROUP_ROWS
    n_grp = n_rows // GROUP_ROWS
    rows = GROUP_ROWS
    assert t_lat == GROUP_ROWS and t_ctx % HGRN_SUPER == 0
    blk = lambda p: pl.BlockSpec((rows, DK_C), lambda h, g, p=p: (g, p * nh + h))
    st_spec, st_shape = _state_spec(n_ctx_req, n_layers, oi, (2, nh, DK_C, DK_C),
                                    (2, None, DK_C, DK_C), lambda h: (0, h, 0, 0), n_ctx, GROUP_ROWS // t_ctx)
    in_specs = [blk(0), blk(1), blk(2), blk(0), blk(1),
                pl.BlockSpec((2, DK_C), lambda h, g: (0, h)),
                pl.BlockSpec((1, DK_C), lambda h, g: (0, 0)),
                pl.BlockSpec((None, None, 2, None, DK_C, DK_C),
                             lambda h, g: (jnp.maximum(g - n_ctx, 0), oi, 0, h, 0, 0))]
    args = [z16, z16, z16, z32, z32, lower, gnorm, state_hgrn]
    aliases = {}
    if prev_st is not None:
        in_specs.append(pl.BlockSpec(memory_space=pl.ANY))
        args.append(prev_st)
        aliases = {len(args) - 1: 1}
    return pl.pallas_call(
        functools.partial(_hgrn_kernel, t_ctx=t_ctx, t_lat=t_lat, n_ctx=n_ctx),
        grid=(nh, n_grp),
        in_specs=in_specs,
        out_specs=[pl.BlockSpec((rows, DK_C), lambda h, g: (g, h)), st_spec],
        out_shape=[jax.ShapeDtypeStruct((n_rows, D_MODEL), BF16), st_shape],
        input_output_aliases=aliases,
        scratch_shapes=[
            pltpu.VMEM((2, rows, DK_C), BF16),
            pltpu.VMEM((2, rows, DK_C), BF16),
            pltpu.VMEM((rows, DK_C), BF16),
            pltpu.VMEM((2, rows // HGRN_SUPER, 8, DK_C), F32),
            pltpu.VMEM((rows, DK_C), F32),
            pltpu.VMEM((rows, DK_C), F32),
            pltpu.VMEM((2, rows, DK_C), F32),
            pltpu.VMEM((2, rows, DK_C), F32),
            pltpu.VMEM((2, 3, HGRN_SUPER, HGRN_SUPER // 2), F32),
            pltpu.SMEM((rows // HGRN_SUPER,), jnp.int32),
        ],
        compiler_params=_params(("arbitrary", "arbitrary"), 32),
        name="hgrn",
    )(*args)


def kernel(x_prompt, x_sample, cache_kv, state_ret, state_hgrn, c, c_ctx, w_mod, b_mod, norm_g,
           w_in_even, w_out_even, rpb, ret_decay, w_in_odd, w_out_odd, hgrn_lb, hgrn_gnorm,
           w_ffn_in, w_ffn_out):
    bp, tp, _ = x_prompt.shape
    bs, ts, _ = x_sample.shape
    np_rows, ns_rows = bp * tp, bs * ts
    n_rows = np_rows + ns_rows
    n_even, n_odd = w_in_even.shape[0], w_in_odd.shape[0]
    assert np_rows % GROUP_ROWS == 0 and ts == GROUP_ROWS
    xs = [x_prompt.reshape(np_rows, D_MODEL), x_sample.reshape(ns_rows, D_MODEL)]

    n_c = bs + 1
    pad = (-n_c) % 8
    cvec = jnp.concatenate([c, c_ctx[None], jnp.zeros((pad, D_MODEL), F32)], axis=0)
    mod_all = _modulation(cvec, w_mod, b_mod)
    grp = np.concatenate([np.full(np_rows // GROUP_ROWS, bs), np.arange(bs)]).astype(np.int32)
    mod_all = mod_all[:, grp].reshape(DEPTH, len(grp), 6, D_MODEL)
    mod_all = jnp.pad(mod_all, ((0, 0), (0, 0), (0, MOD_ROWS - 6), (0, 0)))

    p_lb = jax.nn.softmax(hgrn_lb.astype(F32), axis=0)
    lower = jnp.clip(jnp.cumsum(p_lb, axis=0) - p_lb[0], 0.0, 1.0)
    pair_bias = _na_bias_tables(rpb)
    rope_tabs = _rope_tables(ts)

    w_in_even_b, w_out_even_b = w_in_even.astype(BF16), w_out_even.astype(BF16)
    w_out_odd_b = w_out_odd.astype(BF16)
    w_in_odd_q = w_in_odd[:, :, :D_C].astype(BF16)
    w_in_odd_f = w_in_odd[:, :, D_C:3 * D_C].astype(BF16)
    w_in_odd_ig = w_in_odd[:, :, 3 * D_C:].astype(BF16)
    ffn_w = [_ffn_weights(w_ffn_in[l], w_ffn_out[l]) for l in range(DEPTH)]

    kv_out = ret_out = hg_out = None
    y_split = None
    for l in range(DEPTH):
        mod = mod_all[l]
        g = norm_g[l].reshape(4, 1, D_MODEL)
        if l % 2 == 0:
            e = l // 2
            z = _in_proj(xs, g[0], mod, [w_in_even_b[e]], BF16)
            oa, kv_out = _attention(z, cache_kv, pair_bias, e, n_even, kv_out, bp, tp, n_rows)
            ob, ret_out = _retention(z, ret_decay[e], rope_tabs, state_ret, e, n_even, ret_out, bp, tp, ts, n_rows)
            x = _out_proj([oa, ob], w_out_even_b[e], xs, g[1], mod)
        else:
            oi = l // 2
            z16 = _in_proj(xs, g[0], mod, [w_in_odd_q[oi], w_in_odd_ig[oi]], BF16)
            z32 = _in_proj(xs, g[0], mod, [w_in_odd_f[oi]], F32)
            gn = hgrn_gnorm[oi].reshape(1, DK_C)
            o, hg_out = _hgrn(z16, z32, lower[oi], gn, state_hgrn, oi, n_odd, hg_out, bp, tp, ts, n_rows)
            x = _out_proj([o], w_out_odd_b[oi], xs, g[1], mod)
        if l == DEPTH - 1:
            y_split = _ffn(x, g[2], g[3], mod, *ffn_w[l], split=(x_prompt.shape, x_sample.shape))
        else:
            xs = [_ffn(x, g[2], g[3], mod, *ffn_w[l])]

    return (y_split[0], y_split[1], kv_out, ret_out, hg_out)
```

```python
import functools

import numpy as np
import jax
import jax.numpy as jnp
from jax import lax
from jax.experimental import pallas as pl
from jax.experimental.pallas import tpu as pltpu

F32 = jnp.float32
BF16 = jnp.bfloat16

D_MODEL = 1024
DEPTH = 4
GRID_W = 64
HEAD_DIM = 64
N_HEADS_A = 8
N_HEADS_B = 8
D_A = N_HEADS_A * HEAD_DIM
D_B = N_HEADS_B * HEAD_DIM
WIN_R = 8
WIN_C = 16
N_HEADS_C = 8
DK_C = D_MODEL // N_HEADS_C
D_C = N_HEADS_C * DK_C
D_FF = ((8 * D_MODEL // 3 + 255) // 256) * 256
ATTN_SCALE = HEAD_DIM ** -0.5
HGRN_CHUNK = 32
HGRN_SUPER = 256
HGRN_SAFE_LOG = 75.0
ROPE_BASE = 10000.0
EPS = 1e-6
MASK_NEG = -1e30
F_MIN = 1e-30

GROUP_ROWS = 1024
MOD_ROWS = 8
LANES = 128
HEADS_PER_TILE = LANES // HEAD_DIM
PROJ_TN = 512
FFN_TF = 256
MOD_TN = 1536
MIB = 1024 * 1024

SH1, SC1, GT1, SH2, SC2, GT2 = range(6)

NA_QROWS = 4
NA_KROWS = 12
NA_KSTART = (0, 0, 4, 4)


def _params(sem, vmem_mib):
    return pltpu.CompilerParams(dimension_semantics=sem, vmem_limit_bytes=vmem_mib * MIB)


def _sigmoid(x):
    return 1.0 / (1.0 + jnp.exp(-x))


def _silu(x):
    return x * _sigmoid(x)


def _dot(a, b):
    return jnp.dot(a, b, preferred_element_type=F32)


def _dot_nt(a, b):
    return lax.dot_general(a, b, (((1,), (1,)), ((), ())), preferred_element_type=F32)


def _dot_tn(a, b):
    return lax.dot_general(a, b, (((0,), (0,)), ((), ())), preferred_element_type=F32)


def _mod_kernel(c_ref, w_ref, b_ref, o_ref):
    s = _silu(c_ref[...]).astype(BF16)
    o_ref[...] = _dot(s, w_ref[...].astype(BF16)) + b_ref[...]


def _modulation(cvec, w_mod, b_mod):
    rows = cvec.shape[0]
    n = w_mod.shape[-1]
    return pl.pallas_call(
        _mod_kernel,
        grid=(DEPTH, n // MOD_TN),
        in_specs=[
            pl.BlockSpec((rows, D_MODEL), lambda l, j: (0, 0)),
            pl.BlockSpec((None, D_MODEL, MOD_TN), lambda l, j: (l, 0, j)),
            pl.BlockSpec((None, 1, MOD_TN), lambda l, j: (l, 0, j)),
        ],
        out_specs=pl.BlockSpec((None, rows, MOD_TN), lambda l, j: (l, 0, j)),
        out_shape=jax.ShapeDtypeStruct((DEPTH, rows, n), F32),
        compiler_params=_params(("arbitrary", "arbitrary"), 32),
        name="modulation",
    )(cvec, w_mod, b_mod.reshape(DEPTH, 1, n))


def _norm_mod(x, g, mod, sh_row, sc_row):
    y = x * lax.rsqrt(jnp.mean(x * x, axis=-1, keepdims=True) + EPS) * g
    return y * (1.0 + mod[sc_row:sc_row + 1]) + mod[sh_row:sh_row + 1]


def _stream_specs(xs):
    tm = GROUP_ROWS
    if len(xs) == 1:
        return [pl.BlockSpec((tm, D_MODEL), lambda i: (i, 0))]
    n_ctx = xs[0].shape[0] // tm
    return [pl.BlockSpec((tm, D_MODEL), lambda i: (jnp.minimum(i, n_ctx - 1), 0)),
            pl.BlockSpec((tm, D_MODEL), lambda i: (jnp.maximum(i - n_ctx, 0), 0))]


def _stream_tile(x_refs, n_ctx):
    if len(x_refs) == 1:
        return x_refs[0][...]
    return jnp.where(pl.program_id(0) < n_ctx, x_refs[0][...], x_refs[1][...])


def _in_proj_kernel(*refs, n_x, n_ctx, n_w):
    x_refs, (g_ref, mod_ref) = refs[:n_x], refs[n_x:n_x + 2]
    w_refs, o_refs = refs[n_x + 2:n_x + 2 + sum(n_w)], refs[n_x + 2 + sum(n_w):]
    h = _norm_mod(_stream_tile(x_refs, n_ctx), g_ref[...], mod_ref[...], SH1, SC1).astype(BF16)
    w_iter = iter(w_refs)
    for o_ref, count in zip(o_refs, n_w):
        col0 = 0
        for w_ref in (next(w_iter) for _ in range(count)):
            for j in range(w_ref.shape[1] // PROJ_TN):
                cols = slice(j * PROJ_TN, (j + 1) * PROJ_TN)
                o_ref[:, col0 + j * PROJ_TN:col0 + (j + 1) * PROJ_TN] = _dot(h, w_ref[:, cols]).astype(o_ref.dtype)
            col0 += w_ref.shape[1]


def _resident(block_shape):
    return pl.BlockSpec(block_shape, lambda *_: (0,) * len(block_shape), pipeline_mode=pl.Buffered(1))


def _in_proj(xs, g, mod, outputs):
    m = sum(x.shape[0] for x in xs)
    tm = GROUP_ROWS
    ws = [w for pieces, _ in outputs for w in pieces]
    ns = [sum(w.shape[1] for w in pieces) for pieces, _ in outputs]
    assert all(w.shape[1] % PROJ_TN == 0 for w in ws)
    return pl.pallas_call(
        functools.partial(_in_proj_kernel, n_x=len(xs), n_ctx=xs[0].shape[0] // tm,
                          n_w=tuple(len(pieces) for pieces, _ in outputs)),
        grid=(m // tm,),
        in_specs=_stream_specs(xs) + [
            pl.BlockSpec((1, D_MODEL), lambda i: (0, 0)),
            pl.BlockSpec((None, MOD_ROWS, D_MODEL), lambda i: (i, 0, 0)),
        ] + [_resident(w.shape) for w in ws],
        out_specs=[pl.BlockSpec((tm, n), lambda i: (i, 0)) for n in ns],
        out_shape=[jax.ShapeDtypeStruct((m, n), dt) for n, (_, dt) in zip(ns, outputs)],
        compiler_params=_params(("arbitrary",), 56),
        name="in_proj",
    )(*xs, g, mod, *ws)


def _out_proj_kernel(*refs, n_in, n_x, n_ctx):
    a_refs, w_refs = refs[:n_in], refs[n_in:2 * n_in]
    x_refs = refs[2 * n_in:2 * n_in + n_x]
    g_ref, mod_ref, o_ref = refs[2 * n_in + n_x:]
    y = _dot(a_refs[0][...], w_refs[0][...])
    for a_ref, w_ref in zip(a_refs[1:], w_refs[1:]):
        y = y + _dot(a_ref[...], w_ref[...])
    yn = y * lax.rsqrt(jnp.mean(y * y, axis=-1, keepdims=True) + EPS) * g_ref[...]
    o_ref[...] = _stream_tile(x_refs, n_ctx) + mod_ref[GT1:GT1 + 1, :] * yn


def _out_proj(acts, w, xs, g, mod):
    m = sum(x.shape[0] for x in xs)
    tm = GROUP_ROWS
    n_in = len(acts)
    ks = [a.shape[1] for a in acts]
    assert sum(ks) == w.shape[0] and len(set(ks)) == 1
    in_specs = [pl.BlockSpec((tm, k), lambda i: (i, 0)) for k in ks]
    in_specs += [pl.BlockSpec((ks[0], D_MODEL), lambda i, p=p: (p, 0)) for p in range(n_in)]
    in_specs += _stream_specs(xs) + [
        pl.BlockSpec((1, D_MODEL), lambda i: (0, 0)),
        pl.BlockSpec((None, MOD_ROWS, D_MODEL), lambda i: (i, 0, 0)),
    ]
    return pl.pallas_call(
        functools.partial(_out_proj_kernel, n_in=n_in, n_x=len(xs), n_ctx=xs[0].shape[0] // tm),
        grid=(m // tm,),
        in_specs=in_specs,
        out_specs=pl.BlockSpec((tm, D_MODEL), lambda i: (i, 0)),
        out_shape=jax.ShapeDtypeStruct((m, D_MODEL), F32),
        compiler_params=_params(("arbitrary",), 48),
        name="out_proj",
    )(*acts, *([w] * n_in), *xs, g, mod)


def _ffn_kernel(x_ref, xn_ref, g2_ref, g3_ref, mod_ref, modn_ref, wa_ref, wu_ref, wo_ref, *rest, n_ctx_groups):
    o_refs, (h_ref, acc_ref) = rest[:-2], rest[-2:]
    tm = x_ref.shape[0]
    nf = wa_ref.shape[0]
    ahead = -(-tm // (nf * 16)) * 16
    i = pl.program_id(0)
    cur, nxt = i % 2, (i + 1) % 2

    @pl.when(i == 0)
    def _():
        h_ref[0] = _norm_mod(x_ref[...], g2_ref[...], mod_ref[...], SH2, SC2).astype(BF16)

    acc_ref[...] = jnp.zeros_like(acc_ref)

    def hidden_step(f, carry):
        h = h_ref[cur]
        a = _dot(h, wa_ref[f])
        u = _dot(h, wu_ref[f])
        acc_ref[...] += _dot((_silu(a) * u).astype(BF16), wo_ref[f])
        r0 = pl.multiple_of(jnp.minimum(f * ahead, tm - ahead), 16)
        rows = pl.ds(r0, ahead)
        h_ref[nxt, rows, :] = _norm_mod(xn_ref[rows, :], g2_ref[...], modn_ref[...], SH2, SC2).astype(BF16)
        return carry

    lax.fori_loop(0, nf, hidden_step, 0)

    def finish(o_ref):
        y = acc_ref[...]
        yn = y * lax.rsqrt(jnp.mean(y * y, axis=-1, keepdims=True) + EPS) * g3_ref[...]
        o_ref[...] = (x_ref[...] + mod_ref[GT2:GT2 + 1, :] * yn).reshape(o_ref.shape)

    if len(o_refs) == 1:
        finish(o_refs[0])
    else:
        is_ctx = pl.program_id(0) < n_ctx_groups
        pl.when(is_ctx)(functools.partial(finish, o_refs[0]))
        pl.when(jnp.logical_not(is_ctx))(functools.partial(finish, o_refs[1]))


def _ffn_weights(w_in, w_out):
    nf = D_FF // FFN_TF
    w_in = w_in.reshape(D_MODEL, 2, nf, FFN_TF)
    wa = jnp.transpose(w_in[:, 0], (1, 0, 2)).astype(BF16)
    wu = jnp.transpose(w_in[:, 1], (1, 0, 2)).astype(BF16)
    return wa, wu, w_out.astype(BF16).reshape(nf, FFN_TF, D_MODEL)


def _ffn(x, g2, g3, mod, wa, wu, wo, split=None):
    m = x.shape[0]
    tm = GROUP_ROWS
    n_ctx = 0
    if split is None:
        out_specs = pl.BlockSpec((tm, D_MODEL), lambda i: (i, 0))
        out_shape = jax.ShapeDtypeStruct((m, D_MODEL), F32)
    else:
        (bp, tp, _), (bs, ts, _) = split
        n_ctx = bp * tp // tm
        assert ts == tm and (bp * tp) % tm == 0
        out_specs = [
            pl.BlockSpec((tm // tp, tp, D_MODEL), lambda i: (jnp.minimum(i, n_ctx - 1), 0, 0)),
            pl.BlockSpec((1, ts, D_MODEL), lambda i: (jnp.maximum(i - n_ctx, 0), 0, 0)),
        ]
        out_shape = [jax.ShapeDtypeStruct(s, F32) for s in split]
    last = m // tm - 1
    return pl.pallas_call(
        functools.partial(_ffn_kernel, n_ctx_groups=n_ctx),
        grid=(m // tm,),
        in_specs=[
            pl.BlockSpec((tm, D_MODEL), lambda i: (i, 0)),
            pl.BlockSpec((tm, D_MODEL), lambda i: (jnp.minimum(i + 1, last), 0)),
            pl.BlockSpec((1, D_MODEL), lambda i: (0, 0)),
            pl.BlockSpec((1, D_MODEL), lambda i: (0, 0)),
            pl.BlockSpec((None, MOD_ROWS, D_MODEL), lambda i: (i, 0, 0)),
            pl.BlockSpec((None, MOD_ROWS, D_MODEL), lambda i: (jnp.minimum(i + 1, last), 0, 0)),
            _resident(wa.shape), _resident(wu.shape), _resident(wo.shape),
        ],
        out_specs=out_specs,
        out_shape=out_shape,
        scratch_shapes=[pltpu.VMEM((2, tm, D_MODEL), BF16), pltpu.VMEM((tm, D_MODEL), F32)],
        compiler_params=_params(("arbitrary",), 60),
        name="ffn",
    )(x, x, g2, g3, mod, mod, wa, wu, wo)


def _state_spec(n_req, n_layers, layer, tail, tail_block, tail_index, n_ctx, nb):
    n_own = n_layers if layer == 0 else 1
    first = 0 if layer == 0 else layer

    def index(j, g):
        return (jnp.minimum(g, n_ctx - 1), first) + tail_index(j)

    spec = pl.BlockSpec((nb, n_own) + tail_block, index)
    shape = jax.ShapeDtypeStruct((n_req, n_layers) + tail, F32)
    return spec, shape


def _zero_other_layers(st_ref):
    if st_ref.shape[1] > 1:
        st_ref[:, 1:] = jnp.zeros((st_ref.shape[0], st_ref.shape[1] - 1) + st_ref.shape[2:], st_ref.dtype)


NA_NRO = 2 * WIN_R - 1


def _na_bias_tables(rpb):
    qc = np.arange(GRID_W)[:, None]
    kc = np.arange(GRID_W)[None, :]
    win0 = np.clip(qc - WIN_C // 2, 0, GRID_W - WIN_C)
    col_valid = (kc >= win0) & (kc < win0 + WIN_C)
    col_off = np.clip(kc - qc, 1 - WIN_C, WIN_C - 1) + (WIN_C - 1)
    onehot = (col_off[None] == np.arange(2 * WIN_C - 1)[:, None, None]).astype(np.float32)
    tiles = jnp.einsum('ehrc,cqk->ehrqk', rpb.astype(F32), jnp.asarray(onehot), precision=lax.Precision.HIGHEST)
    tiles = jnp.where(col_valid[None, None, None], tiles, MASK_NEG)
    neg = jnp.full_like(tiles, MASK_NEG)
    nxt = jnp.concatenate([tiles[:, :, 1:], neg[:, :, :1]], axis=2)
    both = jnp.concatenate([tiles, nxt], axis=-1)
    first = jnp.concatenate([tiles, neg], axis=-1)
    second = jnp.concatenate([neg, tiles], axis=-1)
    none = jnp.concatenate([neg[:, :, :1], neg[:, :, :1]], axis=-1)
    return jnp.concatenate([both, first, second, none], axis=2)


def _na_group_bias(pair_ref, hh, g):
    rows = GROUP_ROWS // GRID_W
    bias = []
    for a in range(NA_QROWS):
        r = g * NA_QROWS + a
        row0 = min(max(r - WIN_R // 2, 0), rows - WIN_R)
        pieces = []
        for w in range(0, NA_KROWS, 2):
            kr = NA_KSTART[g] + w
            ro = kr - r + (WIN_R - 1)
            in0 = row0 <= kr < row0 + WIN_R
            in1 = row0 <= kr + 1 < row0 + WIN_R
            slot = ro if in0 and in1 else NA_NRO + ro if in0 else 2 * NA_NRO + ro + 1 if in1 else 3 * NA_NRO
            pieces.append(pair_ref[hh, slot])
        bias.append(jnp.concatenate(pieces, axis=1))
    return jnp.concatenate(bias, axis=0)


def _ctx_attend(q_ref, k_ref, v_ref, o_ref, kv_ref, t):
    for r in range(GROUP_ROWS // t):
        rs = slice(r * t, (r + 1) * t)
        q2, k2, v2 = q_ref[rs, :] * ATTN_SCALE, k_ref[rs, :], v_ref[rs, :]
        head_of_lane = lax.broadcasted_iota(jnp.int32, q2.shape, 1) // HEAD_DIM
        out = None
        for hh in range(HEADS_PER_TILE):
            hs = slice(hh * HEAD_DIM, (hh + 1) * HEAD_DIM)
            kv_ref[r, 0, 0, hh] = k2[:, hs].astype(F32)
            kv_ref[r, 0, 1, hh] = v2[:, hs].astype(F32)
            in_head = head_of_lane == hh
            s = _dot_nt(jnp.where(in_head, q2, jnp.zeros_like(q2)), k2)
            p = jnp.exp(s - jnp.max(s, axis=-1, keepdims=True))
            l = jnp.sum(p, axis=-1, keepdims=True)
            o = _dot(p.astype(BF16), v2) / l
            out = o if out is None else jnp.where(in_head, o, out)
        o_ref[rs, :] = out.astype(o_ref.dtype)
    _zero_other_layers(kv_ref)


def _na_attend(q_ref, k_ref, v_ref, ckv_ref, pair_ref, o_ref):
    q2, k2, v2 = q_ref[...] * ATTN_SCALE, k_ref[...], v_ref[...]
    tq = NA_QROWS * GRID_W
    tk = NA_KROWS * GRID_W
    head_of_lane = lax.broadcasted_iota(jnp.int32, q2.shape, 1) // HEAD_DIM
    kc2 = jnp.concatenate([ckv_ref[0, hh] for hh in range(HEADS_PER_TILE)], axis=1).astype(BF16)
    vc2 = jnp.concatenate([ckv_ref[1, hh] for hh in range(HEADS_PER_TILE)], axis=1).astype(BF16)
    out = None
    for hh in range(HEADS_PER_TILE):
        in_head = head_of_lane == hh
        q = jnp.where(in_head, q2, jnp.zeros_like(q2))
        rows = []
        for g in range(len(NA_KSTART)):
            qg = q[g * tq:(g + 1) * tq]
            k0 = NA_KSTART[g] * GRID_W
            s_loc = _dot_nt(qg, k2[k0:k0 + tk]) + _na_group_bias(pair_ref, hh, g)
            s_ctx = _dot_nt(qg, kc2)
            m = jnp.maximum(jnp.max(s_loc, axis=-1, keepdims=True), jnp.max(s_ctx, axis=-1, keepdims=True))
            p_loc = jnp.exp(s_loc - m)
            p_ctx = jnp.exp(s_ctx - m)
            l = jnp.sum(p_loc, axis=-1, keepdims=True) + jnp.sum(p_ctx, axis=-1, keepdims=True)
            o = _dot(p_loc.astype(BF16), v2[k0:k0 + tk]) + _dot(p_ctx.astype(BF16), vc2)
            rows.append(o / l)
        o_h = jnp.concatenate(rows, axis=0)
        out = o_h if out is None else jnp.where(in_head, o_h, out)
    o_ref[...] = out.astype(o_ref.dtype)


def _attn_kernel(q_ref, k_ref, v_ref, ckv_ref, pair_ref, *rest, t_ctx, n_ctx):
    o_ref, kv_ref = rest[-2:]
    is_ctx = pl.program_id(1) < n_ctx
    pl.when(is_ctx)(functools.partial(_ctx_attend, q_ref, k_ref, v_ref, o_ref, kv_ref, t_ctx))
    pl.when(jnp.logical_not(is_ctx))(
        functools.partial(_na_attend, q_ref, k_ref, v_ref, ckv_ref, pair_ref, o_ref))


def _attention(z, cache_kv, pair_bias, e, n_layers, prev_kv, n_ctx_req, t_ctx, n_rows):
    npair = D_A // LANES
    hp = HEADS_PER_TILE
    n_ctx = n_ctx_req * t_ctx // GROUP_ROWS
    n_grp = n_rows // GROUP_ROWS
    past = cache_kv.shape[-2]
    blk = lambda p: pl.BlockSpec((GROUP_ROWS, LANES), lambda j, g, p=p: (g, p * npair + j))
    kv_spec, kv_shape = _state_spec(n_ctx_req, n_layers, e, (2, N_HEADS_A, t_ctx, HEAD_DIM),
                                    (2, hp, t_ctx, HEAD_DIM), lambda j: (0, j, 0, 0), n_ctx, GROUP_ROWS // t_ctx)
    in_specs = [
        blk(0), blk(1), blk(2),
        pl.BlockSpec((None, None, 2, hp, past, HEAD_DIM), lambda j, g: (jnp.maximum(g - n_ctx, 0), e, 0, j, 0, 0)),
        pl.BlockSpec((None, hp) + pair_bias.shape[2:], lambda j, g: (e, j, 0, 0, 0)),
    ]
    args = [z, z, z, cache_kv, pair_bias]
    aliases = {}
    if prev_kv is not None:
        in_specs.append(pl.BlockSpec(memory_space=pl.ANY))
        args.append(prev_kv)
        aliases = {len(args) - 1: 1}
    return pl.pallas_call(
        functools.partial(_attn_kernel, t_ctx=t_ctx, n_ctx=n_ctx),
        grid=(npair, n_grp),
        in_specs=in_specs,
        out_specs=[pl.BlockSpec((GROUP_ROWS, LANES), lambda j, g: (g, j)), kv_spec],
        out_shape=[jax.ShapeDtypeStruct((n_rows, D_A), BF16), kv_shape],
        input_output_aliases=aliases,
        compiler_params=_params(("arbitrary", "arbitrary"), 56),
        name="attention",
    )(*args)


def _rope_tables(t):
    half = HEAD_DIM // 2
    nf = half // 2
    inv = ROPE_BASE ** (-np.arange(nf, dtype=np.float32) / nf)
    pos = np.arange(t)
    ang_r = (pos // GRID_W).astype(np.float32)[:, None] * inv[None, :]
    ang_c = (pos % GRID_W).astype(np.float32)[:, None] * inv[None, :]
    ang_r, ang_c = jnp.asarray(ang_r), jnp.asarray(ang_c)
    cr, sr, cc, sc = jnp.cos(ang_r), jnp.sin(ang_r), jnp.cos(ang_c), jnp.sin(ang_c)
    cos = jnp.concatenate([cr, cr, cc, cc], axis=-1)
    sin = jnp.concatenate([-sr, sr, -sc, sc], axis=-1)
    return jnp.tile(cos, (1, HEADS_PER_TILE)), jnp.tile(sin, (1, HEADS_PER_TILE))


def _rope(x, cos, sin):
    nf = HEAD_DIM // 4
    lane = lax.broadcasted_iota(jnp.int32, x.shape, 1)
    partner = jnp.where(lane % (2 * nf) < nf, pltpu.roll(x, LANES - nf, axis=1), pltpu.roll(x, nf, axis=1))
    return x * cos + partner * sin


def _decay_matrix(dm_ref, lg, t):
    n_i = lax.broadcasted_iota(jnp.int32, (t, t), 0)
    m_i = lax.broadcasted_iota(jnp.int32, (t, t), 1)
    diff = (n_i - m_i).astype(F32)
    for hh in range(HEADS_PER_TILE):
        dm_ref[hh] = (jnp.where(diff >= 0, jnp.exp(lg[hh][0] * jnp.maximum(diff, 0.0)), 0.0)
                      + jnp.where(diff <= 0, jnp.exp(lg[hh][1] * jnp.maximum(-diff, 0.0)), 0.0))


def _retain(q_ref, k_ref, v_ref, g_ref, o_ref, dm_ref, lg, t, *, rope=None, s0_ref=None, st_ref=None):
    pos = lax.broadcasted_iota(jnp.int32, (t, 1), 0).astype(F32)
    head_of_lane = lax.broadcasted_iota(jnp.int32, (1, LANES), 1) // HEAD_DIM
    lgv = [sum(jnp.where(head_of_lane == hh, lg[hh][d], 0.0) for hh in range(HEADS_PER_TILE)) for d in range(2)]

    def block_diag(d):
        zero = jnp.zeros((HEAD_DIM, HEAD_DIM), F32)
        rows = [jnp.concatenate([s0_ref[d, hh] if c == hh else zero for c in range(HEADS_PER_TILE)], axis=1)
                for hh in range(HEADS_PER_TILE)]
        return jnp.concatenate(rows, axis=0).astype(BF16)

    for r in range(GROUP_ROWS // t):
        rs = slice(r * t, (r + 1) * t)
        q2 = q_ref[rs, :].astype(F32)
        k2 = k_ref[rs, :].astype(F32) * (HEAD_DIM ** -0.5)
        v2, g2 = v_ref[rs, :], g_ref[rs, :].astype(F32)
        if rope is not None:
            q2 = _rope(q2, rope[0][...], rope[1][...])
            k2 = _rope(k2, rope[0][...], rope[1][...])
        k2b = k2.astype(BF16)
        if st_ref is not None:
            st = (_dot_tn((k2 * jnp.exp(lgv[0] * (t - 1.0 - pos))).astype(BF16), v2),
                  _dot_tn((k2 * jnp.exp(lgv[1] * pos)).astype(BF16), v2))
            for hh in range(HEADS_PER_TILE):
                hs = slice(hh * HEAD_DIM, (hh + 1) * HEAD_DIM)
                st_ref[r, 0, 0, hh] = st[0][hs, hs]
                st_ref[r, 0, 1, hh] = st[1][hs, hs]
        o = None
        for hh in range(HEADS_PER_TILE):
            in_head = head_of_lane == hh
            att = _dot_nt(jnp.where(in_head, q2, 0.0).astype(BF16), k2b) * dm_ref[hh]
            o_h = _dot(att.astype(BF16), v2)
            o = o_h if o is None else jnp.where(in_head, o_h, o)
        if s0_ref is not None:
            o = o + _dot((q2 * jnp.exp(lgv[0] * (pos + 1.0))).astype(BF16), block_diag(0))
            o = o + _dot((q2 * jnp.exp(lgv[1] * (t - pos))).astype(BF16), block_diag(1))
        sq = o * o
        ms = sum(jnp.where(head_of_lane == hh,
                           jnp.sum(jnp.where(head_of_lane == hh, sq, 0.0), axis=-1, keepdims=True), 0.0)
                 for hh in range(HEADS_PER_TILE)) * (1.0 / HEAD_DIM)
        o_ref[rs, :] = (o * lax.rsqrt(ms + EPS) * _silu(g2)).astype(o_ref.dtype)
    if st_ref is not None:
        _zero_other_layers(st_ref)


def _ret_kernel(dec_ref, q_ref, k_ref, v_ref, g_ref, cos_ref, sin_ref, s0_ref, *rest, t_ctx, t_lat, n_ctx):
    o_ref, st_ref, dmc_ref, dml_ref = rest[-4:]
    j, g = pl.program_id(0), pl.program_id(1)
    lg = [[-jnp.exp(jnp.full((1, 1), dec_ref[d, j * HEADS_PER_TILE + hh], F32)) for d in range(2)]
          for hh in range(HEADS_PER_TILE)]
    pl.when(g == 0)(functools.partial(_decay_matrix, dmc_ref, lg, t_ctx))
    pl.when(g == n_ctx)(functools.partial(_decay_matrix, dml_ref, lg, t_lat))
    is_ctx = g < n_ctx
    pl.when(is_ctx)(functools.partial(
        _retain, q_ref, k_ref, v_ref, g_ref, o_ref, dmc_ref, lg, t_ctx, st_ref=st_ref))
    pl.when(jnp.logical_not(is_ctx))(functools.partial(
        _retain, q_ref, k_ref, v_ref, g_ref, o_ref, dml_ref, lg, t_lat, rope=(cos_ref, sin_ref), s0_ref=s0_ref))


def _retention(z, ret_decay, rope_tabs, state_ret, e, n_layers, prev_st, n_ctx_req, t_ctx, t_lat, n_rows):
    npair = D_B // LANES
    hp = HEADS_PER_TILE
    sec0 = 3 * D_A // LANES
    n_ctx = n_ctx_req * t_ctx // GROUP_ROWS
    n_grp = n_rows // GROUP_ROWS
    assert t_lat == GROUP_ROWS
    blk = lambda p: pl.BlockSpec((GROUP_ROWS, LANES), lambda j, g, p=p: (g, sec0 + p * npair + j))
    st_spec, st_shape = _state_spec(n_ctx_req, n_layers, e, (2, N_HEADS_B, HEAD_DIM, HEAD_DIM),
                                    (2, hp, HEAD_DIM, HEAD_DIM), lambda j: (0, j, 0, 0), n_ctx,
                                    GROUP_ROWS // t_ctx)
    in_specs = [
        pl.BlockSpec(memory_space=pltpu.SMEM), blk(0), blk(1), blk(2), blk(3),
        pl.BlockSpec((t_lat, LANES), lambda j, g: (0, 0)),
        pl.BlockSpec((t_lat, LANES), lambda j, g: (0, 0)),
        pl.BlockSpec((None, None, 2, hp, HEAD_DIM, HEAD_DIM),
                     lambda j, g: (jnp.maximum(g - n_ctx, 0), e, 0, j, 0, 0)),
    ]
    args = [ret_decay, z, z, z, z, rope_tabs[0], rope_tabs[1], state_ret]
    aliases = {}
    if prev_st is not None:
        in_specs.append(pl.BlockSpec(memory_space=pl.ANY))
        args.append(prev_st)
        aliases = {len(args) - 1: 1}
    return pl.pallas_call(
        functools.partial(_ret_kernel, t_ctx=t_ctx, t_lat=t_lat, n_ctx=n_ctx),
        grid=(npair, n_grp),
        in_specs=in_specs,
        out_specs=[pl.BlockSpec((GROUP_ROWS, LANES), lambda j, g: (g, j)), st_spec],
        out_shape=[jax.ShapeDtypeStruct((n_rows, D_B), BF16), st_shape],
        input_output_aliases=aliases,
        scratch_shapes=[pltpu.VMEM((hp, t_ctx, t_ctx), F32), pltpu.VMEM((hp, t_lat, t_lat), F32)],
        compiler_params=_params(("arbitrary", "arbitrary"), 56),
        name="retention",
    )(*args)


def _seg_scan(x, reverse):
    t = x.shape[0]
    row = lax.broadcasted_iota(jnp.int32, (t, 1), 0) % HGRN_CHUNK
    sft = 1
    while sft < HGRN_CHUNK:
        if reverse:
            x = x + jnp.where(row < HGRN_CHUNK - sft, pltpu.roll(x, t - sft, axis=0), 0.0)
        else:
            x = x + jnp.where(row >= sft, pltpu.roll(x, sft, axis=0), 0.0)
        sft *= 2
    return x


def _hgrn_intra(q, k, v, c, reverse):
    cs = HGRN_CHUNK
    sub = 8
    row = lax.broadcasted_iota(jnp.int32, (cs, 1), 0)
    parts = [jnp.zeros((sub, v.shape[1]), F32) for _ in range(cs // sub)]
    for s in range(cs):
        blk_s = s // sub
        blks = range(0, blk_s + 1) if reverse else range(blk_s, cs // sub)
        c_s, k_s, v_s = c[s:s + 1], k[s:s + 1], v[s:s + 1]
        for bt in blks:
            rs = slice(bt * sub, (bt + 1) * sub)
            w = q[rs] * k_s * jnp.exp(jnp.minimum(c[rs] - c_s, 0.0))
            col = jnp.sum(w, axis=-1, keepdims=True)
            if bt == blk_s:
                keep = (row[rs] <= s) if reverse else (row[rs] >= s)
                col = jnp.where(keep, col, 0.0)
            parts[bt] = parts[bt] + col * v_s
    return jnp.concatenate(parts, axis=0)


def _hgrn_states(qe_ref, ke_ref, v_ref, ee_ref, oi_ref, t, s0_ref=None, st_ref=None):
    sb = HGRN_SUPER
    n_sb = t // sb
    for r in range(GROUP_ROWS // t):
        incr = []
        for j in range(n_sb):
            rows = slice((r * n_sb + j) * sb, (r * n_sb + j + 1) * sb)
            keys = jnp.concatenate([ke_ref[0, rows, :], ke_ref[1, rows, :]], axis=1)
            incr.append(_dot_tn(keys, v_ref[rows, :]))
        for d in range(2):
            st = s0_ref[d] if s0_ref is not None else None
            order = range(n_sb) if d == 0 else reversed(range(n_sb))
            for j in order:
                blk = r * n_sb + j
                rows = slice(blk * sb, (blk + 1) * sb)
                upd = incr[j][d * DK_C:(d + 1) * DK_C]
                if st is None:
                    st = upd
                else:
                    oi_ref[rows, :] += _dot(qe_ref[d, rows, :], st.astype(BF16))
                    decay = jnp.broadcast_to(ee_ref[d, blk][0:1], (DK_C, DK_C)).T
                    st = st * decay + upd
            if st_ref is not None:
                st_ref[r, 0, d] = st
    if st_ref is not None:
        _zero_other_layers(st_ref)


def _hgrn_kernel(q_ref, i_ref, g_ref, ff_ref, fb_ref, lb_ref, gn_ref, s0_ref, *rest, t_ctx, t_lat, n_ctx):
    (o_ref, st_ref, qe_ref, ke_ref, ee_ref, oi_ref,
     sq_ref, sk_ref, sc_ref, mask_ref, unsafe_ref) = rest[-11:]
    cs = HGRN_CHUNK
    sb = HGRN_SUPER
    half = sb // 2
    nch = sb // cs

    @pl.when((pl.program_id(0) == 0) & (pl.program_id(1) == 0))
    def _():
        t = lax.broadcasted_iota(jnp.int32, (sb, half), 0)
        s = lax.broadcasted_iota(jnp.int32, (sb, half), 1) + ((t >> 7) << 7)
        same32 = (t >> 5) == (s >> 5)
        same64 = (t >> 6) == (s >> 6)
        for d, (incl, strict) in enumerate(((s <= t, s < t), (s >= t, s > t))):
            for i, m in enumerate((same32 & incl, same64 & jnp.logical_not(same32) & strict,
                                   jnp.logical_not(same64) & strict)):
                mask_ref[d, i] = jnp.where(m, 1.0, 0.0)

    def block_step(blk, carry):
        rows = pl.ds(pl.multiple_of(blk * sb, sb), sb)
        qs = _silu(q_ref[rows, :].astype(F32)) * (DK_C ** -0.5)
        v = i_ref[rows, :]
        per_dir = []
        c_min = None
        for d, f_ref in enumerate((ff_ref, fb_ref)):
            lb = lb_ref[d:d + 1, :]
            fr = f_ref[rows, :]
            e = jnp.exp(-jnp.abs(fr))
            r = 1.0 / (1.0 + e)
            sig_pos = jnp.where(fr >= 0, r, e * r)
            sig_neg = jnp.where(fr >= 0, e * r, r)
            f = lb + (1.0 - lb) * sig_pos
            k = (1.0 - lb) * sig_neg
            c = _seg_scan(jnp.log(jnp.maximum(f, F_MIN)), reverse=(d == 1))
            c3 = c.reshape(sb // cs, cs, DK_C)
            c_end = c3[:, cs - 1:cs, :] if d == 0 else c3[:, 0:1, :]
            tot = jnp.broadcast_to(c_end, c3.shape).reshape(sb, DK_C)
            per_dir.append((k, c, tot))
            m = jnp.min(c)
            c_min = m if c_min is None else jnp.minimum(c_min, m)

        safe = c_min >= -HGRN_SAFE_LOG
        diag = []
        quad = []
        for d, (k, c, tot) in enumerate(per_dir):
            q32 = qs * jnp.exp(c)
            k32 = k * jnp.exp(tot - c)
            k_hat = jnp.where(safe, k * jnp.exp(-c), 0.0).astype(BF16)
            e32 = [jnp.exp(tot[j * cs:j * cs + 1]) for j in range(nch)]

            def span(lo, hi):
                out = None
                for j in range(lo, hi):
                    out = e32[j] if out is None else out * e32[j]
                return out

            def rescaled(x, facs):
                parts = [x[j * cs:(j + 1) * cs] if f is None else x[j * cs:(j + 1) * cs] * f
                         for j, f in enumerate(facs)]
                return jnp.concatenate(parts, axis=0).astype(BF16)

            def level(n):
                before = [span((j // n) * n, j) for j in range(nch)]
                after = [span(j + 1, (j // n + 1) * n) for j in range(nch)]
                qf, kf = (before, after) if d == 0 else (after, before)
                return rescaled(q32, qf), rescaled(k32, kf)

            q32b, k32b = q32.astype(BF16), k32.astype(BF16)
            q64, k64 = level(2)
            q128, k128 = level(4)
            q256, k256 = level(8)
            diag.append((_dot_nt(q32b, k_hat), _dot_nt(q32b, k32b), _dot_nt(q64, k64)))
            if d == 0:
                quad.append(_dot_nt(q128[half:], k128[:half]))
            else:
                quad.append(_dot_nt(q128[:half], k128[half:]))
            qe_ref[d, rows, :] = q256
            ke_ref[d, rows, :] = k256
            ee_ref[d, blk] = jnp.broadcast_to(span(0, nch), (8, DK_C))

        pieces = []
        for j in range(nch):
            rs = slice(j * cs, (j + 1) * cs)
            own = j * cs // half
            cols = slice(own * half, (own + 1) * half)
            acc = None
            for d in range(2):
                for i in range(3):
                    term = diag[d][i][rs, cols] * mask_ref[d, i, rs, :]
                    acc = term if acc is None else acc + term
            other = quad[0][rs.start - half:rs.stop - half] if own == 1 else quad[1][rs]
            pieces.append(jnp.concatenate([other, acc] if own == 1 else [acc, other], axis=1))
        att = jnp.concatenate(pieces, axis=0)
        oi_ref[rows, :] = _dot(att.astype(BF16), v)

        unsafe_ref[blk] = jnp.logical_not(safe).astype(jnp.int32)
        sq_ref[rows, :] = qs
        for d in range(2):
            sk_ref[d, rows, :] = per_dir[d][0]
            sc_ref[d, rows, :] = per_dir[d][1]
        return carry

    n_blk = GROUP_ROWS // sb
    lax.fori_loop(0, n_blk, block_step, 0, unroll=True)

    def vpu_block(blk, carry):
        @pl.when(unsafe_ref[blk] != 0)
        def _():
            def chunk_step(i, carry2):
                crow = pl.ds(pl.multiple_of(blk * sb + i * cs, cs), cs)
                q, vv = sq_ref[crow, :], i_ref[crow, :].astype(F32)
                oi_ref[crow, :] += (_hgrn_intra(q, sk_ref[0, crow, :], vv, sc_ref[0, crow, :], reverse=False)
                                    + _hgrn_intra(q, sk_ref[1, crow, :], vv, sc_ref[1, crow, :], reverse=True))
                return carry2

            lax.fori_loop(0, sb // cs, chunk_step, 0)

        return carry

    lax.fori_loop(0, n_blk, vpu_block, 0)

    is_ctx = pl.program_id(1) < n_ctx
    pl.when(is_ctx)(functools.partial(
        _hgrn_states, qe_ref, ke_ref, i_ref, ee_ref, oi_ref, t_ctx, st_ref=st_ref))
    pl.when(jnp.logical_not(is_ctx))(functools.partial(
        _hgrn_states, qe_ref, ke_ref, i_ref, ee_ref, oi_ref, t_lat, s0_ref=s0_ref))

    o = oi_ref[...]
    on = o * lax.rsqrt(jnp.mean(o * o, axis=-1, keepdims=True) + EPS) * gn_ref[...]
    o_ref[...] = (on * _silu(g_ref[...].astype(F32))).astype(o_ref.dtype)


def _hgrn(z16, z32, lower, gnorm, state_hgrn, oi, n_layers, prev_st, n_ctx_req, t_ctx, t_lat, n_rows):
    nh = N_HEADS_C
    n_ctx = n_ctx_req * t_ctx // GROUP_ROWS
    n_grp = n_rows // GROUP_ROWS
    rows = GROUP_ROWS
    assert t_lat == GROUP_ROWS and t_ctx % HGRN_SUPER == 0
    blk = lambda p: pl.BlockSpec((rows, DK_C), lambda h, g, p=p: (g, p * nh + h))
    st_spec, st_shape = _state_spec(n_ctx_req, n_layers, oi, (2, nh, DK_C, DK_C),
                                    (2, None, DK_C, DK_C), lambda h: (0, h, 0, 0), n_ctx, GROUP_ROWS // t_ctx)
    in_specs = [blk(0), blk(1), blk(2), blk(0), blk(1),
                pl.BlockSpec((2, DK_C), lambda h, g: (0, h)),
                pl.BlockSpec((1, DK_C), lambda h, g: (0, 0)),
                pl.BlockSpec((None, None, 2, None, DK_C, DK_C),
                             lambda h, g: (jnp.maximum(g - n_ctx, 0), oi, 0, h, 0, 0))]
    args = [z16, z16, z16, z32, z32, lower, gnorm, state_hgrn]
    aliases = {}
    if prev_st is not None:
        in_specs.append(pl.BlockSpec(memory_space=pl.ANY))
        args.append(prev_st)
        aliases = {len(args) - 1: 1}
    return pl.pallas_call(
        functools.partial(_hgrn_kernel, t_ctx=t_ctx, t_lat=t_lat, n_ctx=n_ctx),
        grid=(nh, n_grp),
        in_specs=in_specs,
        out_specs=[pl.BlockSpec((rows, DK_C), lambda h, g: (g, h)), st_spec],
        out_shape=[jax.ShapeDtypeStruct((n_rows, D_MODEL), BF16), st_shape],
        input_output_aliases=aliases,
        scratch_shapes=[
            pltpu.VMEM((2, rows, DK_C), BF16),
            pltpu.VMEM((2, rows, DK_C), BF16),
            pltpu.VMEM((2, rows // HGRN_SUPER, 8, DK_C), F32),
            pltpu.VMEM((rows, DK_C), F32),
            pltpu.VMEM((rows, DK_C), F32),
            pltpu.VMEM((2, rows, DK_C), F32),
            pltpu.VMEM((2, rows, DK_C), F32),
            pltpu.VMEM((2, 3, HGRN_SUPER, HGRN_SUPER // 2), F32),
            pltpu.SMEM((rows // HGRN_SUPER,), jnp.int32),
        ],
        compiler_params=_params(("arbitrary", "arbitrary"), 32),
        name="hgrn",
    )(*args)


def kernel(x_prompt, x_sample, cache_kv, state_ret, state_hgrn, c, c_ctx, w_mod, b_mod, norm_g,
           w_in_even, w_out_even, rpb, ret_decay, w_in_odd, w_out_odd, hgrn_lb, hgrn_gnorm,
           w_ffn_in, w_ffn_out):
    bp, tp, _ = x_prompt.shape
    bs, ts, _ = x_sample.shape
    np_rows, ns_rows = bp * tp, bs * ts
    n_rows = np_rows + ns_rows
    n_even, n_odd = w_in_even.shape[0], w_in_odd.shape[0]
    assert np_rows % GROUP_ROWS == 0 and ts == GROUP_ROWS
    xs = [x_prompt.reshape(np_rows, D_MODEL), x_sample.reshape(ns_rows, D_MODEL)]

    n_c = bs + 1
    pad = (-n_c) % 8
    cvec = jnp.concatenate([c, c_ctx[None], jnp.zeros((pad, D_MODEL), F32)], axis=0)
    mod_all = _modulation(cvec, w_mod, b_mod)
    grp = np.concatenate([np.full(np_rows // GROUP_ROWS, bs), np.arange(bs)]).astype(np.int32)
    mod_all = mod_all[:, grp].reshape(DEPTH, len(grp), 6, D_MODEL)
    mod_all = jnp.pad(mod_all, ((0, 0), (0, 0), (0, MOD_ROWS - 6), (0, 0)))

    p_lb = jax.nn.softmax(hgrn_lb.astype(F32), axis=0)
    lower = jnp.clip(jnp.cumsum(p_lb, axis=0) - p_lb[0], 0.0, 1.0)
    pair_bias = _na_bias_tables(rpb)
    rope_tabs = _rope_tables(ts)

    w_in_even_b, w_out_even_b = w_in_even.astype(BF16), w_out_even.astype(BF16)
    w_out_odd_b = w_out_odd.astype(BF16)
    w_in_odd_q = w_in_odd[:, :, :D_C].astype(BF16)
    w_in_odd_f = w_in_odd[:, :, D_C:3 * D_C].astype(BF16)
    w_in_odd_ig = w_in_odd[:, :, 3 * D_C:].astype(BF16)
    ffn_w = [_ffn_weights(w_ffn_in[l], w_ffn_out[l]) for l in range(DEPTH)]

    kv_out = ret_out = hg_out = None
    y_split = None
    for l in range(DEPTH):
        mod = mod_all[l]
        g = norm_g[l].reshape(4, 1, D_MODEL)
        if l % 2 == 0:
            e = l // 2
            z, = _in_proj(xs, g[0], mod, [([w_in_even_b[e]], BF16)])
            oa, kv_out = _attention(z, cache_kv, pair_bias, e, n_even, kv_out, bp, tp, n_rows)
            ob, ret_out = _retention(z, ret_decay[e], rope_tabs, state_ret, e, n_even, ret_out, bp, tp, ts, n_rows)
            x = _out_proj([oa, ob], w_out_even_b[e], xs, g[1], mod)
        else:
            oi = l // 2
            z16, z32 = _in_proj(xs, g[0], mod, [([w_in_odd_q[oi], w_in_odd_ig[oi]], BF16),
                                                ([w_in_odd_f[oi]], F32)])
            gn = hgrn_gnorm[oi].reshape(1, DK_C)
            o, hg_out = _hgrn(z16, z32, lower[oi], gn, state_hgrn, oi, n_odd, hg_out, bp, tp, ts, n_rows)
            x = _out_proj([o], w_out_odd_b[oi], xs, g[1], mod)
        if l == DEPTH - 1:
            y_split = _ffn(x, g[2], g[3], mod, *ffn_w[l], split=(x_prompt.shape, x_sample.shape))
        else:
            xs = [_ffn(x, g[2], g[3], mod, *ffn_w[l])]

    return (y_split[0], y_split[1], kv_out, ret_out, hg_out)
```

```python
import functools

import numpy as np
import jax
import jax.numpy as jnp
from jax import lax
from jax.experimental import pallas as pl
from jax.experimental.pallas import tpu as pltpu

F32 = jnp.float32
BF16 = jnp.bfloat16

D_MODEL = 1024
DEPTH = 4
GRID_W = 64
HEAD_DIM = 64
N_HEADS_A = 8
N_HEADS_B = 8
D_A = N_HEADS_A * HEAD_DIM
D_B = N_HEADS_B * HEAD_DIM
WIN_R = 8
WIN_C = 16
N_HEADS_C = 8
DK_C = D_MODEL // N_HEADS_C
D_C = N_HEADS_C * DK_C
D_FF = ((8 * D_MODEL // 3 + 255) // 256) * 256
ATTN_SCALE = HEAD_DIM ** -0.5
HGRN_CHUNK = 32
HGRN_SUPER = 256
HGRN_SAFE_LOG = 75.0
ROPE_BASE = 10000.0
EPS = 1e-6
MASK_NEG = -1e30
F_MIN = 1e-30

GROUP_ROWS = 1024
MOD_ROWS = 8
LANES = 128
HEADS_PER_TILE = LANES // HEAD_DIM
PROJ_TN = 512
FFN_TF = 256
MOD_TN = 1536
MIB = 1024 * 1024

SH1, SC1, GT1, SH2, SC2, GT2 = range(6)

NA_QROWS = 4
NA_KROWS = 12
NA_KSTART = (0, 0, 4, 4)


def _params(sem, vmem_mib):
    return pltpu.CompilerParams(dimension_semantics=sem, vmem_limit_bytes=vmem_mib * MIB)


def _sigmoid(x):
    return 1.0 / (1.0 + jnp.exp(-x))


def _silu(x):
    return x * _sigmoid(x)


def _dot(a, b):
    return jnp.dot(a, b, preferred_element_type=F32)


def _dot_nt(a, b):
    return lax.dot_general(a, b, (((1,), (1,)), ((), ())), preferred_element_type=F32)


def _dot_tn(a, b):
    return lax.dot_general(a, b, (((0,), (0,)), ((), ())), preferred_element_type=F32)


def _mod_kernel(c_ref, w_ref, b_ref, o_ref):
    s = _silu(c_ref[...]).astype(BF16)
    o_ref[...] = _dot(s, w_ref[...].astype(BF16)) + b_ref[...]


def _modulation(cvec, w_mod, b_mod):
    rows = cvec.shape[0]
    n = w_mod.shape[-1]
    return pl.pallas_call(
        _mod_kernel,
        grid=(DEPTH, n // MOD_TN),
        in_specs=[
            pl.BlockSpec((rows, D_MODEL), lambda l, j: (0, 0)),
            pl.BlockSpec((None, D_MODEL, MOD_TN), lambda l, j: (l, 0, j)),
            pl.BlockSpec((None, 1, MOD_TN), lambda l, j: (l, 0, j)),
        ],
        out_specs=pl.BlockSpec((None, rows, MOD_TN), lambda l, j: (l, 0, j)),
        out_shape=jax.ShapeDtypeStruct((DEPTH, rows, n), F32),
        compiler_params=_params(("arbitrary", "arbitrary"), 32),
        name="modulation",
    )(cvec, w_mod, b_mod.reshape(DEPTH, 1, n))


def _norm_mod(x, g, mod, sh_row, sc_row):
    y = x * lax.rsqrt(jnp.mean(x * x, axis=-1, keepdims=True) + EPS) * g
    return y * (1.0 + mod[sc_row:sc_row + 1]) + mod[sh_row:sh_row + 1]


def _stream_specs(xs):
    tm = GROUP_ROWS
    if len(xs) == 1:
        return [pl.BlockSpec((tm, D_MODEL), lambda i: (i, 0))]
    n_ctx = xs[0].shape[0] // tm
    return [pl.BlockSpec((tm, D_MODEL), lambda i: (jnp.minimum(i, n_ctx - 1), 0)),
            pl.BlockSpec((tm, D_MODEL), lambda i: (jnp.maximum(i - n_ctx, 0), 0))]


def _stream_tile(x_refs, n_ctx):
    if len(x_refs) == 1:
        return x_refs[0][...]
    return jnp.where(pl.program_id(0) < n_ctx, x_refs[0][...], x_refs[1][...])


def _in_proj_kernel(*refs, n_x, n_ctx, n_w):
    x_refs, (g_ref, mod_ref) = refs[:n_x], refs[n_x:n_x + 2]
    w_refs, o_refs = refs[n_x + 2:n_x + 2 + sum(n_w)], refs[n_x + 2 + sum(n_w):]
    h = _norm_mod(_stream_tile(x_refs, n_ctx), g_ref[...], mod_ref[...], SH1, SC1).astype(BF16)
    w_iter = iter(w_refs)
    for o_ref, count in zip(o_refs, n_w):
        col0 = 0
        for w_ref in (next(w_iter) for _ in range(count)):
            for j in range(w_ref.shape[1] // PROJ_TN):
                cols = slice(j * PROJ_TN, (j + 1) * PROJ_TN)
                o_ref[:, col0 + j * PROJ_TN:col0 + (j + 1) * PROJ_TN] = _dot(h, w_ref[:, cols]).astype(o_ref.dtype)
            col0 += w_ref.shape[1]


def _resident(block_shape):
    return pl.BlockSpec(block_shape, lambda *_: (0,) * len(block_shape), pipeline_mode=pl.Buffered(1))


def _in_proj(xs, g, mod, outputs):
    m = sum(x.shape[0] for x in xs)
    tm = GROUP_ROWS
    ws = [w for pieces, _ in outputs for w in pieces]
    ns = [sum(w.shape[1] for w in pieces) for pieces, _ in outputs]
    assert all(w.shape[1] % PROJ_TN == 0 for w in ws)
    return pl.pallas_call(
        functools.partial(_in_proj_kernel, n_x=len(xs), n_ctx=xs[0].shape[0] // tm,
                          n_w=tuple(len(pieces) for pieces, _ in outputs)),
        grid=(m // tm,),
        in_specs=_stream_specs(xs) + [
            pl.BlockSpec((1, D_MODEL), lambda i: (0, 0)),
            pl.BlockSpec((None, MOD_ROWS, D_MODEL), lambda i: (i, 0, 0)),
        ] + [_resident(w.shape) for w in ws],
        out_specs=[pl.BlockSpec((tm, n), lambda i: (i, 0)) for n in ns],
        out_shape=[jax.ShapeDtypeStruct((m, n), dt) for n, (_, dt) in zip(ns, outputs)],
        compiler_params=_params(("arbitrary",), 56),
        name="in_proj",
    )(*xs, g, mod, *ws)


def _out_proj_kernel(*refs, n_in, n_x, n_ctx):
    a_refs, w_refs = refs[:n_in], refs[n_in:2 * n_in]
    x_refs = refs[2 * n_in:2 * n_in + n_x]
    g_ref, mod_ref, o_ref = refs[2 * n_in + n_x:]
    y = _dot(a_refs[0][...], w_refs[0][...])
    for a_ref, w_ref in zip(a_refs[1:], w_refs[1:]):
        y = y + _dot(a_ref[...], w_ref[...])
    yn = y * lax.rsqrt(jnp.mean(y * y, axis=-1, keepdims=True) + EPS) * g_ref[...]
    o_ref[...] = _stream_tile(x_refs, n_ctx) + mod_ref[GT1:GT1 + 1, :] * yn


def _out_proj(acts, w, xs, g, mod):
    m = sum(x.shape[0] for x in xs)
    tm = GROUP_ROWS
    n_in = len(acts)
    ks = [a.shape[1] for a in acts]
    assert sum(ks) == w.shape[0] and len(set(ks)) == 1
    in_specs = [pl.BlockSpec((tm, k), lambda i: (i, 0)) for k in ks]
    in_specs += [pl.BlockSpec((ks[0], D_MODEL), lambda i, p=p: (p, 0)) for p in range(n_in)]
    in_specs += _stream_specs(xs) + [
        pl.BlockSpec((1, D_MODEL), lambda i: (0, 0)),
        pl.BlockSpec((None, MOD_ROWS, D_MODEL), lambda i: (i, 0, 0)),
    ]
    return pl.pallas_call(
        functools.partial(_out_proj_kernel, n_in=n_in, n_x=len(xs), n_ctx=xs[0].shape[0] // tm),
        grid=(m // tm,),
        in_specs=in_specs,
        out_specs=pl.BlockSpec((tm, D_MODEL), lambda i: (i, 0)),
        out_shape=jax.ShapeDtypeStruct((m, D_MODEL), F32),
        compiler_params=_params(("arbitrary",), 48),
        name="out_proj",
    )(*acts, *([w] * n_in), *xs, g, mod)


def _ffn_kernel(x_ref, xn_ref, g2_ref, g3_ref, mod_ref, modn_ref, win_ref, wo_ref, *rest, n_ctx_groups):
    o_refs, (h_ref, acc_ref) = rest[:-2], rest[-2:]
    tm = x_ref.shape[0]
    nf = wo_ref.shape[0]
    ahead = -(-tm // (nf * 16)) * 16
    i = pl.program_id(0)
    cur, nxt = i % 2, (i + 1) % 2

    @pl.when(i == 0)
    def _():
        h_ref[0] = _norm_mod(x_ref[...], g2_ref[...], mod_ref[...], SH2, SC2).astype(BF16)

    acc_ref[...] = jnp.zeros_like(acc_ref)

    def hidden_step(f, carry):
        h = h_ref[cur]
        a = _dot(h, win_ref[:, pl.ds(pl.multiple_of(f * FFN_TF, FFN_TF), FFN_TF)])
        u = _dot(h, win_ref[:, pl.ds(pl.multiple_of(D_FF + f * FFN_TF, FFN_TF), FFN_TF)])
        acc_ref[...] += _dot((_silu(a) * u).astype(BF16), wo_ref[f])
        r0 = pl.multiple_of(jnp.minimum(f * ahead, tm - ahead), 16)
        rows = pl.ds(r0, ahead)
        h_ref[nxt, rows, :] = _norm_mod(xn_ref[rows, :], g2_ref[...], modn_ref[...], SH2, SC2).astype(BF16)
        return carry

    lax.fori_loop(0, nf, hidden_step, 0)

    def finish(o_ref):
        y = acc_ref[...]
        yn = y * lax.rsqrt(jnp.mean(y * y, axis=-1, keepdims=True) + EPS) * g3_ref[...]
        o_ref[...] = (x_ref[...] + mod_ref[GT2:GT2 + 1, :] * yn).reshape(o_ref.shape)

    if len(o_refs) == 1:
        finish(o_refs[0])
    else:
        is_ctx = pl.program_id(0) < n_ctx_groups
        pl.when(is_ctx)(functools.partial(finish, o_refs[0]))
        pl.when(jnp.logical_not(is_ctx))(functools.partial(finish, o_refs[1]))


def _ffn_weights(w_in, w_out):
    return w_in.astype(BF16), w_out.astype(BF16).reshape(D_FF // FFN_TF, FFN_TF, D_MODEL)


def _ffn(x, g2, g3, mod, w_in, wo, split=None):
    m = x.shape[0]
    tm = GROUP_ROWS
    n_ctx = 0
    if split is None:
        out_specs = pl.BlockSpec((tm, D_MODEL), lambda i: (i, 0))
        out_shape = jax.ShapeDtypeStruct((m, D_MODEL), F32)
    else:
        (bp, tp, _), (bs, ts, _) = split
        n_ctx = bp * tp // tm
        assert ts == tm and (bp * tp) % tm == 0
        out_specs = [
            pl.BlockSpec((tm // tp, tp, D_MODEL), lambda i: (jnp.minimum(i, n_ctx - 1), 0, 0)),
            pl.BlockSpec((1, ts, D_MODEL), lambda i: (jnp.maximum(i - n_ctx, 0), 0, 0)),
        ]
        out_shape = [jax.ShapeDtypeStruct(s, F32) for s in split]
    last = m // tm - 1
    return pl.pallas_call(
        functools.partial(_ffn_kernel, n_ctx_groups=n_ctx),
        grid=(m // tm,),
        in_specs=[
            pl.BlockSpec((tm, D_MODEL), lambda i: (i, 0)),
            pl.BlockSpec((tm, D_MODEL), lambda i: (jnp.minimum(i + 1, last), 0)),
            pl.BlockSpec((1, D_MODEL), lambda i: (0, 0)),
            pl.BlockSpec((1, D_MODEL), lambda i: (0, 0)),
            pl.BlockSpec((None, MOD_ROWS, D_MODEL), lambda i: (i, 0, 0)),
            pl.BlockSpec((None, MOD_ROWS, D_MODEL), lambda i: (jnp.minimum(i + 1, last), 0, 0)),
            _resident(w_in.shape), _resident(wo.shape),
        ],
        out_specs=out_specs,
        out_shape=out_shape,
        scratch_shapes=[pltpu.VMEM((2, tm, D_MODEL), BF16), pltpu.VMEM((tm, D_MODEL), F32)],
        compiler_params=_params(("arbitrary",), 60),
        name="ffn",
    )(x, x, g2, g3, mod, mod, w_in, wo)


def _state_spec(n_req, n_layers, layer, tail, tail_block, tail_index, n_ctx, nb):
    n_own = n_layers if layer == 0 else 1
    first = 0 if layer == 0 else layer

    def index(j, g):
        return (jnp.minimum(g, n_ctx - 1), first) + tail_index(j)

    spec = pl.BlockSpec((nb, n_own) + tail_block, index)
    shape = jax.ShapeDtypeStruct((n_req, n_layers) + tail, F32)
    return spec, shape


def _zero_other_layers(st_ref):
    if st_ref.shape[1] > 1:
        st_ref[:, 1:] = jnp.zeros((st_ref.shape[0], st_ref.shape[1] - 1) + st_ref.shape[2:], st_ref.dtype)


NA_NRO = 2 * WIN_R - 1


def _na_bias_tables(rpb):
    qc = np.arange(GRID_W)[:, None]
    kc = np.arange(GRID_W)[None, :]
    win0 = np.clip(qc - WIN_C // 2, 0, GRID_W - WIN_C)
    col_valid = (kc >= win0) & (kc < win0 + WIN_C)
    col_off = np.clip(kc - qc, 1 - WIN_C, WIN_C - 1) + (WIN_C - 1)
    onehot = (col_off[None] == np.arange(2 * WIN_C - 1)[:, None, None]).astype(np.float32)
    tiles = jnp.einsum('ehrc,cqk->ehrqk', rpb.astype(F32), jnp.asarray(onehot), precision=lax.Precision.HIGHEST)
    tiles = jnp.where(col_valid[None, None, None], tiles, MASK_NEG)
    nxt = jnp.concatenate([tiles[:, :, 1:], tiles[:, :, -1:]], axis=2)
    return jnp.concatenate([tiles, nxt], axis=-1)


def _na_group_bias(pair_ref, hh, g):
    rows = GROUP_ROWS // GRID_W
    first_half = lax.broadcasted_iota(jnp.int32, (GRID_W, 2 * GRID_W), 1) < GRID_W
    bias = []
    for a in range(NA_QROWS):
        r = g * NA_QROWS + a
        row0 = min(max(r - WIN_R // 2, 0), rows - WIN_R)
        pieces = []
        for w in range(0, NA_KROWS, 2):
            kr = NA_KSTART[g] + w
            ro = kr - r + (WIN_R - 1)
            in0 = row0 <= kr < row0 + WIN_R
            in1 = row0 <= kr + 1 < row0 + WIN_R
            assert not (in0 or in1) or 0 <= ro < NA_NRO
            if in0 and in1:
                piece = pair_ref[hh, ro]
            elif in0:
                piece = jnp.where(first_half, pair_ref[hh, ro], MASK_NEG)
            elif in1:
                piece = jnp.where(first_half, MASK_NEG, pair_ref[hh, ro])
            else:
                piece = jnp.full((GRID_W, 2 * GRID_W), MASK_NEG, F32)
            pieces.append(piece)
        bias.append(jnp.concatenate(pieces, axis=1))
    return jnp.concatenate(bias, axis=0)


def _ctx_attend(q_ref, k_ref, v_ref, o_ref, kv_ref, t):
    for r in range(GROUP_ROWS // t):
        rs = slice(r * t, (r + 1) * t)
        q2, k2, v2 = q_ref[rs, :] * ATTN_SCALE, k_ref[rs, :], v_ref[rs, :]
        head_of_lane = lax.broadcasted_iota(jnp.int32, q2.shape, 1) // HEAD_DIM
        out = None
        for hh in range(HEADS_PER_TILE):
            hs = slice(hh * HEAD_DIM, (hh + 1) * HEAD_DIM)
            kv_ref[r, 0, 0, hh] = k2[:, hs].astype(F32)
            kv_ref[r, 0, 1, hh] = v2[:, hs].astype(F32)
            in_head = head_of_lane == hh
            s = _dot_nt(jnp.where(in_head, q2, jnp.zeros_like(q2)), k2)
            p = jnp.exp(s - jnp.max(s, axis=-1, keepdims=True))
            l = jnp.sum(p, axis=-1, keepdims=True)
            o = _dot(p.astype(BF16), v2) / l
            out = o if out is None else jnp.where(in_head, o, out)
        o_ref[rs, :] = out.astype(o_ref.dtype)
    _zero_other_layers(kv_ref)


def _na_attend(q_ref, k_ref, v_ref, ckv_ref, pair_ref, o_ref):
    q2, k2, v2 = q_ref[...] * ATTN_SCALE, k_ref[...], v_ref[...]
    tq = NA_QROWS * GRID_W
    tk = NA_KROWS * GRID_W
    head_of_lane = lax.broadcasted_iota(jnp.int32, q2.shape, 1) // HEAD_DIM
    kc2 = jnp.concatenate([ckv_ref[0, hh] for hh in range(HEADS_PER_TILE)], axis=1).astype(BF16)
    vc2 = jnp.concatenate([ckv_ref[1, hh] for hh in range(HEADS_PER_TILE)], axis=1).astype(BF16)
    out = None
    for hh in range(HEADS_PER_TILE):
        in_head = head_of_lane == hh
        q = jnp.where(in_head, q2, jnp.zeros_like(q2))
        rows = []
        for g in range(len(NA_KSTART)):
            qg = q[g * tq:(g + 1) * tq]
            k0 = NA_KSTART[g] * GRID_W
            s_loc = _dot_nt(qg, k2[k0:k0 + tk]) + _na_group_bias(pair_ref, hh, g)
            s_ctx = _dot_nt(qg, kc2)
            m = jnp.maximum(jnp.max(s_loc, axis=-1, keepdims=True), jnp.max(s_ctx, axis=-1, keepdims=True))
            p_loc = jnp.exp(s_loc - m)
            p_ctx = jnp.exp(s_ctx - m)
            l = jnp.sum(p_loc, axis=-1, keepdims=True) + jnp.sum(p_ctx, axis=-1, keepdims=True)
            o = _dot(p_loc.astype(BF16), v2[k0:k0 + tk]) + _dot(p_ctx.astype(BF16), vc2)
            rows.append(o / l)
        o_h = jnp.concatenate(rows, axis=0)
        out = o_h if out is None else jnp.where(in_head, o_h, out)
    o_ref[...] = out.astype(o_ref.dtype)


def _attn_kernel(q_ref, k_ref, v_ref, ckv_ref, pair_ref, *rest, t_ctx, n_ctx):
    o_ref, kv_ref = rest[-2:]
    is_ctx = pl.program_id(1) < n_ctx
    pl.when(is_ctx)(functools.partial(_ctx_attend, q_ref, k_ref, v_ref, o_ref, kv_ref, t_ctx))
    pl.when(jnp.logical_not(is_ctx))(
        functools.partial(_na_attend, q_ref, k_ref, v_ref, ckv_ref, pair_ref, o_ref))


def _attention(z, cache_kv, pair_bias, e, n_layers, prev_kv, n_ctx_req, t_ctx, n_rows):
    npair = D_A // LANES
    hp = HEADS_PER_TILE
    n_ctx = n_ctx_req * t_ctx // GROUP_ROWS
    n_grp = n_rows // GROUP_ROWS
    past = cache_kv.shape[-2]
    blk = lambda p: pl.BlockSpec((GROUP_ROWS, LANES), lambda j, g, p=p: (g, p * npair + j))
    kv_spec, kv_shape = _state_spec(n_ctx_req, n_layers, e, (2, N_HEADS_A, t_ctx, HEAD_DIM),
                                    (2, hp, t_ctx, HEAD_DIM), lambda j: (0, j, 0, 0), n_ctx, GROUP_ROWS // t_ctx)
    in_specs = [
        blk(0), blk(1), blk(2),
        pl.BlockSpec((None, None, 2, hp, past, HEAD_DIM), lambda j, g: (jnp.maximum(g - n_ctx, 0), e, 0, j, 0, 0)),
        pl.BlockSpec((None, hp) + pair_bias.shape[2:], lambda j, g: (e, j, 0, 0, 0)),
    ]
    args = [z, z, z, cache_kv, pair_bias]
    aliases = {}
    if prev_kv is not None:
        in_specs.append(pl.BlockSpec(memory_space=pl.ANY))
        args.append(prev_kv)
        aliases = {len(args) - 1: 1}
    return pl.pallas_call(
        functools.partial(_attn_kernel, t_ctx=t_ctx, n_ctx=n_ctx),
        grid=(npair, n_grp),
        in_specs=in_specs,
        out_specs=[pl.BlockSpec((GROUP_ROWS, LANES), lambda j, g: (g, j)), kv_spec],
        out_shape=[jax.ShapeDtypeStruct((n_rows, D_A), BF16), kv_shape],
        input_output_aliases=aliases,
        compiler_params=_params(("arbitrary", "arbitrary"), 56),
        name="attention",
    )(*args)


def _rope_tables(t):
    half = HEAD_DIM // 2
    nf = half // 2
    inv = ROPE_BASE ** (-np.arange(nf, dtype=np.float32) / nf)
    pos = np.arange(t)
    ang_r = (pos // GRID_W).astype(np.float32)[:, None] * inv[None, :]
    ang_c = (pos % GRID_W).astype(np.float32)[:, None] * inv[None, :]
    ang_r, ang_c = jnp.asarray(ang_r), jnp.asarray(ang_c)
    cr, sr, cc, sc = jnp.cos(ang_r), jnp.sin(ang_r), jnp.cos(ang_c), jnp.sin(ang_c)
    cos = jnp.concatenate([cr, cr, cc, cc], axis=-1)
    sin = jnp.concatenate([-sr, sr, -sc, sc], axis=-1)
    return jnp.tile(cos, (1, HEADS_PER_TILE)), jnp.tile(sin, (1, HEADS_PER_TILE))


def _rope(x, cos, sin):
    nf = HEAD_DIM // 4
    lane = lax.broadcasted_iota(jnp.int32, x.shape, 1)
    partner = jnp.where(lane % (2 * nf) < nf, pltpu.roll(x, LANES - nf, axis=1), pltpu.roll(x, nf, axis=1))
    return x * cos + partner * sin


def _decay_matrix(dm_ref, lg, t):
    n_i = lax.broadcasted_iota(jnp.int32, (t, t), 0)
    m_i = lax.broadcasted_iota(jnp.int32, (t, t), 1)
    diff = (n_i - m_i).astype(F32)
    dist = jnp.abs(diff)
    for hh in range(HEADS_PER_TILE):
        one_sided = jnp.exp(jnp.where(diff >= 0, lg[hh][0], lg[hh][1]) * dist)
        dm_ref[hh] = jnp.where(diff == 0, 2.0, one_sided)


def _head_rms_gate(o, g, head_of_lane):
    sq = o * o
    ms = sum(jnp.where(head_of_lane == hh,
                       jnp.sum(jnp.where(head_of_lane == hh, sq, 0.0), axis=-1, keepdims=True), 0.0)
             for hh in range(HEADS_PER_TILE)) * (1.0 / HEAD_DIM)
    return o * lax.rsqrt(ms + EPS) * _silu(g)


def _two_sided_scores(q, kb, vb, dm_ref, head_of_lane):
    o = None
    for hh in range(HEADS_PER_TILE):
        in_head = head_of_lane == hh
        att = _dot_nt(jnp.where(in_head, q, 0.0).astype(BF16), kb) * dm_ref[hh]
        o_h = _dot(att.astype(BF16), vb)
        o = o_h if o is None else jnp.where(in_head, o_h, o)
    return o


def _retain(q_ref, k_ref, v_ref, g_ref, o_ref, st_ref, dm_ref, lg, t):
    pos = lax.broadcasted_iota(jnp.int32, (t, 1), 0).astype(F32)
    head_of_lane = lax.broadcasted_iota(jnp.int32, (1, LANES), 1) // HEAD_DIM
    lgv = [sum(jnp.where(head_of_lane == hh, lg[hh][d], 0.0) for hh in range(HEADS_PER_TILE)) for d in range(2)]
    for r in range(GROUP_ROWS // t):
        rs = slice(r * t, (r + 1) * t)
        q2 = q_ref[rs, :].astype(F32)
        k2 = k_ref[rs, :].astype(F32) * (HEAD_DIM ** -0.5)
        v2 = v_ref[rs, :]
        st = (_dot_tn((k2 * jnp.exp(lgv[0] * (t - 1.0 - pos))).astype(BF16), v2),
              _dot_tn((k2 * jnp.exp(lgv[1] * pos)).astype(BF16), v2))
        for hh in range(HEADS_PER_TILE):
            hs = slice(hh * HEAD_DIM, (hh + 1) * HEAD_DIM)
            st_ref[r, 0, 0, hh] = st[0][hs, hs]
            st_ref[r, 0, 1, hh] = st[1][hs, hs]
        o = _two_sided_scores(q2, k2.astype(BF16), v2, dm_ref, head_of_lane)
        o_ref[rs, :] = _head_rms_gate(o, g_ref[rs, :].astype(F32), head_of_lane).astype(o_ref.dtype)
    _zero_other_layers(st_ref)


def _retain_chunked(q_ref, k_ref, v_ref, g_ref, o_ref, dm_ref, lg, t, c, rope, s0_ref):
    n_c = t // c
    pos = lax.broadcasted_iota(jnp.int32, (c, 1), 0).astype(F32)
    head_of_lane = lax.broadcasted_iota(jnp.int32, (1, LANES), 1) // HEAD_DIM
    head_of_row = lax.broadcasted_iota(jnp.int32, (LANES, 1), 0) // HEAD_DIM
    same_head = head_of_row == head_of_lane
    lgv = [sum(jnp.where(head_of_lane == hh, lg[hh][d], 0.0) for hh in range(HEADS_PER_TILE)) for d in range(2)]
    lgc = [sum(jnp.where(head_of_row == hh, lg[hh][d], 0.0) for hh in range(HEADS_PER_TILE)) for d in range(2)]
    chunk_decay = [jnp.exp(lgc[d] * float(c)) for d in range(2)]
    q_dec = (jnp.exp(lgv[0] * (pos + 1.0)), jnp.exp(lgv[1] * (c - pos)))
    k_dec = (jnp.exp(lgv[0] * (c - 1.0 - pos)), jnp.exp(lgv[1] * pos))

    q2 = _rope(q_ref[...].astype(F32), rope[0][...], rope[1][...])
    k2 = _rope(k_ref[...].astype(F32) * (HEAD_DIM ** -0.5), rope[0][...], rope[1][...])
    cs = [slice(i * c, (i + 1) * c) for i in range(n_c)]

    def initial(d):
        zero = jnp.zeros((HEAD_DIM, HEAD_DIM), F32)
        rows = [jnp.concatenate([s0_ref[d, hh] if col == hh else zero for col in range(HEADS_PER_TILE)], axis=1)
                for hh in range(HEADS_PER_TILE)]
        return jnp.concatenate(rows, axis=0)

    incr = [[jnp.where(same_head, _dot_tn((k2[s] * k_dec[d]).astype(BF16), v_ref[s, :]), 0.0) for s in cs]
            for d in range(2)]
    fwd = [initial(0)]
    for i in range(n_c - 1):
        fwd.append(fwd[i] * chunk_decay[0] + incr[0][i])
    bwd = [initial(1)]
    for i in reversed(range(1, n_c)):
        bwd.insert(0, bwd[0] * chunk_decay[1] + incr[1][i])

    for i, s in enumerate(cs):
        qc = q2[s]
        o = _two_sided_scores(qc, k2[s].astype(BF16), v_ref[s, :], dm_ref, head_of_lane)
        o = o + _dot((qc * q_dec[0]).astype(BF16), fwd[i].astype(BF16))
        o = o + _dot((qc * q_dec[1]).astype(BF16), bwd[i].astype(BF16))
        o_ref[s, :] = _head_rms_gate(o, g_ref[s, :].astype(F32), head_of_lane).astype(o_ref.dtype)


def _ret_kernel(dec_ref, q_ref, k_ref, v_ref, g_ref, cos_ref, sin_ref, s0_ref, *rest, t_ctx, t_lat, n_ctx):
    o_ref, st_ref, dm_ref = rest[-3:]
    j, g = pl.program_id(0), pl.program_id(1)
    lg = [[-jnp.exp(jnp.full((1, 1), dec_ref[d, j * HEADS_PER_TILE + hh], F32)) for d in range(2)]
          for hh in range(HEADS_PER_TILE)]
    pl.when(g == 0)(functools.partial(_decay_matrix, dm_ref, lg, t_ctx))
    is_ctx = g < n_ctx
    pl.when(is_ctx)(functools.partial(_retain, q_ref, k_ref, v_ref, g_ref, o_ref, st_ref, dm_ref, lg, t_ctx))
    pl.when(jnp.logical_not(is_ctx))(functools.partial(
        _retain_chunked, q_ref, k_ref, v_ref, g_ref, o_ref, dm_ref, lg, t_lat, t_ctx, (cos_ref, sin_ref), s0_ref))


def _retention(z, ret_decay, rope_tabs, state_ret, e, n_layers, prev_st, n_ctx_req, t_ctx, t_lat, n_rows):
    npair = D_B // LANES
    hp = HEADS_PER_TILE
    sec0 = 3 * D_A // LANES
    n_ctx = n_ctx_req * t_ctx // GROUP_ROWS
    n_grp = n_rows // GROUP_ROWS
    assert t_lat == GROUP_ROWS
    blk = lambda p: pl.BlockSpec((GROUP_ROWS, LANES), lambda j, g, p=p: (g, sec0 + p * npair + j))
    st_spec, st_shape = _state_spec(n_ctx_req, n_layers, e, (2, N_HEADS_B, HEAD_DIM, HEAD_DIM),
                                    (2, hp, HEAD_DIM, HEAD_DIM), lambda j: (0, j, 0, 0), n_ctx,
                                    GROUP_ROWS // t_ctx)
    in_specs = [
        pl.BlockSpec(memory_space=pltpu.SMEM), blk(0), blk(1), blk(2), blk(3),
        pl.BlockSpec((t_lat, LANES), lambda j, g: (0, 0)),
        pl.BlockSpec((t_lat, LANES), lambda j, g: (0, 0)),
        pl.BlockSpec((None, None, 2, hp, HEAD_DIM, HEAD_DIM),
                     lambda j, g: (jnp.maximum(g - n_ctx, 0), e, 0, j, 0, 0)),
    ]
    args = [ret_decay, z, z, z, z, rope_tabs[0], rope_tabs[1], state_ret]
    aliases = {}
    if prev_st is not None:
        in_specs.append(pl.BlockSpec(memory_space=pl.ANY))
        args.append(prev_st)
        aliases = {len(args) - 1: 1}
    return pl.pallas_call(
        functools.partial(_ret_kernel, t_ctx=t_ctx, t_lat=t_lat, n_ctx=n_ctx),
        grid=(npair, n_grp),
        in_specs=in_specs,
        out_specs=[pl.BlockSpec((GROUP_ROWS, LANES), lambda j, g: (g, j)), st_spec],
        out_shape=[jax.ShapeDtypeStruct((n_rows, D_B), BF16), st_shape],
        input_output_aliases=aliases,
        scratch_shapes=[pltpu.VMEM((hp, t_ctx, t_ctx), F32)],
        compiler_params=_params(("arbitrary", "arbitrary"), 56),
        name="retention",
    )(*args)


def _seg_scan(x, reverse):
    t = x.shape[0]
    row = lax.broadcasted_iota(jnp.int32, (t, 1), 0) % HGRN_CHUNK
    sft = 1
    while sft < HGRN_CHUNK:
        if reverse:
            x = x + jnp.where(row < HGRN_CHUNK - sft, pltpu.roll(x, t - sft, axis=0), 0.0)
        else:
            x = x + jnp.where(row >= sft, pltpu.roll(x, sft, axis=0), 0.0)
        sft *= 2
    return x


def _hgrn_intra(q, k, v, c, reverse):
    cs = HGRN_CHUNK
    sub = 8
    row = lax.broadcasted_iota(jnp.int32, (cs, 1), 0)
    parts = [jnp.zeros((sub, v.shape[1]), F32) for _ in range(cs // sub)]
    for s in range(cs):
        blk_s = s // sub
        blks = range(0, blk_s + 1) if reverse else range(blk_s, cs // sub)
        c_s, k_s, v_s = c[s:s + 1], k[s:s + 1], v[s:s + 1]
        for bt in blks:
            rs = slice(bt * sub, (bt + 1) * sub)
            w = q[rs] * k_s * jnp.exp(jnp.minimum(c[rs] - c_s, 0.0))
            col = jnp.sum(w, axis=-1, keepdims=True)
            if bt == blk_s:
                keep = (row[rs] <= s) if reverse else (row[rs] >= s)
                col = jnp.where(keep, col, 0.0)
            parts[bt] = parts[bt] + col * v_s
    return jnp.concatenate(parts, axis=0)


def _hgrn_states(qe_ref, ke_ref, v_ref, ee_ref, oi_ref, t, s0_ref=None, st_ref=None):
    sb = HGRN_SUPER
    n_sb = t // sb
    for r in range(GROUP_ROWS // t):
        incr = []
        for j in range(n_sb):
            rows = slice((r * n_sb + j) * sb, (r * n_sb + j + 1) * sb)
            keys = jnp.concatenate([ke_ref[0, rows, :], ke_ref[1, rows, :]], axis=1)
            incr.append(_dot_tn(keys, v_ref[rows, :]))
        for d in range(2):
            st = s0_ref[d] if s0_ref is not None else None
            order = range(n_sb) if d == 0 else reversed(range(n_sb))
            for j in order:
                blk = r * n_sb + j
                rows = slice(blk * sb, (blk + 1) * sb)
                upd = incr[j][d * DK_C:(d + 1) * DK_C]
                if st is None:
                    st = upd
                else:
                    oi_ref[rows, :] += _dot(qe_ref[d, rows, :], st.astype(BF16))
                    decay = jnp.broadcast_to(ee_ref[d, blk][0:1], (DK_C, DK_C)).T
                    st = st * decay + upd
            if st_ref is not None:
                st_ref[r, 0, d] = st
    if st_ref is not None:
        _zero_other_layers(st_ref)


def _hgrn_kernel(q_ref, i_ref, g_ref, ff_ref, fb_ref, lb_ref, gn_ref, s0_ref, *rest, t_ctx, t_lat, n_ctx):
    (o_ref, st_ref, qe_ref, ke_ref, ee_ref, oi_ref,
     sq_ref, sk_ref, sc_ref, mask_ref, unsafe_ref) = rest[-11:]
    cs = HGRN_CHUNK
    sb = HGRN_SUPER
    half = sb // 2
    nch = sb // cs

    @pl.when((pl.program_id(0) == 0) & (pl.program_id(1) == 0))
    def _():
        t = lax.broadcasted_iota(jnp.int32, (sb, half), 0)
        s = lax.broadcasted_iota(jnp.int32, (sb, half), 1) + ((t >> 7) << 7)
        same32 = (t >> 5) == (s >> 5)
        same64 = (t >> 6) == (s >> 6)
        for d, (incl, strict) in enumerate(((s <= t, s < t), (s >= t, s > t))):
            for i, m in enumerate((same32 & incl, same64 & jnp.logical_not(same32) & strict,
                                   jnp.logical_not(same64) & strict)):
                mask_ref[d, i] = jnp.where(m, 1.0, 0.0)

    def block_step(blk, carry):
        rows = pl.ds(pl.multiple_of(blk * sb, sb), sb)
        qs = _silu(q_ref[rows, :].astype(F32)) * (DK_C ** -0.5)
        v = i_ref[rows, :]
        per_dir = []
        c_min = None
        for d, f_ref in enumerate((ff_ref, fb_ref)):
            lb = lb_ref[d:d + 1, :]
            fr = f_ref[rows, :]
            e = jnp.exp(-jnp.abs(fr))
            r = 1.0 / (1.0 + e)
            sig_pos = jnp.where(fr >= 0, r, e * r)
            sig_neg = jnp.where(fr >= 0, e * r, r)
            f = lb + (1.0 - lb) * sig_pos
            k = (1.0 - lb) * sig_neg
            c = _seg_scan(jnp.log(jnp.maximum(f, F_MIN)), reverse=(d == 1))
            c3 = c.reshape(sb // cs, cs, DK_C)
            c_end = c3[:, cs - 1:cs, :] if d == 0 else c3[:, 0:1, :]
            tot = jnp.broadcast_to(c_end, c3.shape).reshape(sb, DK_C)
            per_dir.append((k, c, tot))
            m = jnp.min(c)
            c_min = m if c_min is None else jnp.minimum(c_min, m)

        safe = c_min >= -HGRN_SAFE_LOG
        diag = []
        quad = []
        for d, (k, c, tot) in enumerate(per_dir):
            q32 = qs * jnp.exp(c)
            k32 = k * jnp.exp(tot - c)
            k_hat = jnp.where(safe, k * jnp.exp(-c), 0.0).astype(BF16)
            e32 = [jnp.exp(tot[j * cs:j * cs + 1]) for j in range(nch)]

            def span(lo, hi):
                out = None
                for j in range(lo, hi):
                    out = e32[j] if out is None else out * e32[j]
                return out

            def rescaled(x, facs):
                parts = [x[j * cs:(j + 1) * cs] if f is None else x[j * cs:(j + 1) * cs] * f
                         for j, f in enumerate(facs)]
                return jnp.concatenate(parts, axis=0).astype(BF16)

            def level(n):
                before = [span((j // n) * n, j) for j in range(nch)]
                after = [span(j + 1, (j // n + 1) * n) for j in range(nch)]
                qf, kf = (before, after) if d == 0 else (after, before)
                return rescaled(q32, qf), rescaled(k32, kf)

            q32b, k32b = q32.astype(BF16), k32.astype(BF16)
            q64, k64 = level(2)
            q128, k128 = level(4)
            q256, k256 = level(8)
            diag.append((_dot_nt(q32b, k_hat), _dot_nt(q32b, k32b), _dot_nt(q64, k64)))
            if d == 0:
                quad.append(_dot_nt(q128[half:], k128[:half]))
            else:
                quad.append(_dot_nt(q128[:half], k128[half:]))
            qe_ref[d, rows, :] = q256
            ke_ref[d, rows, :] = k256
            ee_ref[d, blk] = jnp.broadcast_to(span(0, nch), (8, DK_C))

        pieces = []
        for j in range(nch):
            rs = slice(j * cs, (j + 1) * cs)
            own = j * cs // half
            cols = slice(own * half, (own + 1) * half)
            acc = None
            for d in range(2):
                for i in range(3):
                    term = diag[d][i][rs, cols] * mask_ref[d, i, rs, :]
                    acc = term if acc is None else acc + term
            other = quad[0][rs.start - half:rs.stop - half] if own == 1 else quad[1][rs]
            pieces.append(jnp.concatenate([other, acc] if own == 1 else [acc, other], axis=1))
        att = jnp.concatenate(pieces, axis=0)
        oi_ref[rows, :] = _dot(att.astype(BF16), v)

        unsafe_ref[blk] = jnp.logical_not(safe).astype(jnp.int32)
        sq_ref[rows, :] = qs
        for d in range(2):
            sk_ref[d, rows, :] = per_dir[d][0]
            sc_ref[d, rows, :] = per_dir[d][1]
        return carry

    n_blk = GROUP_ROWS // sb
    lax.fori_loop(0, n_blk, block_step, 0, unroll=True)

    def vpu_block(blk, carry):
        @pl.when(unsafe_ref[blk] != 0)
        def _():
            def chunk_step(i, carry2):
                crow = pl.ds(pl.multiple_of(blk * sb + i * cs, cs), cs)
                q, vv = sq_ref[crow, :], i_ref[crow, :].astype(F32)
                oi_ref[crow, :] += (_hgrn_intra(q, sk_ref[0, crow, :], vv, sc_ref[0, crow, :], reverse=False)
                                    + _hgrn_intra(q, sk_ref[1, crow, :], vv, sc_ref[1, crow, :], reverse=True))
                return carry2

            lax.fori_loop(0, sb // cs, chunk_step, 0)

        return carry

    lax.fori_loop(0, n_blk, vpu_block, 0)

    is_ctx = pl.program_id(1) < n_ctx
    pl.when(is_ctx)(functools.partial(
        _hgrn_states, qe_ref, ke_ref, i_ref, ee_ref, oi_ref, t_ctx, st_ref=st_ref))
    pl.when(jnp.logical_not(is_ctx))(functools.partial(
        _hgrn_states, qe_ref, ke_ref, i_ref, ee_ref, oi_ref, t_lat, s0_ref=s0_ref))

    o = oi_ref[...]
    on = o * lax.rsqrt(jnp.mean(o * o, axis=-1, keepdims=True) + EPS) * gn_ref[...]
    o_ref[...] = (on * _silu(g_ref[...].astype(F32))).astype(o_ref.dtype)


def _hgrn(z16, z32, lower, gnorm, state_hgrn, oi, n_layers, prev_st, n_ctx_req, t_ctx, t_lat, n_rows):
    nh = N_HEADS_C
    n_ctx = n_ctx_req * t_ctx // GROUP_ROWS
    n_grp = n_rows // GROUP_ROWS
    rows = GROUP_ROWS
    assert t_lat == GROUP_ROWS and t_ctx % HGRN_SUPER == 0
    blk = lambda p: pl.BlockSpec((rows, DK_C), lambda h, g, p=p: (g, p * nh + h))
    st_spec, st_shape = _state_spec(n_ctx_req, n_layers, oi, (2, nh, DK_C, DK_C),
                                    (2, None, DK_C, DK_C), lambda h: (0, h, 0, 0), n_ctx, GROUP_ROWS // t_ctx)
    in_specs = [blk(0), blk(1), blk(2), blk(0), blk(1),
                pl.BlockSpec((2, DK_C), lambda h, g: (0, h)),
                pl.BlockSpec((1, DK_C), lambda h, g: (0, 0)),
                pl.BlockSpec((None, None, 2, None, DK_C, DK_C),
                             lambda h, g: (jnp.maximum(g - n_ctx, 0), oi, 0, h, 0, 0))]
    args = [z16, z16, z16, z32, z32, lower, gnorm, state_hgrn]
    aliases = {}
    if prev_st is not None:
        in_specs.append(pl.BlockSpec(memory_space=pl.ANY))
        args.append(prev_st)
        aliases = {len(args) - 1: 1}
    return pl.pallas_call(
        functools.partial(_hgrn_kernel, t_ctx=t_ctx, t_lat=t_lat, n_ctx=n_ctx),
        grid=(nh, n_grp),
        in_specs=in_specs,
        out_specs=[pl.BlockSpec((rows, DK_C), lambda h, g: (g, h)), st_spec],
        out_shape=[jax.ShapeDtypeStruct((n_rows, D_MODEL), BF16), st_shape],
        input_output_aliases=aliases,
        scratch_shapes=[
            pltpu.VMEM((2, rows, DK_C), BF16),
            pltpu.VMEM((2, rows, DK_C), BF16),
            pltpu.VMEM((2, rows // HGRN_SUPER, 8, DK_C), F32),
            pltpu.VMEM((rows, DK_C), F32),
            pltpu.VMEM((rows, DK_C), F32),
            pltpu.VMEM((2, rows, DK_C), F32),
            pltpu.VMEM((2, rows, DK_C), F32),
            pltpu.VMEM((2, 3, HGRN_SUPER, HGRN_SUPER // 2), F32),
            pltpu.SMEM((rows // HGRN_SUPER,), jnp.int32),
        ],
        compiler_params=_params(("arbitrary", "arbitrary"), 32),
        name="hgrn",
    )(*args)


def kernel(x_prompt, x_sample, cache_kv, state_ret, state_hgrn, c, c_ctx, w_mod, b_mod, norm_g,
           w_in_even, w_out_even, rpb, ret_decay, w_in_odd, w_out_odd, hgrn_lb, hgrn_gnorm,
           w_ffn_in, w_ffn_out):
    bp, tp, _ = x_prompt.shape
    bs, ts, _ = x_sample.shape
    np_rows, ns_rows = bp * tp, bs * ts
    n_rows = np_rows + ns_rows
    n_even, n_odd = w_in_even.shape[0], w_in_odd.shape[0]
    assert np_rows % GROUP_ROWS == 0 and ts == GROUP_ROWS
    xs = [x_prompt.reshape(np_rows, D_MODEL), x_sample.reshape(ns_rows, D_MODEL)]

    n_c = bs + 1
    pad = (-n_c) % 8
    cvec = jnp.concatenate([c, c_ctx[None], jnp.zeros((pad, D_MODEL), F32)], axis=0)
    mod_all = _modulation(cvec, w_mod, b_mod)
    grp = np.concatenate([np.full(np_rows // GROUP_ROWS, bs), np.arange(bs)]).astype(np.int32)
    mod_all = mod_all[:, grp].reshape(DEPTH, len(grp), 6, D_MODEL)
    mod_all = jnp.pad(mod_all, ((0, 0), (0, 0), (0, MOD_ROWS - 6), (0, 0)))

    p_lb = jax.nn.softmax(hgrn_lb.astype(F32), axis=0)
    lower = jnp.clip(jnp.cumsum(p_lb, axis=0) - p_lb[0], 0.0, 1.0)
    pair_bias = _na_bias_tables(rpb)
    rope_tabs = _rope_tables(ts)

    w_in_even_b, w_out_even_b = w_in_even.astype(BF16), w_out_even.astype(BF16)
    w_out_odd_b = w_out_odd.astype(BF16)
    w_in_odd_q = w_in_odd[:, :, :D_C].astype(BF16)
    w_in_odd_f = w_in_odd[:, :, D_C:3 * D_C].astype(BF16)
    w_in_odd_ig = w_in_odd[:, :, 3 * D_C:].astype(BF16)
    ffn_w = [_ffn_weights(w_ffn_in[l], w_ffn_out[l]) for l in range(DEPTH)]

    kv_out = ret_out = hg_out = None
    y_split = None
    for l in range(DEPTH):
        mod = mod_all[l]
        g = norm_g[l].reshape(4, 1, D_MODEL)
        if l % 2 == 0:
            e = l // 2
            z, = _in_proj(xs, g[0], mod, [([w_in_even_b[e]], BF16)])
            oa, kv_out = _attention(z, cache_kv, pair_bias, e, n_even, kv_out, bp, tp, n_rows)
            ob, ret_out = _retention(z, ret_decay[e], rope_tabs, state_ret, e, n_even, ret_out, bp, tp, ts, n_rows)
            x = _out_proj([oa, ob], w_out_even_b[e], xs, g[1], mod)
        else:
            oi = l // 2
            z16, z32 = _in_proj(xs, g[0], mod, [([w_in_odd_q[oi], w_in_odd_ig[oi]], BF16),
                                                ([w_in_odd_f[oi]], F32)])
            gn = hgrn_gnorm[oi].reshape(1, DK_C)
            o, hg_out = _hgrn(z16, z32, lower[oi], gn, state_hgrn, oi, n_odd, hg_out, bp, tp, ts, n_rows)
            x = _out_proj([o], w_out_odd_b[oi], xs, g[1], mod)
        if l == DEPTH - 1:
            y_split = _ffn(x, g[2], g[3], mod, *ffn_w[l], split=(x_prompt.shape, x_sample.shape))
        else:
            xs = [_ffn(x, g[2], g[3], mod, *ffn_w[l])]

    return (y_split[0], y_split[1], kv_out, ret_out, hg_out)
```

```python
import functools

import numpy as np
import jax
import jax.numpy as jnp
from jax import lax
from jax.experimental import pallas as pl
from jax.experimental.pallas import tpu as pltpu

F32 = jnp.float32
BF16 = jnp.bfloat16

D_MODEL = 1024
DEPTH = 4
GRID_W = 64
HEAD_DIM = 64
N_HEADS_A = 8
N_HEADS_B = 8
D_A = N_HEADS_A * HEAD_DIM
D_B = N_HEADS_B * HEAD_DIM
WIN_R = 8
WIN_C = 16
N_HEADS_C = 8
DK_C = D_MODEL // N_HEADS_C
D_C = N_HEADS_C * DK_C
D_FF = ((8 * D_MODEL // 3 + 255) // 256) * 256
ATTN_SCALE = HEAD_DIM ** -0.5
HGRN_CHUNK = 32
HGRN_SUPER = 256
HGRN_SAFE_LOG = 75.0
ROPE_BASE = 10000.0
EPS = 1e-6
MASK_NEG = -1e30
F_MIN = 1e-30

GROUP_ROWS = 1024
MOD_ROWS = 8
LANES = 128
HEADS_PER_TILE = LANES // HEAD_DIM
PROJ_TN = 512
FFN_TF = 256
MOD_TN = 1536
MIB = 1024 * 1024

SH1, SC1, GT1, SH2, SC2, GT2 = range(6)

NA_QROWS = 4
NA_KROWS = 12
NA_KSTART = (0, 0, 4, 4)


def _params(sem, vmem_mib):
    return pltpu.CompilerParams(dimension_semantics=sem, vmem_limit_bytes=vmem_mib * MIB)


def _sigmoid(x):
    return 1.0 / (1.0 + jnp.exp(-x))


def _silu(x):
    return x * _sigmoid(x)


def _dot(a, b):
    return jnp.dot(a, b, preferred_element_type=F32)


def _dot_nt(a, b):
    return lax.dot_general(a, b, (((1,), (1,)), ((), ())), preferred_element_type=F32)


def _dot_tt(a, b):
    return lax.dot_general(a, b, (((0,), (1,)), ((), ())), preferred_element_type=F32)


def _dot_tn(a, b):
    return lax.dot_general(a, b, (((0,), (0,)), ((), ())), preferred_element_type=F32)


def _mod_kernel(c_ref, w_ref, b_ref, o_ref):
    s = _silu(c_ref[...]).astype(BF16)
    o_ref[...] = _dot(s, w_ref[...].astype(BF16)) + b_ref[...]


def _modulation(cvec, w_mod, b_mod):
    rows = cvec.shape[0]
    n = w_mod.shape[-1]
    return pl.pallas_call(
        _mod_kernel,
        grid=(DEPTH, n // MOD_TN),
        in_specs=[
            pl.BlockSpec((rows, D_MODEL), lambda l, j: (0, 0)),
            pl.BlockSpec((None, D_MODEL, MOD_TN), lambda l, j: (l, 0, j)),
            pl.BlockSpec((None, 1, MOD_TN), lambda l, j: (l, 0, j)),
        ],
        out_specs=pl.BlockSpec((None, rows, MOD_TN), lambda l, j: (l, 0, j)),
        out_shape=jax.ShapeDtypeStruct((DEPTH, rows, n), F32),
        compiler_params=_params(("arbitrary", "arbitrary"), 32),
        name="modulation",
    )(cvec, w_mod, b_mod.reshape(DEPTH, 1, n))


def _norm_mod(x, g, mod, sh_row, sc_row):
    y = x * lax.rsqrt(jnp.mean(x * x, axis=-1, keepdims=True) + EPS) * g
    return y * (1.0 + mod[sc_row:sc_row + 1]) + mod[sh_row:sh_row + 1]


def _stream_specs(xs):
    tm = GROUP_ROWS
    if len(xs) == 1:
        return [pl.BlockSpec((tm, D_MODEL), lambda i: (i, 0))]
    n_ctx = xs[0].shape[0] // tm
    return [pl.BlockSpec((tm, D_MODEL), lambda i: (jnp.minimum(i, n_ctx - 1), 0)),
            pl.BlockSpec((tm, D_MODEL), lambda i: (jnp.maximum(i - n_ctx, 0), 0))]


def _stream_tile(x_refs, n_ctx):
    if len(x_refs) == 1:
        return x_refs[0][...]
    return jnp.where(pl.program_id(0) < n_ctx, x_refs[0][...], x_refs[1][...])


def _in_proj_kernel(*refs, n_x, n_ctx, n_w):
    x_refs, (g_ref, mod_ref) = refs[:n_x], refs[n_x:n_x + 2]
    w_refs, o_refs = refs[n_x + 2:n_x + 2 + sum(n_w)], refs[n_x + 2 + sum(n_w):]
    h = _norm_mod(_stream_tile(x_refs, n_ctx), g_ref[...], mod_ref[...], SH1, SC1).astype(BF16)
    w_iter = iter(w_refs)
    for o_ref, count in zip(o_refs, n_w):
        col0 = 0
        for w_ref in (next(w_iter) for _ in range(count)):
            for j in range(w_ref.shape[1] // PROJ_TN):
                cols = slice(j * PROJ_TN, (j + 1) * PROJ_TN)
                o_ref[:, col0 + j * PROJ_TN:col0 + (j + 1) * PROJ_TN] = _dot(h, w_ref[:, cols]).astype(o_ref.dtype)
            col0 += w_ref.shape[1]


def _resident(block_shape):
    return pl.BlockSpec(block_shape, lambda *_: (0,) * len(block_shape), pipeline_mode=pl.Buffered(1))


def _in_proj(xs, g, mod, outputs):
    m = sum(x.shape[0] for x in xs)
    tm = GROUP_ROWS
    ws = [w for pieces, _ in outputs for w in pieces]
    ns = [sum(w.shape[1] for w in pieces) for pieces, _ in outputs]
    assert all(w.shape[1] % PROJ_TN == 0 for w in ws)
    return pl.pallas_call(
        functools.partial(_in_proj_kernel, n_x=len(xs), n_ctx=xs[0].shape[0] // tm,
                          n_w=tuple(len(pieces) for pieces, _ in outputs)),
        grid=(m // tm,),
        in_specs=_stream_specs(xs) + [
            pl.BlockSpec((1, D_MODEL), lambda i: (0, 0)),
            pl.BlockSpec((None, MOD_ROWS, D_MODEL), lambda i: (i, 0, 0)),
        ] + [_resident(w.shape) for w in ws],
        out_specs=[pl.BlockSpec((tm, n), lambda i: (i, 0)) for n in ns],
        out_shape=[jax.ShapeDtypeStruct((m, n), dt) for n, (_, dt) in zip(ns, outputs)],
        compiler_params=_params(("arbitrary",), 56),
        name="in_proj",
    )(*xs, g, mod, *ws)


def _out_proj_kernel(*refs, n_in, n_x, n_ctx):
    a_refs, w_refs = refs[:n_in], refs[n_in:2 * n_in]
    x_refs = refs[2 * n_in:2 * n_in + n_x]
    g_ref, mod_ref, o_ref = refs[2 * n_in + n_x:]
    y = _dot(a_refs[0][...], w_refs[0][...])
    for a_ref, w_ref in zip(a_refs[1:], w_refs[1:]):
        y = y + _dot(a_ref[...], w_ref[...])
    yn = y * lax.rsqrt(jnp.mean(y * y, axis=-1, keepdims=True) + EPS) * g_ref[...]
    o_ref[...] = _stream_tile(x_refs, n_ctx) + mod_ref[GT1:GT1 + 1, :] * yn


def _out_proj(acts, w, xs, g, mod):
    m = sum(x.shape[0] for x in xs)
    tm = GROUP_ROWS
    n_in = len(acts)
    ks = [a.shape[1] for a in acts]
    assert sum(ks) == w.shape[0] and len(set(ks)) == 1
    in_specs = [pl.BlockSpec((tm, k), lambda i: (i, 0)) for k in ks]
    in_specs += [pl.BlockSpec((ks[0], D_MODEL), lambda i, p=p: (p, 0)) for p in range(n_in)]
    in_specs += _stream_specs(xs) + [
        pl.BlockSpec((1, D_MODEL), lambda i: (0, 0)),
        pl.BlockSpec((None, MOD_ROWS, D_MODEL), lambda i: (i, 0, 0)),
    ]
    return pl.pallas_call(
        functools.partial(_out_proj_kernel, n_in=n_in, n_x=len(xs), n_ctx=xs[0].shape[0] // tm),
        grid=(m // tm,),
        in_specs=in_specs,
        out_specs=pl.BlockSpec((tm, D_MODEL), lambda i: (i, 0)),
        out_shape=jax.ShapeDtypeStruct((m, D_MODEL), F32),
        compiler_params=_params(("arbitrary",), 48),
        name="out_proj",
    )(*acts, *([w] * n_in), *xs, g, mod)


def _ffn_kernel(x_ref, xn_ref, g2_ref, g3_ref, mod_ref, modn_ref, win_ref, wo_ref, *rest, n_ctx_groups):
    o_refs, (h_ref, acc_ref) = rest[:-2], rest[-2:]
    tm = x_ref.shape[0]
    nf = wo_ref.shape[0]
    ahead = -(-tm // (nf * 16)) * 16
    i = pl.program_id(0)
    cur, nxt = i % 2, (i + 1) % 2

    @pl.when(i == 0)
    def _():
        h_ref[0] = _norm_mod(x_ref[...], g2_ref[...], mod_ref[...], SH2, SC2).astype(BF16)

    acc_ref[...] = jnp.zeros_like(acc_ref)

    def hidden_step(f, carry):
        h = h_ref[cur]
        a = _dot(h, win_ref[:, pl.ds(pl.multiple_of(f * FFN_TF, FFN_TF), FFN_TF)])
        u = _dot(h, win_ref[:, pl.ds(pl.multiple_of(D_FF + f * FFN_TF, FFN_TF), FFN_TF)])
        acc_ref[...] += _dot((_silu(a) * u).astype(BF16), wo_ref[f])
        r0 = pl.multiple_of(jnp.minimum(f * ahead, tm - ahead), 16)
        rows = pl.ds(r0, ahead)
        h_ref[nxt, rows, :] = _norm_mod(xn_ref[rows, :], g2_ref[...], modn_ref[...], SH2, SC2).astype(BF16)
        return carry

    lax.fori_loop(0, nf, hidden_step, 0, unroll=True)

    def finish(o_ref):
        y = acc_ref[...]
        yn = y * lax.rsqrt(jnp.mean(y * y, axis=-1, keepdims=True) + EPS) * g3_ref[...]
        o_ref[...] = (x_ref[...] + mod_ref[GT2:GT2 + 1, :] * yn).reshape(o_ref.shape)

    if len(o_refs) == 1:
        finish(o_refs[0])
    else:
        is_ctx = pl.program_id(0) < n_ctx_groups
        pl.when(is_ctx)(functools.partial(finish, o_refs[0]))
        pl.when(jnp.logical_not(is_ctx))(functools.partial(finish, o_refs[1]))


def _ffn_weights(w_in, w_out):
    return w_in.astype(BF16), w_out.astype(BF16).reshape(D_FF // FFN_TF, FFN_TF, D_MODEL)


def _ffn(x, g2, g3, mod, w_in, wo, split=None):
    m = x.shape[0]
    tm = GROUP_ROWS
    n_ctx = 0
    if split is None:
        out_specs = pl.BlockSpec((tm, D_MODEL), lambda i: (i, 0))
        out_shape = jax.ShapeDtypeStruct((m, D_MODEL), F32)
    else:
        (bp, tp, _), (bs, ts, _) = split
        n_ctx = bp * tp // tm
        assert ts == tm and (bp * tp) % tm == 0
        out_specs = [
            pl.BlockSpec((tm // tp, tp, D_MODEL), lambda i: (jnp.minimum(i, n_ctx - 1), 0, 0)),
            pl.BlockSpec((1, ts, D_MODEL), lambda i: (jnp.maximum(i - n_ctx, 0), 0, 0)),
        ]
        out_shape = [jax.ShapeDtypeStruct(s, F32) for s in split]
    last = m // tm - 1
    return pl.pallas_call(
        functools.partial(_ffn_kernel, n_ctx_groups=n_ctx),
        grid=(m // tm,),
        in_specs=[
            pl.BlockSpec((tm, D_MODEL), lambda i: (i, 0)),
            pl.BlockSpec((tm, D_MODEL), lambda i: (jnp.minimum(i + 1, last), 0)),
            pl.BlockSpec((1, D_MODEL), lambda i: (0, 0)),
            pl.BlockSpec((1, D_MODEL), lambda i: (0, 0)),
            pl.BlockSpec((None, MOD_ROWS, D_MODEL), lambda i: (i, 0, 0)),
            pl.BlockSpec((None, MOD_ROWS, D_MODEL), lambda i: (jnp.minimum(i + 1, last), 0, 0)),
            _resident(w_in.shape), _resident(wo.shape),
        ],
        out_specs=out_specs,
        out_shape=out_shape,
        scratch_shapes=[pltpu.VMEM((2, tm, D_MODEL), BF16), pltpu.VMEM((tm, D_MODEL), F32)],
        compiler_params=_params(("arbitrary",), 60),
        name="ffn",
    )(x, x, g2, g3, mod, mod, w_in, wo)


def _state_spec(n_req, n_layers, layer, tail, tail_block, tail_index, n_ctx, nb):
    n_own = n_layers if layer == 0 else 1
    first = 0 if layer == 0 else layer

    def index(j, g):
        return (jnp.minimum(g, n_ctx - 1), first) + tail_index(j)

    spec = pl.BlockSpec((nb, n_own) + tail_block, index)
    shape = jax.ShapeDtypeStruct((n_req, n_layers) + tail, F32)
    return spec, shape


def _zero_other_layers(st_ref):
    if st_ref.shape[1] > 1:
        st_ref[:, 1:] = jnp.zeros((st_ref.shape[0], st_ref.shape[1] - 1) + st_ref.shape[2:], st_ref.dtype)


NA_NRO = 2 * WIN_R - 1


def _na_bias_tables(rpb):
    qc = np.arange(GRID_W)[:, None]
    kc = np.arange(GRID_W)[None, :]
    win0 = np.clip(qc - WIN_C // 2, 0, GRID_W - WIN_C)
    col_valid = (kc >= win0) & (kc < win0 + WIN_C)
    col_off = np.clip(kc - qc, 1 - WIN_C, WIN_C - 1) + (WIN_C - 1)
    onehot = (col_off[None] == np.arange(2 * WIN_C - 1)[:, None, None]).astype(np.float32)
    tiles = jnp.einsum('ehrc,cqk->ehrqk', rpb.astype(F32), jnp.asarray(onehot), precision=lax.Precision.HIGHEST)
    tiles = jnp.where(col_valid[None, None, None], tiles, MASK_NEG)
    nxt = jnp.concatenate([tiles[:, :, 1:], tiles[:, :, -1:]], axis=2)
    return jnp.concatenate([tiles, nxt], axis=-1)


def _na_group_bias(pair_ref, hh, g):
    rows = GROUP_ROWS // GRID_W
    first_half = lax.broadcasted_iota(jnp.int32, (GRID_W, 2 * GRID_W), 1) < GRID_W
    bias = []
    for a in range(NA_QROWS):
        r = g * NA_QROWS + a
        row0 = min(max(r - WIN_R // 2, 0), rows - WIN_R)
        pieces = []
        for w in range(0, NA_KROWS, 2):
            kr = NA_KSTART[g] + w
            ro = kr - r + (WIN_R - 1)
            in0 = row0 <= kr < row0 + WIN_R
            in1 = row0 <= kr + 1 < row0 + WIN_R
            assert not (in0 or in1) or 0 <= ro < NA_NRO
            if in0 and in1:
                piece = pair_ref[hh, ro]
            elif in0:
                piece = jnp.where(first_half, pair_ref[hh, ro], MASK_NEG)
            elif in1:
                piece = jnp.where(first_half, MASK_NEG, pair_ref[hh, ro])
            else:
                piece = jnp.full((GRID_W, 2 * GRID_W), MASK_NEG, F32)
            pieces.append(piece)
        bias.append(jnp.concatenate(pieces, axis=1))
    return jnp.concatenate(bias, axis=0)


def _ctx_attend(q_ref, k_ref, v_ref, o_ref, kv_ref, t):
    for r in range(GROUP_ROWS // t):
        rs = slice(r * t, (r + 1) * t)
        q2, k2, v2 = q_ref[rs, :] * ATTN_SCALE, k_ref[rs, :], v_ref[rs, :]
        head_of_lane = lax.broadcasted_iota(jnp.int32, q2.shape, 1) // HEAD_DIM
        out = None
        for hh in range(HEADS_PER_TILE):
            hs = slice(hh * HEAD_DIM, (hh + 1) * HEAD_DIM)
            kv_ref[r, 0, 0, hh] = k2[:, hs].astype(F32)
            kv_ref[r, 0, 1, hh] = v2[:, hs].astype(F32)
            in_head = head_of_lane == hh
            s = _dot_nt(jnp.where(in_head, q2, jnp.zeros_like(q2)), k2)
            p = jnp.exp(s - jnp.max(s, axis=-1, keepdims=True))
            l = jnp.sum(p, axis=-1, keepdims=True)
            o = _dot(p.astype(BF16), v2) / l
            out = o if out is None else jnp.where(in_head, o, out)
        o_ref[rs, :] = out.astype(o_ref.dtype)
    _zero_other_layers(kv_ref)


def _na_attend(q_ref, k_ref, v_ref, ckv_ref, pair_ref, o_ref):
    q2, k2, v2 = q_ref[...] * ATTN_SCALE, k_ref[...], v_ref[...]
    tq = NA_QROWS * GRID_W
    tk = NA_KROWS * GRID_W
    head_of_lane = lax.broadcasted_iota(jnp.int32, q2.shape, 1) // HEAD_DIM
    kc2 = jnp.concatenate([ckv_ref[0, hh] for hh in range(HEADS_PER_TILE)], axis=1).astype(BF16)
    vc2 = jnp.concatenate([ckv_ref[1, hh] for hh in range(HEADS_PER_TILE)], axis=1).astype(BF16)
    keys = [jnp.concatenate([k2[s0 * GRID_W:s0 * GRID_W + tk], kc2], axis=0) for s0 in NA_KSTART]
    vals = [jnp.concatenate([v2[s0 * GRID_W:s0 * GRID_W + tk], vc2], axis=0) for s0 in NA_KSTART]
    no_bias = jnp.zeros((tq, kc2.shape[0]), F32)
    out = None
    for hh in range(HEADS_PER_TILE):
        in_head = head_of_lane == hh
        q = jnp.where(in_head, q2, jnp.zeros_like(q2))
        rows = []
        for g in range(len(NA_KSTART)):
            bias = jnp.concatenate([_na_group_bias(pair_ref, hh, g), no_bias], axis=1)
            s = _dot_nt(q[g * tq:(g + 1) * tq], keys[g]) + bias
            p = jnp.exp(s - jnp.max(s, axis=-1, keepdims=True))
            l = jnp.sum(p, axis=-1, keepdims=True)
            rows.append(_dot_tt(vals[g], p.astype(BF16)).T / l)
        o_h = jnp.concatenate(rows, axis=0)
        out = o_h if out is None else jnp.where(in_head, o_h, out)
    o_ref[...] = out.astype(o_ref.dtype)


def _attn_kernel(q_ref, k_ref, v_ref, ckv_ref, pair_ref, *rest, t_ctx, n_ctx):
    o_ref, kv_ref = rest[-2:]
    is_ctx = pl.program_id(1) < n_ctx
    pl.when(is_ctx)(functools.partial(_ctx_attend, q_ref, k_ref, v_ref, o_ref, kv_ref, t_ctx))
    pl.when(jnp.logical_not(is_ctx))(
        functools.partial(_na_attend, q_ref, k_ref, v_ref, ckv_ref, pair_ref, o_ref))


def _attention(z, cache_kv, pair_bias, e, n_layers, prev_kv, n_ctx_req, t_ctx, n_rows):
    npair = D_A // LANES
    hp = HEADS_PER_TILE
    n_ctx = n_ctx_req * t_ctx // GROUP_ROWS
    n_grp = n_rows // GROUP_ROWS
    past = cache_kv.shape[-2]
    blk = lambda p: pl.BlockSpec((GROUP_ROWS, LANES), lambda j, g, p=p: (g, p * npair + j))
    kv_spec, kv_shape = _state_spec(n_ctx_req, n_layers, e, (2, N_HEADS_A, t_ctx, HEAD_DIM),
                                    (2, hp, t_ctx, HEAD_DIM), lambda j: (0, j, 0, 0), n_ctx, GROUP_ROWS // t_ctx)
    in_specs = [
        blk(0), blk(1), blk(2),
        pl.BlockSpec((None, None, 2, hp, past, HEAD_DIM), lambda j, g: (jnp.maximum(g - n_ctx, 0), e, 0, j, 0, 0)),
        pl.BlockSpec((None, hp) + pair_bias.shape[2:], lambda j, g: (e, j, 0, 0, 0)),
    ]
    args = [z, z, z, cache_kv, pair_bias]
    aliases = {}
    if prev_kv is not None:
        in_specs.append(pl.BlockSpec(memory_space=pl.ANY))
        args.append(prev_kv)
        aliases = {len(args) - 1: 1}
    return pl.pallas_call(
        functools.partial(_attn_kernel, t_ctx=t_ctx, n_ctx=n_ctx),
        grid=(npair, n_grp),
        in_specs=in_specs,
        out_specs=[pl.BlockSpec((GROUP_ROWS, LANES), lambda j, g: (g, j)), kv_spec],
        out_shape=[jax.ShapeDtypeStruct((n_rows, D_A), BF16), kv_shape],
        input_output_aliases=aliases,
        compiler_params=_params(("arbitrary", "arbitrary"), 56),
        name="attention",
    )(*args)


def _rope_tables(t):
    half = HEAD_DIM // 2
    nf = half // 2
    inv = ROPE_BASE ** (-np.arange(nf, dtype=np.float32) / nf)
    pos = np.arange(t)
    ang_r = (pos // GRID_W).astype(np.float32)[:, None] * inv[None, :]
    ang_c = (pos % GRID_W).astype(np.float32)[:, None] * inv[None, :]
    ang_r, ang_c = jnp.asarray(ang_r), jnp.asarray(ang_c)
    cr, sr, cc, sc = jnp.cos(ang_r), jnp.sin(ang_r), jnp.cos(ang_c), jnp.sin(ang_c)
    cos = jnp.concatenate([cr, cr, cc, cc], axis=-1)
    sin = jnp.concatenate([-sr, sr, -sc, sc], axis=-1)
    return jnp.tile(cos, (1, HEADS_PER_TILE)), jnp.tile(sin, (1, HEADS_PER_TILE))


def _rope(x, cos, sin):
    nf = HEAD_DIM // 4
    lane = lax.broadcasted_iota(jnp.int32, x.shape, 1)
    partner = jnp.where(lane % (2 * nf) < nf, pltpu.roll(x, LANES - nf, axis=1), pltpu.roll(x, nf, axis=1))
    return x * cos + partner * sin


def _decay_matrix(dm_ref, lg, t):
    n_i = lax.broadcasted_iota(jnp.int32, (t, t), 0)
    m_i = lax.broadcasted_iota(jnp.int32, (t, t), 1)
    diff = (n_i - m_i).astype(F32)
    dist = jnp.abs(diff)
    for hh in range(HEADS_PER_TILE):
        one_sided = jnp.exp(jnp.where(diff >= 0, lg[hh][0], lg[hh][1]) * dist)
        dm_ref[hh] = jnp.where(diff == 0, 2.0, one_sided)


def _head_rms_gate(o, g, head_of_lane):
    sq = o * o
    ms = sum(jnp.where(head_of_lane == hh,
                       jnp.sum(jnp.where(head_of_lane == hh, sq, 0.0), axis=-1, keepdims=True), 0.0)
             for hh in range(HEADS_PER_TILE)) * (1.0 / HEAD_DIM)
    return o * lax.rsqrt(ms + EPS) * _silu(g)


def _two_sided_scores(q, kb, vb, dm_ref, head_of_lane):
    o = None
    for hh in range(HEADS_PER_TILE):
        in_head = head_of_lane == hh
        att = _dot_nt(jnp.where(in_head, q, 0.0).astype(BF16), kb) * dm_ref[hh]
        o_h = _dot(att.astype(BF16), vb)
        o = o_h if o is None else jnp.where(in_head, o_h, o)
    return o


def _retain(q_ref, k_ref, v_ref, g_ref, o_ref, st_ref, dm_ref, lg, t):
    pos = lax.broadcasted_iota(jnp.int32, (t, 1), 0).astype(F32)
    head_of_lane = lax.broadcasted_iota(jnp.int32, (1, LANES), 1) // HEAD_DIM
    lgv = [sum(jnp.where(head_of_lane == hh, lg[hh][d], 0.0) for hh in range(HEADS_PER_TILE)) for d in range(2)]
    for r in range(GROUP_ROWS // t):
        rs = slice(r * t, (r + 1) * t)
        q2 = q_ref[rs, :].astype(F32)
        k2 = k_ref[rs, :].astype(F32) * (HEAD_DIM ** -0.5)
        v2 = v_ref[rs, :]
        st = (_dot_tn((k2 * jnp.exp(lgv[0] * (t - 1.0 - pos))).astype(BF16), v2),
              _dot_tn((k2 * jnp.exp(lgv[1] * pos)).astype(BF16), v2))
        for hh in range(HEADS_PER_TILE):
            hs = slice(hh * HEAD_DIM, (hh + 1) * HEAD_DIM)
            st_ref[r, 0, 0, hh] = st[0][hs, hs]
            st_ref[r, 0, 1, hh] = st[1][hs, hs]
        o = _two_sided_scores(q2, k2.astype(BF16), v2, dm_ref, head_of_lane)
        o_ref[rs, :] = _head_rms_gate(o, g_ref[rs, :].astype(F32), head_of_lane).astype(o_ref.dtype)
    _zero_other_layers(st_ref)


def _retain_chunked(q_ref, k_ref, v_ref, g_ref, o_ref, dm_ref, lg, t, c, rope, s0_ref):
    n_c = t // c
    pos = lax.broadcasted_iota(jnp.int32, (c, 1), 0).astype(F32)
    head_of_lane = lax.broadcasted_iota(jnp.int32, (1, LANES), 1) // HEAD_DIM
    head_of_row = lax.broadcasted_iota(jnp.int32, (LANES, 1), 0) // HEAD_DIM
    same_head = head_of_row == head_of_lane
    lgv = [sum(jnp.where(head_of_lane == hh, lg[hh][d], 0.0) for hh in range(HEADS_PER_TILE)) for d in range(2)]
    lgc = [sum(jnp.where(head_of_row == hh, lg[hh][d], 0.0) for hh in range(HEADS_PER_TILE)) for d in range(2)]
    chunk_decay = [jnp.exp(lgc[d] * float(c)) for d in range(2)]
    q_dec = (jnp.exp(lgv[0] * (pos + 1.0)), jnp.exp(lgv[1] * (c - pos)))
    k_dec = (jnp.exp(lgv[0] * (c - 1.0 - pos)), jnp.exp(lgv[1] * pos))

    q2 = _rope(q_ref[...].astype(F32), rope[0][...], rope[1][...])
    k2 = _rope(k_ref[...].astype(F32) * (HEAD_DIM ** -0.5), rope[0][...], rope[1][...])
    cs = [slice(i * c, (i + 1) * c) for i in range(n_c)]

    def initial(d):
        zero = jnp.zeros((HEAD_DIM, HEAD_DIM), F32)
        rows = [jnp.concatenate([s0_ref[d, hh] if col == hh else zero for col in range(HEADS_PER_TILE)], axis=1)
                for hh in range(HEADS_PER_TILE)]
        return jnp.concatenate(rows, axis=0)

    incr = [[jnp.where(same_head, _dot_tn((k2[s] * k_dec[d]).astype(BF16), v_ref[s, :]), 0.0) for s in cs]
            for d in range(2)]
    fwd = [initial(0)]
    for i in range(n_c - 1):
        fwd.append(fwd[i] * chunk_decay[0] + incr[0][i])
    bwd = [initial(1)]
    for i in reversed(range(1, n_c)):
        bwd.insert(0, bwd[0] * chunk_decay[1] + incr[1][i])

    for i, s in enumerate(cs):
        qc = q2[s]
        o = _two_sided_scores(qc, k2[s].astype(BF16), v_ref[s, :], dm_ref, head_of_lane)
        o = o + _dot((qc * q_dec[0]).astype(BF16), fwd[i].astype(BF16))
        o = o + _dot((qc * q_dec[1]).astype(BF16), bwd[i].astype(BF16))
        o_ref[s, :] = _head_rms_gate(o, g_ref[s, :].astype(F32), head_of_lane).astype(o_ref.dtype)


def _ret_kernel(dec_ref, q_ref, k_ref, v_ref, g_ref, cos_ref, sin_ref, s0_ref, *rest, t_ctx, t_lat, n_ctx):
    o_ref, st_ref, dm_ref = rest[-3:]
    j, g = pl.program_id(0), pl.program_id(1)
    lg = [[-jnp.exp(jnp.full((1, 1), dec_ref[d, j * HEADS_PER_TILE + hh], F32)) for d in range(2)]
          for hh in range(HEADS_PER_TILE)]
    pl.when(g == 0)(functools.partial(_decay_matrix, dm_ref, lg, t_ctx))
    is_ctx = g < n_ctx
    pl.when(is_ctx)(functools.partial(_retain, q_ref, k_ref, v_ref, g_ref, o_ref, st_ref, dm_ref, lg, t_ctx))
    pl.when(jnp.logical_not(is_ctx))(functools.partial(
        _retain_chunked, q_ref, k_ref, v_ref, g_ref, o_ref, dm_ref, lg, t_lat, t_ctx, (cos_ref, sin_ref), s0_ref))


def _retention(z, ret_decay, rope_tabs, state_ret, e, n_layers, prev_st, n_ctx_req, t_ctx, t_lat, n_rows):
    npair = D_B // LANES
    hp = HEADS_PER_TILE
    sec0 = 3 * D_A // LANES
    n_ctx = n_ctx_req * t_ctx // GROUP_ROWS
    n_grp = n_rows // GROUP_ROWS
    assert t_lat == GROUP_ROWS
    blk = lambda p: pl.BlockSpec((GROUP_ROWS, LANES), lambda j, g, p=p: (g, sec0 + p * npair + j))
    st_spec, st_shape = _state_spec(n_ctx_req, n_layers, e, (2, N_HEADS_B, HEAD_DIM, HEAD_DIM),
                                    (2, hp, HEAD_DIM, HEAD_DIM), lambda j: (0, j, 0, 0), n_ctx,
                                    GROUP_ROWS // t_ctx)
    in_specs = [
        pl.BlockSpec(memory_space=pltpu.SMEM), blk(0), blk(1), blk(2), blk(3),
        pl.BlockSpec((t_lat, LANES), lambda j, g: (0, 0)),
        pl.BlockSpec((t_lat, LANES), lambda j, g: (0, 0)),
        pl.BlockSpec((None, None, 2, hp, HEAD_DIM, HEAD_DIM),
                     lambda j, g: (jnp.maximum(g - n_ctx, 0), e, 0, j, 0, 0)),
    ]
    args = [ret_decay, z, z, z, z, rope_tabs[0], rope_tabs[1], state_ret]
    aliases = {}
    if prev_st is not None:
        in_specs.append(pl.BlockSpec(memory_space=pl.ANY))
        args.append(prev_st)
        aliases = {len(args) - 1: 1}
    return pl.pallas_call(
        functools.partial(_ret_kernel, t_ctx=t_ctx, t_lat=t_lat, n_ctx=n_ctx),
        grid=(npair, n_grp),
        in_specs=in_specs,
        out_specs=[pl.BlockSpec((GROUP_ROWS, LANES), lambda j, g: (g, j)), st_spec],
        out_shape=[jax.ShapeDtypeStruct((n_rows, D_B), BF16), st_shape],
        input_output_aliases=aliases,
        scratch_shapes=[pltpu.VMEM((hp, t_ctx, t_ctx), F32)],
        compiler_params=_params(("arbitrary", "arbitrary"), 56),
        name="retention",
    )(*args)


def _seg_scan(x, reverse):
    t = x.shape[0]
    row = lax.broadcasted_iota(jnp.int32, (t, 1), 0) % HGRN_CHUNK
    sft = 1
    while sft < HGRN_CHUNK:
        if reverse:
            x = x + jnp.where(row < HGRN_CHUNK - sft, pltpu.roll(x, t - sft, axis=0), 0.0)
        else:
            x = x + jnp.where(row >= sft, pltpu.roll(x, sft, axis=0), 0.0)
        sft *= 2
    return x


def _hgrn_intra(q, k, v, c, reverse):
    cs = HGRN_CHUNK
    sub = 8
    row = lax.broadcasted_iota(jnp.int32, (cs, 1), 0)
    parts = [jnp.zeros((sub, v.shape[1]), F32) for _ in range(cs // sub)]
    for s in range(cs):
        blk_s = s // sub
        blks = range(0, blk_s + 1) if reverse else range(blk_s, cs // sub)
        c_s, k_s, v_s = c[s:s + 1], k[s:s + 1], v[s:s + 1]
        for bt in blks:
            rs = slice(bt * sub, (bt + 1) * sub)
            w = q[rs] * k_s * jnp.exp(jnp.minimum(c[rs] - c_s, 0.0))
            col = jnp.sum(w, axis=-1, keepdims=True)
            if bt == blk_s:
                keep = (row[rs] <= s) if reverse else (row[rs] >= s)
                col = jnp.where(keep, col, 0.0)
            parts[bt] = parts[bt] + col * v_s
    return jnp.concatenate(parts, axis=0)


def _hgrn_states(qe_ref, ke_ref, v_ref, ee_ref, oi_ref, t, s0_ref=None, st_ref=None):
    sb = HGRN_SUPER
    n_sb = t // sb
    for r in range(GROUP_ROWS // t):
        incr = []
        for j in range(n_sb):
            rows = slice((r * n_sb + j) * sb, (r * n_sb + j + 1) * sb)
            keys = jnp.concatenate([ke_ref[0, rows, :], ke_ref[1, rows, :]], axis=1)
            incr.append(_dot_tn(keys, v_ref[rows, :]))
        for d in range(2):
            st = s0_ref[d] if s0_ref is not None else None
            order = range(n_sb) if d == 0 else reversed(range(n_sb))
            for j in order:
                blk = r * n_sb + j
                rows = slice(blk * sb, (blk + 1) * sb)
                upd = incr[j][d * DK_C:(d + 1) * DK_C]
                if st is None:
                    st = upd
                else:
                    oi_ref[rows, :] += _dot(qe_ref[d, rows, :], st.astype(BF16))
                    decay = jnp.broadcast_to(ee_ref[d, blk][0:1], (DK_C, DK_C)).T
                    st = st * decay + upd
            if st_ref is not None:
                st_ref[r, 0, d] = st
    if st_ref is not None:
        _zero_other_layers(st_ref)


def _hgrn_kernel(q_ref, i_ref, g_ref, ff_ref, fb_ref, lb_ref, gn_ref, s0_ref, *rest, t_ctx, t_lat, n_ctx):
    (o_ref, st_ref, qe_ref, ke_ref, ee_ref, oi_ref,
     sq_ref, sk_ref, sc_ref, mask_ref, unsafe_ref) = rest[-11:]
    cs = HGRN_CHUNK
    sb = HGRN_SUPER
    half = sb // 2
    nch = sb // cs

    @pl.when((pl.program_id(0) == 0) & (pl.program_id(1) == 0))
    def _():
        t = lax.broadcasted_iota(jnp.int32, (sb, half), 0)
        s = lax.broadcasted_iota(jnp.int32, (sb, half), 1) + ((t >> 7) << 7)
        same32 = (t >> 5) == (s >> 5)
        same64 = (t >> 6) == (s >> 6)
        for d, (incl, strict) in enumerate(((s <= t, s < t), (s >= t, s > t))):
            for i, m in enumerate((same32 & incl, same64 & jnp.logical_not(same32) & strict,
                                   jnp.logical_not(same64) & strict)):
                mask_ref[d, i] = jnp.where(m, 1.0, 0.0)

    def block_step(blk, carry):
        rows = pl.ds(pl.multiple_of(blk * sb, sb), sb)
        qs = _silu(q_ref[rows, :].astype(F32)) * (DK_C ** -0.5)
        v = i_ref[rows, :]
        per_dir = []
        c_min = None
        for d, f_ref in enumerate((ff_ref, fb_ref)):
            lb = lb_ref[d:d + 1, :]
            fr = f_ref[rows, :]
            e = jnp.exp(-jnp.abs(fr))
            r = 1.0 / (1.0 + e)
            sig_pos = jnp.where(fr >= 0, r, e * r)
            sig_neg = jnp.where(fr >= 0, e * r, r)
            f = lb + (1.0 - lb) * sig_pos
            k = (1.0 - lb) * sig_neg
            c = _seg_scan(jnp.log(jnp.maximum(f, F_MIN)), reverse=(d == 1))
            c3 = c.reshape(sb // cs, cs, DK_C)
            c_end = c3[:, cs - 1:cs, :] if d == 0 else c3[:, 0:1, :]
            tot = jnp.broadcast_to(c_end, c3.shape).reshape(sb, DK_C)
            per_dir.append((k, c, tot))
            m = jnp.min(c)
            c_min = m if c_min is None else jnp.minimum(c_min, m)

        safe = c_min >= -HGRN_SAFE_LOG
        diag = []
        quad = []
        for d, (k, c, tot) in enumerate(per_dir):
            q32 = qs * jnp.exp(c)
            k32 = k * jnp.exp(tot - c)
            k_hat = jnp.where(safe, k * jnp.exp(-c), 0.0).astype(BF16)
            e32 = [jnp.exp(tot[j * cs:j * cs + 1]) for j in range(nch)]

            def span(lo, hi):
                out = None
                for j in range(lo, hi):
                    out = e32[j] if out is None else out * e32[j]
                return out

            def rescaled(x, facs):
                parts = [x[j * cs:(j + 1) * cs] if f is None else x[j * cs:(j + 1) * cs] * f
                         for j, f in enumerate(facs)]
                return jnp.concatenate(parts, axis=0).astype(BF16)

            def level(n):
                before = [span((j // n) * n, j) for j in range(nch)]
                after = [span(j + 1, (j // n + 1) * n) for j in range(nch)]
                qf, kf = (before, after) if d == 0 else (after, before)
                return rescaled(q32, qf), rescaled(k32, kf)

            q32b, k32b = q32.astype(BF16), k32.astype(BF16)
            q64, k64 = level(2)
            q128, k128 = level(4)
            q256, k256 = level(8)
            diag.append((_dot_nt(q32b, k_hat), _dot_nt(q32b, k32b), _dot_nt(q64, k64)))
            if d == 0:
                quad.append(_dot_nt(q128[half:], k128[:half]))
            else:
                quad.append(_dot_nt(q128[:half], k128[half:]))
            qe_ref[d, rows, :] = q256
            ke_ref[d, rows, :] = k256
            ee_ref[d, blk] = jnp.broadcast_to(span(0, nch), (8, DK_C))

        pieces = []
        for j in range(nch):
            rs = slice(j * cs, (j + 1) * cs)
            own = j * cs // half
            cols = slice(own * half, (own + 1) * half)
            acc = None
            for d in range(2):
                for i in range(3):
                    term = diag[d][i][rs, cols] * mask_ref[d, i, rs, :]
                    acc = term if acc is None else acc + term
            other = quad[0][rs.start - half:rs.stop - half] if own == 1 else quad[1][rs]
            pieces.append(jnp.concatenate([other, acc] if own == 1 else [acc, other], axis=1))
        att = jnp.concatenate(pieces, axis=0)
        oi_ref[rows, :] = _dot(att.astype(BF16), v)

        unsafe_ref[blk] = jnp.logical_not(safe).astype(jnp.int32)
        sq_ref[rows, :] = qs
        for d in range(2):
            sk_ref[d, rows, :] = per_dir[d][0]
            sc_ref[d, rows, :] = per_dir[d][1]
        return carry

    n_blk = GROUP_ROWS // sb
    lax.fori_loop(0, n_blk, block_step, 0, unroll=True)

    def vpu_block(blk, carry):
        @pl.when(unsafe_ref[blk] != 0)
        def _():
            def chunk_step(i, carry2):
                crow = pl.ds(pl.multiple_of(blk * sb + i * cs, cs), cs)
                q, vv = sq_ref[crow, :], i_ref[crow, :].astype(F32)
                oi_ref[crow, :] += (_hgrn_intra(q, sk_ref[0, crow, :], vv, sc_ref[0, crow, :], reverse=False)
                                    + _hgrn_intra(q, sk_ref[1, crow, :], vv, sc_ref[1, crow, :], reverse=True))
                return carry2

            lax.fori_loop(0, sb // cs, chunk_step, 0)

        return carry

    lax.fori_loop(0, n_blk, vpu_block, 0)

    is_ctx = pl.program_id(1) < n_ctx
    pl.when(is_ctx)(functools.partial(
        _hgrn_states, qe_ref, ke_ref, i_ref, ee_ref, oi_ref, t_ctx, st_ref=st_ref))
    pl.when(jnp.logical_not(is_ctx))(functools.partial(
        _hgrn_states, qe_ref, ke_ref, i_ref, ee_ref, oi_ref, t_lat, s0_ref=s0_ref))

    o = oi_ref[...]
    on = o * lax.rsqrt(jnp.mean(o * o, axis=-1, keepdims=True) + EPS) * gn_ref[...]
    o_ref[...] = (on * _silu(g_ref[...].astype(F32))).astype(o_ref.dtype)


def _hgrn(z16, z32, lower, gnorm, state_hgrn, oi, n_layers, prev_st, n_ctx_req, t_ctx, t_lat, n_rows):
    nh = N_HEADS_C
    n_ctx = n_ctx_req * t_ctx // GROUP_ROWS
    n_grp = n_rows // GROUP_ROWS
    rows = GROUP_ROWS
    assert t_lat == GROUP_ROWS and t_ctx % HGRN_SUPER == 0
    blk = lambda p: pl.BlockSpec((rows, DK_C), lambda h, g, p=p: (g, p * nh + h))
    st_spec, st_shape = _state_spec(n_ctx_req, n_layers, oi, (2, nh, DK_C, DK_C),
                                    (2, None, DK_C, DK_C), lambda h: (0, h, 0, 0), n_ctx, GROUP_ROWS // t_ctx)
    in_specs = [blk(0), blk(1), blk(2), blk(0), blk(1),
                pl.BlockSpec((2, DK_C), lambda h, g: (0, h)),
                pl.BlockSpec((1, DK_C), lambda h, g: (0, 0)),
                pl.BlockSpec((None, None, 2, None, DK_C, DK_C),
                             lambda h, g: (jnp.maximum(g - n_ctx, 0), oi, 0, h, 0, 0))]
    args = [z16, z16, z16, z32, z32, lower, gnorm, state_hgrn]
    aliases = {}
    if prev_st is not None:
        in_specs.append(pl.BlockSpec(memory_space=pl.ANY))
        args.append(prev_st)
        aliases = {len(args) - 1: 1}
    return pl.pallas_call(
        functools.partial(_hgrn_kernel, t_ctx=t_ctx, t_lat=t_lat, n_ctx=n_ctx),
        grid=(nh, n_grp),
        in_specs=in_specs,
        out_specs=[pl.BlockSpec((rows, DK_C), lambda h, g: (g, h)), st_spec],
        out_shape=[jax.ShapeDtypeStruct((n_rows, D_MODEL), BF16), st_shape],
        input_output_aliases=aliases,
        scratch_shapes=[
            pltpu.VMEM((2, rows, DK_C), BF16),
            pltpu.VMEM((2, rows, DK_C), BF16),
            pltpu.VMEM((2, rows // HGRN_SUPER, 8, DK_C), F32),
            pltpu.VMEM((rows, DK_C), F32),
            pltpu.VMEM((rows, DK_C), F32),
            pltpu.VMEM((2, rows, DK_C), F32),
            pltpu.VMEM((2, rows, DK_C), F32),
            pltpu.VMEM((2, 3, HGRN_SUPER, HGRN_SUPER // 2), F32),
            pltpu.SMEM((rows // HGRN_SUPER,), jnp.int32),
        ],
        compiler_params=_params(("arbitrary", "arbitrary"), 32),
        name="hgrn",
    )(*args)


def kernel(x_prompt, x_sample, cache_kv, state_ret, state_hgrn, c, c_ctx, w_mod, b_mod, norm_g,
           w_in_even, w_out_even, rpb, ret_decay, w_in_odd, w_out_odd, hgrn_lb, hgrn_gnorm,
           w_ffn_in, w_ffn_out):
    bp, tp, _ = x_prompt.shape
    bs, ts, _ = x_sample.shape
    np_rows, ns_rows = bp * tp, bs * ts
    n_rows = np_rows + ns_rows
    n_even, n_odd = w_in_even.shape[0], w_in_odd.shape[0]
    assert np_rows % GROUP_ROWS == 0 and ts == GROUP_ROWS
    xs = [x_prompt.reshape(np_rows, D_MODEL), x_sample.reshape(ns_rows, D_MODEL)]

    n_c = bs + 1
    pad = (-n_c) % 8
    cvec = jnp.concatenate([c, c_ctx[None], jnp.zeros((pad, D_MODEL), F32)], axis=0)
    mod_all = _modulation(cvec, w_mod, b_mod)
    grp = np.concatenate([np.full(np_rows // GROUP_ROWS, bs), np.arange(bs)]).astype(np.int32)
    mod_all = mod_all[:, grp].reshape(DEPTH, len(grp), 6, D_MODEL)
    mod_all = jnp.pad(mod_all, ((0, 0), (0, 0), (0, MOD_ROWS - 6), (0, 0)))

    p_lb = jax.nn.softmax(hgrn_lb.astype(F32), axis=0)
    lower = jnp.clip(jnp.cumsum(p_lb, axis=0) - p_lb[0], 0.0, 1.0)
    pair_bias = _na_bias_tables(rpb)
    rope_tabs = _rope_tables(ts)

    w_in_even_b, w_out_even_b = w_in_even.astype(BF16), w_out_even.astype(BF16)
    w_out_odd_b = w_out_odd.astype(BF16)
    w_in_odd_q = w_in_odd[:, :, :D_C].astype(BF16)
    w_in_odd_f = w_in_odd[:, :, D_C:3 * D_C].astype(BF16)
    w_in_odd_ig = w_in_odd[:, :, 3 * D_C:].astype(BF16)
    ffn_w = [_ffn_weights(w_ffn_in[l], w_ffn_out[l]) for l in range(DEPTH)]

    kv_out = ret_out = hg_out = None
    y_split = None
    for l in range(DEPTH):
        mod = mod_all[l]
        g = norm_g[l].reshape(4, 1, D_MODEL)
        if l % 2 == 0:
            e = l // 2
            z, = _in_proj(xs, g[0], mod, [([w_in_even_b[e]], BF16)])
            oa, kv_out = _attention(z, cache_kv, pair_bias, e, n_even, kv_out, bp, tp, n_rows)
            ob, ret_out = _retention(z, ret_decay[e], rope_tabs, state_ret, e, n_even, ret_out, bp, tp, ts, n_rows)
            x = _out_proj([oa, ob], w_out_even_b[e], xs, g[1], mod)
        else:
            oi = l // 2
            z16, z32 = _in_proj(xs, g[0], mod, [([w_in_odd_q[oi], w_in_odd_ig[oi]], BF16),
                                                ([w_in_odd_f[oi]], F32)])
            gn = hgrn_gnorm[oi].reshape(1, DK_C)
            o, hg_out = _hgrn(z16, z32, lower[oi], gn, state_hgrn, oi, n_odd, hg_out, bp, tp, ts, n_rows)
            x = _out_proj([o], w_out_odd_b[oi], xs, g[1], mod)
        if l == DEPTH - 1:
            y_split = _ffn(x, g[2], g[3], mod, *ffn_w[l], split=(x_prompt.shape, x_sample.shape))
        else:
            xs = [_ffn(x, g[2], g[3], mod, *ffn_w[l])]

    return (y_split[0], y_split[1], kv_out, ret_out, hg_out)
```

```python
import functools

import numpy as np
import jax
import jax.numpy as jnp
from jax import lax
from jax.experimental import pallas as pl
from jax.experimental.pallas import tpu as pltpu

F32 = jnp.float32
BF16 = jnp.bfloat16

D_MODEL = 1024
DEPTH = 4
GRID_W = 64
HEAD_DIM = 64
N_HEADS_A = 8
N_HEADS_B = 8
D_A = N_HEADS_A * HEAD_DIM
D_B = N_HEADS_B * HEAD_DIM
WIN_R = 8
WIN_C = 16
N_HEADS_C = 8
DK_C = D_MODEL // N_HEADS_C
D_C = N_HEADS_C * DK_C
D_FF = ((8 * D_MODEL // 3 + 255) // 256) * 256
ATTN_SCALE = HEAD_DIM ** -0.5
HGRN_CHUNK = 32
HGRN_SUPER = 256
HGRN_HEADS = 2
HGRN_SAFE_LOG = 75.0
ROPE_BASE = 10000.0
EPS = 1e-6
MASK_NEG = -1e30
F_MIN = 1e-30

GROUP_ROWS = 1024
MOD_ROWS = 8
LANES = 128
HEADS_PER_TILE = LANES // HEAD_DIM
PROJ_TN = 512
FFN_TF = 256
MOD_TN = 1536
MIB = 1024 * 1024

SH1, SC1, GT1, SH2, SC2, GT2 = range(6)

NA_QROWS = 4
NA_KROWS = 12
NA_KSTART = (0, 0, 4, 4)


def _params(sem, vmem_mib):
    return pltpu.CompilerParams(dimension_semantics=sem, vmem_limit_bytes=vmem_mib * MIB)


def _sigmoid(x):
    return 1.0 / (1.0 + jnp.exp(-x))


def _silu(x):
    return x * _sigmoid(x)


def _dot(a, b):
    return jnp.dot(a, b, preferred_element_type=F32)


def _dot_nt(a, b):
    return lax.dot_general(a, b, (((1,), (1,)), ((), ())), preferred_element_type=F32)


def _dot_tt(a, b):
    return lax.dot_general(a, b, (((0,), (1,)), ((), ())), preferred_element_type=F32)


def _dot_tn(a, b):
    return lax.dot_general(a, b, (((0,), (0,)), ((), ())), preferred_element_type=F32)


def _mod_kernel(c_ref, w_ref, b_ref, o_ref):
    s = _silu(c_ref[...]).astype(BF16)
    o_ref[...] = _dot(s, w_ref[...].astype(BF16)) + b_ref[...]


def _modulation(cvec, w_mod, b_mod):
    rows = cvec.shape[0]
    n = w_mod.shape[-1]
    return pl.pallas_call(
        _mod_kernel,
        grid=(DEPTH, n // MOD_TN),
        in_specs=[
            pl.BlockSpec((rows, D_MODEL), lambda l, j: (0, 0)),
            pl.BlockSpec((None, D_MODEL, MOD_TN), lambda l, j: (l, 0, j)),
            pl.BlockSpec((None, 1, MOD_TN), lambda l, j: (l, 0, j)),
        ],
        out_specs=pl.BlockSpec((None, rows, MOD_TN), lambda l, j: (l, 0, j)),
        out_shape=jax.ShapeDtypeStruct((DEPTH, rows, n), F32),
        compiler_params=_params(("arbitrary", "arbitrary"), 32),
        name="modulation",
    )(cvec, w_mod, b_mod.reshape(DEPTH, 1, n))


def _norm_mod(x, g, mod, sh_row, sc_row):
    y = x * lax.rsqrt(jnp.mean(x * x, axis=-1, keepdims=True) + EPS) * g
    return y * (1.0 + mod[sc_row:sc_row + 1]) + mod[sh_row:sh_row + 1]


def _stream_specs(xs):
    tm = GROUP_ROWS
    if len(xs) == 1:
        return [pl.BlockSpec((tm, D_MODEL), lambda i: (i, 0))]
    n_ctx = xs[0].shape[0] // tm
    return [pl.BlockSpec((tm, D_MODEL), lambda i: (jnp.minimum(i, n_ctx - 1), 0)),
            pl.BlockSpec((tm, D_MODEL), lambda i: (jnp.maximum(i - n_ctx, 0), 0))]


def _stream_tile(x_refs, n_ctx):
    if len(x_refs) == 1:
        return x_refs[0][...]
    return jnp.where(pl.program_id(0) < n_ctx, x_refs[0][...], x_refs[1][...])


def _in_proj_kernel(*refs, n_x, n_ctx, n_w):
    x_refs, (g_ref, mod_ref) = refs[:n_x], refs[n_x:n_x + 2]
    w_refs, o_refs = refs[n_x + 2:n_x + 2 + sum(n_w)], refs[n_x + 2 + sum(n_w):]
    h = _norm_mod(_stream_tile(x_refs, n_ctx), g_ref[...], mod_ref[...], SH1, SC1).astype(BF16)
    w_iter = iter(w_refs)
    for o_ref, count in zip(o_refs, n_w):
        col0 = 0
        for w_ref in (next(w_iter) for _ in range(count)):
            for j in range(w_ref.shape[1] // PROJ_TN):
                cols = slice(j * PROJ_TN, (j + 1) * PROJ_TN)
                o_ref[:, col0 + j * PROJ_TN:col0 + (j + 1) * PROJ_TN] = _dot(h, w_ref[:, cols]).astype(o_ref.dtype)
            col0 += w_ref.shape[1]


def _resident(block_shape):
    return pl.BlockSpec(block_shape, lambda *_: (0,) * len(block_shape), pipeline_mode=pl.Buffered(1))


def _in_proj(xs, g, mod, outputs):
    m = sum(x.shape[0] for x in xs)
    tm = GROUP_ROWS
    ws = [w for pieces, _ in outputs for w in pieces]
    ns = [sum(w.shape[1] for w in pieces) for pieces, _ in outputs]
    assert all(w.shape[1] % PROJ_TN == 0 for w in ws)
    return pl.pallas_call(
        functools.partial(_in_proj_kernel, n_x=len(xs), n_ctx=xs[0].shape[0] // tm,
                          n_w=tuple(len(pieces) for pieces, _ in outputs)),
        grid=(m // tm,),
        in_specs=_stream_specs(xs) + [
            pl.BlockSpec((1, D_MODEL), lambda i: (0, 0)),
            pl.BlockSpec((None, MOD_ROWS, D_MODEL), lambda i: (i, 0, 0)),
        ] + [_resident(w.shape) for w in ws],
        out_specs=[pl.BlockSpec((tm, n), lambda i: (i, 0)) for n in ns],
        out_shape=[jax.ShapeDtypeStruct((m, n), dt) for n, (_, dt) in zip(ns, outputs)],
        compiler_params=_params(("arbitrary",), 56),
        name="in_proj",
    )(*xs, g, mod, *ws)


def _out_proj_kernel(*refs, n_in, n_x, n_ctx):
    a_refs, w_refs = refs[:n_in], refs[n_in:2 * n_in]
    x_refs = refs[2 * n_in:2 * n_in + n_x]
    g_ref, mod_ref, o_ref = refs[2 * n_in + n_x:]
    y = _dot(a_refs[0][...], w_refs[0][...])
    for a_ref, w_ref in zip(a_refs[1:], w_refs[1:]):
        y = y + _dot(a_ref[...], w_ref[...])
    yn = y * lax.rsqrt(jnp.mean(y * y, axis=-1, keepdims=True) + EPS) * g_ref[...]
    o_ref[...] = _stream_tile(x_refs, n_ctx) + mod_ref[GT1:GT1 + 1, :] * yn


def _out_proj(acts, w, xs, g, mod):
    m = sum(x.shape[0] for x in xs)
    tm = GROUP_ROWS
    n_in = len(acts)
    ks = [a.shape[1] for a in acts]
    assert sum(ks) == w.shape[0] and len(set(ks)) == 1
    in_specs = [pl.BlockSpec((tm, k), lambda i: (i, 0)) for k in ks]
    in_specs += [pl.BlockSpec((ks[0], D_MODEL), lambda i, p=p: (p, 0)) for p in range(n_in)]
    in_specs += _stream_specs(xs) + [
        pl.BlockSpec((1, D_MODEL), lambda i: (0, 0)),
        pl.BlockSpec((None, MOD_ROWS, D_MODEL), lambda i: (i, 0, 0)),
    ]
    return pl.pallas_call(
        functools.partial(_out_proj_kernel, n_in=n_in, n_x=len(xs), n_ctx=xs[0].shape[0] // tm),
        grid=(m // tm,),
        in_specs=in_specs,
        out_specs=pl.BlockSpec((tm, D_MODEL), lambda i: (i, 0)),
        out_shape=jax.ShapeDtypeStruct((m, D_MODEL), F32),
        compiler_params=_params(("arbitrary",), 48),
        name="out_proj",
    )(*acts, *([w] * n_in), *xs, g, mod)


def _ffn_kernel(x_ref, xn_ref, g2_ref, g3_ref, mod_ref, modn_ref, win_ref, wo_ref, *rest, n_ctx_groups):
    o_refs, (h_ref, acc_ref) = rest[:-2], rest[-2:]
    tm = x_ref.shape[0]
    nf = wo_ref.shape[0]
    ahead = -(-tm // (nf * 16)) * 16
    i = pl.program_id(0)
    cur, nxt = i % 2, (i + 1) % 2

    @pl.when(i == 0)
    def _():
        h_ref[0] = _norm_mod(x_ref[...], g2_ref[...], mod_ref[...], SH2, SC2).astype(BF16)

    acc_ref[...] = jnp.zeros_like(acc_ref)

    def hidden_step(f, carry):
        h = h_ref[cur]
        a = _dot(h, win_ref[:, pl.ds(pl.multiple_of(f * FFN_TF, FFN_TF), FFN_TF)])
        u = _dot(h, win_ref[:, pl.ds(pl.multiple_of(D_FF + f * FFN_TF, FFN_TF), FFN_TF)])
        acc_ref[...] += _dot((_silu(a) * u).astype(BF16), wo_ref[f])
        r0 = pl.multiple_of(jnp.minimum(f * ahead, tm - ahead), 16)
        rows = pl.ds(r0, ahead)
        h_ref[nxt, rows, :] = _norm_mod(xn_ref[rows, :], g2_ref[...], modn_ref[...], SH2, SC2).astype(BF16)
        return carry

    lax.fori_loop(0, nf, hidden_step, 0, unroll=True)

    def finish(o_ref):
        y = acc_ref[...]
        yn = y * lax.rsqrt(jnp.mean(y * y, axis=-1, keepdims=True) + EPS) * g3_ref[...]
        o_ref[...] = (x_ref[...] + mod_ref[GT2:GT2 + 1, :] * yn).reshape(o_ref.shape)

    if len(o_refs) == 1:
        finish(o_refs[0])
    else:
        is_ctx = pl.program_id(0) < n_ctx_groups
        pl.when(is_ctx)(functools.partial(finish, o_refs[0]))
        pl.when(jnp.logical_not(is_ctx))(functools.partial(finish, o_refs[1]))


def _ffn_weights(w_in, w_out):
    return w_in.astype(BF16), w_out.astype(BF16).reshape(D_FF // FFN_TF, FFN_TF, D_MODEL)


def _ffn(x, g2, g3, mod, w_in, wo, split=None):
    m = x.shape[0]
    tm = GROUP_ROWS
    n_ctx = 0
    if split is None:
        out_specs = pl.BlockSpec((tm, D_MODEL), lambda i: (i, 0))
        out_shape = jax.ShapeDtypeStruct((m, D_MODEL), F32)
    else:
        (bp, tp, _), (bs, ts, _) = split
        n_ctx = bp * tp // tm
        assert ts == tm and (bp * tp) % tm == 0
        out_specs = [
            pl.BlockSpec((tm // tp, tp, D_MODEL), lambda i: (jnp.minimum(i, n_ctx - 1), 0, 0)),
            pl.BlockSpec((1, ts, D_MODEL), lambda i: (jnp.maximum(i - n_ctx, 0), 0, 0)),
        ]
        out_shape = [jax.ShapeDtypeStruct(s, F32) for s in split]
    last = m // tm - 1
    return pl.pallas_call(
        functools.partial(_ffn_kernel, n_ctx_groups=n_ctx),
        grid=(m // tm,),
        in_specs=[
            pl.BlockSpec((tm, D_MODEL), lambda i: (i, 0)),
            pl.BlockSpec((tm, D_MODEL), lambda i: (jnp.minimum(i + 1, last), 0)),
            pl.BlockSpec((1, D_MODEL), lambda i: (0, 0)),
            pl.BlockSpec((1, D_MODEL), lambda i: (0, 0)),
            pl.BlockSpec((None, MOD_ROWS, D_MODEL), lambda i: (i, 0, 0)),
            pl.BlockSpec((None, MOD_ROWS, D_MODEL), lambda i: (jnp.minimum(i + 1, last), 0, 0)),
            _resident(w_in.shape), _resident(wo.shape),
        ],
        out_specs=out_specs,
        out_shape=out_shape,
        scratch_shapes=[pltpu.VMEM((2, tm, D_MODEL), BF16), pltpu.VMEM((tm, D_MODEL), F32)],
        compiler_params=_params(("arbitrary",), 60),
        name="ffn",
    )(x, x, g2, g3, mod, mod, w_in, wo)


def _state_spec(n_req, n_layers, layer, tail, tail_block, tail_index, n_ctx, nb):
    n_own = n_layers if layer == 0 else 1
    first = 0 if layer == 0 else layer

    def index(j, g):
        return (jnp.minimum(g, n_ctx - 1), first) + tail_index(j)

    spec = pl.BlockSpec((nb, n_own) + tail_block, index)
    shape = jax.ShapeDtypeStruct((n_req, n_layers) + tail, F32)
    return spec, shape


def _zero_other_layers(st_ref):
    if st_ref.shape[1] > 1:
        st_ref[:, 1:] = jnp.zeros((st_ref.shape[0], st_ref.shape[1] - 1) + st_ref.shape[2:], st_ref.dtype)


NA_NRO = 2 * WIN_R - 1


def _na_bias_tables(rpb):
    qc = np.arange(GRID_W)[:, None]
    kc = np.arange(GRID_W)[None, :]
    win0 = np.clip(qc - WIN_C // 2, 0, GRID_W - WIN_C)
    col_valid = (kc >= win0) & (kc < win0 + WIN_C)
    col_off = np.clip(kc - qc, 1 - WIN_C, WIN_C - 1) + (WIN_C - 1)
    onehot = (col_off[None] == np.arange(2 * WIN_C - 1)[:, None, None]).astype(np.float32)
    tiles = jnp.einsum('ehrc,cqk->ehrqk', rpb.astype(F32), jnp.asarray(onehot), precision=lax.Precision.HIGHEST)
    tiles = jnp.where(col_valid[None, None, None], tiles, MASK_NEG)
    nxt = jnp.concatenate([tiles[:, :, 1:], tiles[:, :, -1:]], axis=2)
    return jnp.concatenate([tiles, nxt], axis=-1)


def _na_group_bias(pair_ref, hh, g):
    rows = GROUP_ROWS // GRID_W
    first_half = lax.broadcasted_iota(jnp.int32, (GRID_W, 2 * GRID_W), 1) < GRID_W
    bias = []
    for a in range(NA_QROWS):
        r = g * NA_QROWS + a
        row0 = min(max(r - WIN_R // 2, 0), rows - WIN_R)
        pieces = []
        for w in range(0, NA_KROWS, 2):
            kr = NA_KSTART[g] + w
            ro = kr - r + (WIN_R - 1)
            in0 = row0 <= kr < row0 + WIN_R
            in1 = row0 <= kr + 1 < row0 + WIN_R
            assert not (in0 or in1) or 0 <= ro < NA_NRO
            if in0 and in1:
                piece = pair_ref[hh, ro]
            elif in0:
                piece = jnp.where(first_half, pair_ref[hh, ro], MASK_NEG)
            elif in1:
                piece = jnp.where(first_half, MASK_NEG, pair_ref[hh, ro])
            else:
                piece = jnp.full((GRID_W, 2 * GRID_W), MASK_NEG, F32)
            pieces.append(piece)
        bias.append(jnp.concatenate(pieces, axis=1))
    return jnp.concatenate(bias, axis=0)


def _ctx_attend(q_ref, k_ref, v_ref, o_ref, kv_ref, t):
    for r in range(GROUP_ROWS // t):
        rs = slice(r * t, (r + 1) * t)
        q2, k2, v2 = q_ref[rs, :] * ATTN_SCALE, k_ref[rs, :], v_ref[rs, :]
        head_of_lane = lax.broadcasted_iota(jnp.int32, q2.shape, 1) // HEAD_DIM
        out = None
        for hh in range(HEADS_PER_TILE):
            hs = slice(hh * HEAD_DIM, (hh + 1) * HEAD_DIM)
            kv_ref[r, 0, 0, hh] = k2[:, hs].astype(F32)
            kv_ref[r, 0, 1, hh] = v2[:, hs].astype(F32)
            in_head = head_of_lane == hh
            s = _dot_nt(jnp.where(in_head, q2, jnp.zeros_like(q2)), k2)
            p = jnp.exp(s - jnp.max(s, axis=-1, keepdims=True))
            l = jnp.sum(p, axis=-1, keepdims=True)
            o = _dot(p.astype(BF16), v2) / l
            out = o if out is None else jnp.where(in_head, o, out)
        o_ref[rs, :] = out.astype(o_ref.dtype)
    _zero_other_layers(kv_ref)


def _na_attend(q_ref, k_ref, v_ref, ckv_ref, pair_ref, o_ref):
    q2, k2, v2 = q_ref[...] * ATTN_SCALE, k_ref[...], v_ref[...]
    tq = NA_QROWS * GRID_W
    tk = NA_KROWS * GRID_W
    head_of_lane = lax.broadcasted_iota(jnp.int32, q2.shape, 1) // HEAD_DIM
    kc2 = jnp.concatenate([ckv_ref[0, hh] for hh in range(HEADS_PER_TILE)], axis=1).astype(BF16)
    vc2 = jnp.concatenate([ckv_ref[1, hh] for hh in range(HEADS_PER_TILE)], axis=1).astype(BF16)
    keys = [jnp.concatenate([k2[s0 * GRID_W:s0 * GRID_W + tk], kc2], axis=0) for s0 in NA_KSTART]
    vals = [jnp.concatenate([v2[s0 * GRID_W:s0 * GRID_W + tk], vc2], axis=0) for s0 in NA_KSTART]
    no_bias = jnp.zeros((tq, kc2.shape[0]), F32)
    out = None
    for hh in range(HEADS_PER_TILE):
        in_head = head_of_lane == hh
        q = jnp.where(in_head, q2, jnp.zeros_like(q2))
        rows = []
        for g in range(len(NA_KSTART)):
            bias = jnp.concatenate([_na_group_bias(pair_ref, hh, g), no_bias], axis=1)
            s = _dot_nt(q[g * tq:(g + 1) * tq], keys[g]) + bias
            p = jnp.exp(s - jnp.max(s, axis=-1, keepdims=True))
            l = jnp.sum(p, axis=-1, keepdims=True)
            rows.append(_dot_tt(vals[g], p.astype(BF16)).T / l)
        o_h = jnp.concatenate(rows, axis=0)
        out = o_h if out is None else jnp.where(in_head, o_h, out)
    o_ref[...] = out.astype(o_ref.dtype)


def _attn_kernel(q_ref, k_ref, v_ref, ckv_ref, pair_ref, *rest, t_ctx, n_ctx):
    o_ref, kv_ref = rest[-2:]
    is_ctx = pl.program_id(1) < n_ctx
    pl.when(is_ctx)(functools.partial(_ctx_attend, q_ref, k_ref, v_ref, o_ref, kv_ref, t_ctx))
    pl.when(jnp.logical_not(is_ctx))(
        functools.partial(_na_attend, q_ref, k_ref, v_ref, ckv_ref, pair_ref, o_ref))


def _attention(z, cache_kv, pair_bias, e, n_layers, prev_kv, n_ctx_req, t_ctx, n_rows):
    npair = D_A // LANES
    hp = HEADS_PER_TILE
    n_ctx = n_ctx_req * t_ctx // GROUP_ROWS
    n_grp = n_rows // GROUP_ROWS
    past = cache_kv.shape[-2]
    blk = lambda p: pl.BlockSpec((GROUP_ROWS, LANES), lambda j, g, p=p: (g, p * npair + j))
    kv_spec, kv_shape = _state_spec(n_ctx_req, n_layers, e, (2, N_HEADS_A, t_ctx, HEAD_DIM),
                                    (2, hp, t_ctx, HEAD_DIM), lambda j: (0, j, 0, 0), n_ctx, GROUP_ROWS // t_ctx)
    in_specs = [
        blk(0), blk(1), blk(2),
        pl.BlockSpec((None, None, 2, hp, past, HEAD_DIM), lambda j, g: (jnp.maximum(g - n_ctx, 0), e, 0, j, 0, 0)),
        pl.BlockSpec((None, hp) + pair_bias.shape[2:], lambda j, g: (e, j, 0, 0, 0)),
    ]
    args = [z, z, z, cache_kv, pair_bias]
    aliases = {}
    if prev_kv is not None:
        in_specs.append(pl.BlockSpec(memory_space=pl.ANY))
        args.append(prev_kv)
        aliases = {len(args) - 1: 1}
    return pl.pallas_call(
        functools.partial(_attn_kernel, t_ctx=t_ctx, n_ctx=n_ctx),
        grid=(npair, n_grp),
        in_specs=in_specs,
        out_specs=[pl.BlockSpec((GROUP_ROWS, LANES), lambda j, g: (g, j)), kv_spec],
        out_shape=[jax.ShapeDtypeStruct((n_rows, D_A), BF16), kv_shape],
        input_output_aliases=aliases,
        compiler_params=_params(("arbitrary", "arbitrary"), 56),
        name="attention",
    )(*args)


def _rope_tables(t):
    half = HEAD_DIM // 2
    nf = half // 2
    inv = ROPE_BASE ** (-np.arange(nf, dtype=np.float32) / nf)
    pos = np.arange(t)
    ang_r = (pos // GRID_W).astype(np.float32)[:, None] * inv[None, :]
    ang_c = (pos % GRID_W).astype(np.float32)[:, None] * inv[None, :]
    ang_r, ang_c = jnp.asarray(ang_r), jnp.asarray(ang_c)
    cr, sr, cc, sc = jnp.cos(ang_r), jnp.sin(ang_r), jnp.cos(ang_c), jnp.sin(ang_c)
    cos = jnp.concatenate([cr, cr, cc, cc], axis=-1)
    sin = jnp.concatenate([-sr, sr, -sc, sc], axis=-1)
    return jnp.tile(cos, (1, HEADS_PER_TILE)), jnp.tile(sin, (1, HEADS_PER_TILE))


def _rope(x, cos, sin):
    nf = HEAD_DIM // 4
    lane = lax.broadcasted_iota(jnp.int32, x.shape, 1)
    partner = jnp.where(lane % (2 * nf) < nf, pltpu.roll(x, LANES - nf, axis=1), pltpu.roll(x, nf, axis=1))
    return x * cos + partner * sin


def _decay_matrix(dm_ref, lg, t):
    n_i = lax.broadcasted_iota(jnp.int32, (t, t), 0)
    m_i = lax.broadcasted_iota(jnp.int32, (t, t), 1)
    diff = (n_i - m_i).astype(F32)
    dist = jnp.abs(diff)
    for hh in range(HEADS_PER_TILE):
        one_sided = jnp.exp(jnp.where(diff >= 0, lg[hh][0], lg[hh][1]) * dist)
        dm_ref[hh] = jnp.where(diff == 0, 2.0, one_sided)


def _head_rms_gate(o, g, head_of_lane):
    sq = o * o
    ms = sum(jnp.where(head_of_lane == hh,
                       jnp.sum(jnp.where(head_of_lane == hh, sq, 0.0), axis=-1, keepdims=True), 0.0)
             for hh in range(HEADS_PER_TILE)) * (1.0 / HEAD_DIM)
    return o * lax.rsqrt(ms + EPS) * _silu(g)


def _two_sided_scores(q, kb, vb, dm_ref, head_of_lane):
    o = None
    for hh in range(HEADS_PER_TILE):
        in_head = head_of_lane == hh
        att = _dot_nt(jnp.where(in_head, q, 0.0).astype(BF16), kb) * dm_ref[hh]
        o_h = _dot(att.astype(BF16), vb)
        o = o_h if o is None else jnp.where(in_head, o_h, o)
    return o


def _retain(q_ref, k_ref, v_ref, g_ref, o_ref, st_ref, dm_ref, lg, t):
    pos = lax.broadcasted_iota(jnp.int32, (t, 1), 0).astype(F32)
    head_of_lane = lax.broadcasted_iota(jnp.int32, (1, LANES), 1) // HEAD_DIM
    lgv = [sum(jnp.where(head_of_lane == hh, lg[hh][d], 0.0) for hh in range(HEADS_PER_TILE)) for d in range(2)]
    for r in range(GROUP_ROWS // t):
        rs = slice(r * t, (r + 1) * t)
        q2 = q_ref[rs, :].astype(F32)
        k2 = k_ref[rs, :].astype(F32) * (HEAD_DIM ** -0.5)
        v2 = v_ref[rs, :]
        st = (_dot_tn((k2 * jnp.exp(lgv[0] * (t - 1.0 - pos))).astype(BF16), v2),
              _dot_tn((k2 * jnp.exp(lgv[1] * pos)).astype(BF16), v2))
        for hh in range(HEADS_PER_TILE):
            hs = slice(hh * HEAD_DIM, (hh + 1) * HEAD_DIM)
            st_ref[r, 0, 0, hh] = st[0][hs, hs]
            st_ref[r, 0, 1, hh] = st[1][hs, hs]
        o = _two_sided_scores(q2, k2.astype(BF16), v2, dm_ref, head_of_lane)
        o_ref[rs, :] = _head_rms_gate(o, g_ref[rs, :].astype(F32), head_of_lane).astype(o_ref.dtype)
    _zero_other_layers(st_ref)


def _retain_chunked(q_ref, k_ref, v_ref, g_ref, o_ref, dm_ref, lg, t, c, rope, s0_ref):
    n_c = t // c
    pos = lax.broadcasted_iota(jnp.int32, (c, 1), 0).astype(F32)
    head_of_lane = lax.broadcasted_iota(jnp.int32, (1, LANES), 1) // HEAD_DIM
    head_of_row = lax.broadcasted_iota(jnp.int32, (LANES, 1), 0) // HEAD_DIM
    same_head = head_of_row == head_of_lane
    lgv = [sum(jnp.where(head_of_lane == hh, lg[hh][d], 0.0) for hh in range(HEADS_PER_TILE)) for d in range(2)]
    lgc = [sum(jnp.where(head_of_row == hh, lg[hh][d], 0.0) for hh in range(HEADS_PER_TILE)) for d in range(2)]
    chunk_decay = [jnp.exp(lgc[d] * float(c)) for d in range(2)]
    q_dec = (jnp.exp(lgv[0] * (pos + 1.0)), jnp.exp(lgv[1] * (c - pos)))
    k_dec = (jnp.exp(lgv[0] * (c - 1.0 - pos)), jnp.exp(lgv[1] * pos))

    q2 = _rope(q_ref[...].astype(F32), rope[0][...], rope[1][...])
    k2 = _rope(k_ref[...].astype(F32) * (HEAD_DIM ** -0.5), rope[0][...], rope[1][...])
    cs = [slice(i * c, (i + 1) * c) for i in range(n_c)]

    def initial(d):
        zero = jnp.zeros((HEAD_DIM, HEAD_DIM), F32)
        rows = [jnp.concatenate([s0_ref[d, hh] if col == hh else zero for col in range(HEADS_PER_TILE)], axis=1)
                for hh in range(HEADS_PER_TILE)]
        return jnp.concatenate(rows, axis=0)

    incr = [[jnp.where(same_head, _dot_tn((k2[s] * k_dec[d]).astype(BF16), v_ref[s, :]), 0.0) for s in cs]
            for d in range(2)]
    fwd = [initial(0)]
    for i in range(n_c - 1):
        fwd.append(fwd[i] * chunk_decay[0] + incr[0][i])
    bwd = [initial(1)]
    for i in reversed(range(1, n_c)):
        bwd.insert(0, bwd[0] * chunk_decay[1] + incr[1][i])

    for i, s in enumerate(cs):
        qc = q2[s]
        o = _two_sided_scores(qc, k2[s].astype(BF16), v_ref[s, :], dm_ref, head_of_lane)
        o = o + _dot((qc * q_dec[0]).astype(BF16), fwd[i].astype(BF16))
        o = o + _dot((qc * q_dec[1]).astype(BF16), bwd[i].astype(BF16))
        o_ref[s, :] = _head_rms_gate(o, g_ref[s, :].astype(F32), head_of_lane).astype(o_ref.dtype)


def _ret_kernel(dec_ref, q_ref, k_ref, v_ref, g_ref, cos_ref, sin_ref, s0_ref, *rest, t_ctx, t_lat, n_ctx):
    o_ref, st_ref, dm_ref = rest[-3:]
    j, g = pl.program_id(0), pl.program_id(1)
    lg = [[-jnp.exp(jnp.full((1, 1), dec_ref[d, j * HEADS_PER_TILE + hh], F32)) for d in range(2)]
          for hh in range(HEADS_PER_TILE)]
    pl.when(g == 0)(functools.partial(_decay_matrix, dm_ref, lg, t_ctx))
    is_ctx = g < n_ctx
    pl.when(is_ctx)(functools.partial(_retain, q_ref, k_ref, v_ref, g_ref, o_ref, st_ref, dm_ref, lg, t_ctx))
    pl.when(jnp.logical_not(is_ctx))(functools.partial(
        _retain_chunked, q_ref, k_ref, v_ref, g_ref, o_ref, dm_ref, lg, t_lat, t_ctx, (cos_ref, sin_ref), s0_ref))


def _retention(z, ret_decay, rope_tabs, state_ret, e, n_layers, prev_st, n_ctx_req, t_ctx, t_lat, n_rows):
    npair = D_B // LANES
    hp = HEADS_PER_TILE
    sec0 = 3 * D_A // LANES
    n_ctx = n_ctx_req * t_ctx // GROUP_ROWS
    n_grp = n_rows // GROUP_ROWS
    assert t_lat == GROUP_ROWS
    blk = lambda p: pl.BlockSpec((GROUP_ROWS, LANES), lambda j, g, p=p: (g, sec0 + p * npair + j))
    st_spec, st_shape = _state_spec(n_ctx_req, n_layers, e, (2, N_HEADS_B, HEAD_DIM, HEAD_DIM),
                                    (2, hp, HEAD_DIM, HEAD_DIM), lambda j: (0, j, 0, 0), n_ctx,
                                    GROUP_ROWS // t_ctx)
    in_specs = [
        pl.BlockSpec(memory_space=pltpu.SMEM), blk(0), blk(1), blk(2), blk(3),
        pl.BlockSpec((t_lat, LANES), lambda j, g: (0, 0)),
        pl.BlockSpec((t_lat, LANES), lambda j, g: (0, 0)),
        pl.BlockSpec((None, None, 2, hp, HEAD_DIM, HEAD_DIM),
                     lambda j, g: (jnp.maximum(g - n_ctx, 0), e, 0, j, 0, 0)),
    ]
    args = [ret_decay, z, z, z, z, rope_tabs[0], rope_tabs[1], state_ret]
    aliases = {}
    if prev_st is not None:
        in_specs.append(pl.BlockSpec(memory_space=pl.ANY))
        args.append(prev_st)
        aliases = {len(args) - 1: 1}
    return pl.pallas_call(
        functools.partial(_ret_kernel, t_ctx=t_ctx, t_lat=t_lat, n_ctx=n_ctx),
        grid=(npair, n_grp),
        in_specs=in_specs,
        out_specs=[pl.BlockSpec((GROUP_ROWS, LANES), lambda j, g: (g, j)), st_spec],
        out_shape=[jax.ShapeDtypeStruct((n_rows, D_B), BF16), st_shape],
        input_output_aliases=aliases,
        scratch_shapes=[pltpu.VMEM((hp, t_ctx, t_ctx), F32)],
        compiler_params=_params(("arbitrary", "arbitrary"), 56),
        name="retention",
    )(*args)


def _seg_scan(x, reverse):
    t = x.shape[0]
    row = lax.broadcasted_iota(jnp.int32, (t, 1), 0) % HGRN_CHUNK
    sft = 1
    while sft < HGRN_CHUNK:
        if reverse:
            x = x + jnp.where(row < HGRN_CHUNK - sft, pltpu.roll(x, t - sft, axis=0), 0.0)
        else:
            x = x + jnp.where(row >= sft, pltpu.roll(x, sft, axis=0), 0.0)
        sft *= 2
    return x


def _hgrn_intra(q, k, v, c, reverse):
    cs = HGRN_CHUNK
    sub = 8
    row = lax.broadcasted_iota(jnp.int32, (cs, 1), 0)
    parts = [jnp.zeros((sub, v.shape[1]), F32) for _ in range(cs // sub)]
    for s in range(cs):
        blk_s = s // sub
        blks = range(0, blk_s + 1) if reverse else range(blk_s, cs // sub)
        c_s, k_s, v_s = c[s:s + 1], k[s:s + 1], v[s:s + 1]
        for bt in blks:
            rs = slice(bt * sub, (bt + 1) * sub)
            w = q[rs] * k_s * jnp.exp(jnp.minimum(c[rs] - c_s, 0.0))
            col = jnp.sum(w, axis=-1, keepdims=True)
            if bt == blk_s:
                keep = (row[rs] <= s) if reverse else (row[rs] >= s)
                col = jnp.where(keep, col, 0.0)
            parts[bt] = parts[bt] + col * v_s
    return jnp.concatenate(parts, axis=0)


def _hgrn_states(qe_ref, ke_ref, v_ref, ee_ref, oi_ref, t, s0_ref=None, st_ref=None):
    sb = HGRN_SUPER
    n_sb = t // sb
    for r in range(GROUP_ROWS // t):
        incr = []
        for j in range(n_sb):
            rows = slice((r * n_sb + j) * sb, (r * n_sb + j + 1) * sb)
            keys = jnp.concatenate([ke_ref[0, rows, :], ke_ref[1, rows, :]], axis=1)
            incr.append(_dot_tn(keys, v_ref[rows, :]))
        for d in range(2):
            st = s0_ref[d] if s0_ref is not None else None
            order = range(n_sb) if d == 0 else reversed(range(n_sb))
            for j in order:
                blk = r * n_sb + j
                rows = slice(blk * sb, (blk + 1) * sb)
                upd = incr[j][d * DK_C:(d + 1) * DK_C]
                if st is None:
                    st = upd
                else:
                    oi_ref[rows, :] += _dot(qe_ref[d, rows, :], st.astype(BF16))
                    decay = jnp.broadcast_to(ee_ref[d, blk][0:1], (DK_C, DK_C)).T
                    st = st * decay + upd
            if st_ref is not None:
                st_ref[r, 0, d] = st
    if st_ref is not None:
        _zero_other_layers(st_ref)


def _hgrn_kernel(q_ref, i_ref, g_ref, ff_ref, fb_ref, lb_ref, gn_ref, s0_ref, *rest, t_ctx, t_lat, n_ctx):
    o_ref, st_ref = rest[-11:-9]
    scratch = rest[-9:]
    mask_ref = scratch[-2]
    sb = HGRN_SUPER
    half = sb // 2

    @pl.when((pl.program_id(0) == 0) & (pl.program_id(1) == 0))
    def _():
        t = lax.broadcasted_iota(jnp.int32, (sb, half), 0)
        s = lax.broadcasted_iota(jnp.int32, (sb, half), 1) + ((t >> 7) << 7)
        same32 = (t >> 5) == (s >> 5)
        same64 = (t >> 6) == (s >> 6)
        for d, (incl, strict) in enumerate(((s <= t, s < t), (s >= t, s > t))):
            for i, m in enumerate((same32 & incl, same64 & jnp.logical_not(same32) & strict,
                                   jnp.logical_not(same64) & strict)):
                mask_ref[d, i] = jnp.where(m, 1.0, 0.0)

    for hd in range(HGRN_HEADS):
        ls = slice(hd * DK_C, (hd + 1) * DK_C)
        _hgrn_head(q_ref.at[:, ls], i_ref.at[:, ls], g_ref.at[:, ls], ff_ref.at[:, ls], fb_ref.at[:, ls],
                   lb_ref.at[:, ls], gn_ref, s0_ref.at[:, hd], o_ref.at[:, ls], st_ref.at[:, :, :, hd],
                   *scratch, t_ctx=t_ctx, t_lat=t_lat, n_ctx=n_ctx)


def _hgrn_head(q_ref, i_ref, g_ref, ff_ref, fb_ref, lb_ref, gn_ref, s0_ref, o_ref, st_ref,
               qe_ref, ke_ref, ee_ref, oi_ref, sq_ref, sk_ref, sc_ref, mask_ref, unsafe_ref,
               *, t_ctx, t_lat, n_ctx):
    cs = HGRN_CHUNK
    sb = HGRN_SUPER
    half = sb // 2
    nch = sb // cs

    def block_step(blk, carry):
        rows = pl.ds(pl.multiple_of(blk * sb, sb), sb)
        qs = _silu(q_ref[rows, :].astype(F32)) * (DK_C ** -0.5)
        v = i_ref[rows, :]
        per_dir = []
        c_min = None
        for d, f_ref in enumerate((ff_ref, fb_ref)):
            lb = lb_ref[d:d + 1, :]
            fr = f_ref[rows, :]
            e = jnp.exp(-jnp.abs(fr))
            r = 1.0 / (1.0 + e)
            sig_pos = jnp.where(fr >= 0, r, e * r)
            sig_neg = jnp.where(fr >= 0, e * r, r)
            f = lb + (1.0 - lb) * sig_pos
            k = (1.0 - lb) * sig_neg
            c = _seg_scan(jnp.log(jnp.maximum(f, F_MIN)), reverse=(d == 1))
            c3 = c.reshape(sb // cs, cs, DK_C)
            c_end = c3[:, cs - 1:cs, :] if d == 0 else c3[:, 0:1, :]
            tot = jnp.broadcast_to(c_end, c3.shape).reshape(sb, DK_C)
            per_dir.append((k, c, tot))
            m = jnp.min(c)
            c_min = m if c_min is None else jnp.minimum(c_min, m)

        safe = c_min >= -HGRN_SAFE_LOG
        diag = []
        quad = []
        for d, (k, c, tot) in enumerate(per_dir):
            q32 = qs * jnp.exp(c)
            k32 = k * jnp.exp(tot - c)
            k_hat = jnp.where(safe, k * jnp.exp(-c), 0.0).astype(BF16)
            e32 = [jnp.exp(tot[j * cs:j * cs + 1]) for j in range(nch)]

            def span(lo, hi):
                out = None
                for j in range(lo, hi):
                    out = e32[j] if out is None else out * e32[j]
                return out

            def rescaled(x, facs):
                parts = [x[j * cs:(j + 1) * cs] if f is None else x[j * cs:(j + 1) * cs] * f
                         for j, f in enumerate(facs)]
                return jnp.concatenate(parts, axis=0).astype(BF16)

            def level(n):
                before = [span((j // n) * n, j) for j in range(nch)]
                after = [span(j + 1, (j // n + 1) * n) for j in range(nch)]
                qf, kf = (before, after) if d == 0 else (after, before)
                return rescaled(q32, qf), rescaled(k32, kf)

            q32b, k32b = q32.astype(BF16), k32.astype(BF16)
            q64, k64 = level(2)
            q128, k128 = level(4)
            q256, k256 = level(8)
            diag.append((_dot_nt(q32b, k_hat), _dot_nt(q32b, k32b), _dot_nt(q64, k64)))
            if d == 0:
                quad.append(_dot_nt(q128[half:], k128[:half]))
            else:
                quad.append(_dot_nt(q128[:half], k128[half:]))
            qe_ref[d, rows, :] = q256
            ke_ref[d, rows, :] = k256
            ee_ref[d, blk] = jnp.broadcast_to(span(0, nch), (8, DK_C))

        pieces = []
        for j in range(nch):
            rs = slice(j * cs, (j + 1) * cs)
            own = j * cs // half
            cols = slice(own * half, (own + 1) * half)
            acc = None
            for d in range(2):
                for i in range(3):
                    term = diag[d][i][rs, cols] * mask_ref[d, i, rs, :]
                    acc = term if acc is None else acc + term
            other = quad[0][rs.start - half:rs.stop - half] if own == 1 else quad[1][rs]
            pieces.append(jnp.concatenate([other, acc] if own == 1 else [acc, other], axis=1))
        att = jnp.concatenate(pieces, axis=0)
        oi_ref[rows, :] = _dot(att.astype(BF16), v)

        unsafe_ref[blk] = jnp.logical_not(safe).astype(jnp.int32)
        sq_ref[rows, :] = qs
        for d in range(2):
            sk_ref[d, rows, :] = per_dir[d][0]
            sc_ref[d, rows, :] = per_dir[d][1]
        return carry

    n_blk = GROUP_ROWS // sb
    lax.fori_loop(0, n_blk, block_step, 0, unroll=True)

    def vpu_block(blk, carry):
        @pl.when(unsafe_ref[blk] != 0)
        def _():
            def chunk_step(i, carry2):
                crow = pl.ds(pl.multiple_of(blk * sb + i * cs, cs), cs)
                q, vv = sq_ref[crow, :], i_ref[crow, :].astype(F32)
                oi_ref[crow, :] += (_hgrn_intra(q, sk_ref[0, crow, :], vv, sc_ref[0, crow, :], reverse=False)
                                    + _hgrn_intra(q, sk_ref[1, crow, :], vv, sc_ref[1, crow, :], reverse=True))
                return carry2

            lax.fori_loop(0, sb // cs, chunk_step, 0)

        return carry

    lax.fori_loop(0, n_blk, vpu_block, 0)

    is_ctx = pl.program_id(1) < n_ctx
    pl.when(is_ctx)(functools.partial(
        _hgrn_states, qe_ref, ke_ref, i_ref, ee_ref, oi_ref, t_ctx, st_ref=st_ref))
    pl.when(jnp.logical_not(is_ctx))(functools.partial(
        _hgrn_states, qe_ref, ke_ref, i_ref, ee_ref, oi_ref, t_lat, s0_ref=s0_ref))

    o = oi_ref[...]
    on = o * lax.rsqrt(jnp.mean(o * o, axis=-1, keepdims=True) + EPS) * gn_ref[...]
    o_ref[...] = (on * _silu(g_ref[...].astype(F32))).astype(o_ref.dtype)


def _hgrn(z16, z32, lower, gnorm, state_hgrn, oi, n_layers, prev_st, n_ctx_req, t_ctx, t_lat, n_rows):
    nh = N_HEADS_C
    n_ctx = n_ctx_req * t_ctx // GROUP_ROWS
    n_grp = n_rows // GROUP_ROWS
    rows = GROUP_ROWS
    assert t_lat == GROUP_ROWS and t_ctx % HGRN_SUPER == 0
    hs = HGRN_HEADS
    width = hs * DK_C
    assert nh % hs == 0
    blk = lambda p: pl.BlockSpec((rows, width), lambda h, g, p=p: (g, p * nh // hs + h))
    st_spec, st_shape = _state_spec(n_ctx_req, n_layers, oi, (2, nh, DK_C, DK_C),
                                    (2, hs, DK_C, DK_C), lambda h: (0, h, 0, 0), n_ctx, GROUP_ROWS // t_ctx)
    in_specs = [blk(0), blk(1), blk(2), blk(0), blk(1),
                pl.BlockSpec((2, width), lambda h, g: (0, h)),
                pl.BlockSpec((1, DK_C), lambda h, g: (0, 0)),
                pl.BlockSpec((None, None, 2, hs, DK_C, DK_C),
                             lambda h, g: (jnp.maximum(g - n_ctx, 0), oi, 0, h, 0, 0))]
    args = [z16, z16, z16, z32, z32, lower, gnorm, state_hgrn]
    aliases = {}
    if prev_st is not None:
        in_specs.append(pl.BlockSpec(memory_space=pl.ANY))
        args.append(prev_st)
        aliases = {len(args) - 1: 1}
    return pl.pallas_call(
        functools.partial(_hgrn_kernel, t_ctx=t_ctx, t_lat=t_lat, n_ctx=n_ctx),
        grid=(nh // hs, n_grp),
        in_specs=in_specs,
        out_specs=[pl.BlockSpec((rows, width), lambda h, g: (g, h)), st_spec],
        out_shape=[jax.ShapeDtypeStruct((n_rows, D_MODEL), BF16), st_shape],
        input_output_aliases=aliases,
        scratch_shapes=[
            pltpu.VMEM((2, rows, DK_C), BF16),
            pltpu.VMEM((2, rows, DK_C), BF16),
            pltpu.VMEM((2, rows // HGRN_SUPER, 8, DK_C), F32),
            pltpu.VMEM((rows, DK_C), F32),
            pltpu.VMEM((rows, DK_C), F32),
            pltpu.VMEM((2, rows, DK_C), F32),
            pltpu.VMEM((2, rows, DK_C), F32),
            pltpu.VMEM((2, 3, HGRN_SUPER, HGRN_SUPER // 2), F32),
            pltpu.SMEM((rows // HGRN_SUPER,), jnp.int32),
        ],
        compiler_params=_params(("arbitrary", "arbitrary"), 32),
        name="hgrn",
    )(*args)


def kernel(x_prompt, x_sample, cache_kv, state_ret, state_hgrn, c, c_ctx, w_mod, b_mod, norm_g,
           w_in_even, w_out_even, rpb, ret_decay, w_in_odd, w_out_odd, hgrn_lb, hgrn_gnorm,
           w_ffn_in, w_ffn_out):
    bp, tp, _ = x_prompt.shape
    bs, ts, _ = x_sample.shape
    np_rows, ns_rows = bp * tp, bs * ts
    n_rows = np_rows + ns_rows
    n_even, n_odd = w_in_even.shape[0], w_in_odd.shape[0]
    assert np_rows % GROUP_ROWS == 0 and ts == GROUP_ROWS
    xs = [x_prompt.reshape(np_rows, D_MODEL), x_sample.reshape(ns_rows, D_MODEL)]

    n_c = bs + 1
    pad = (-n_c) % 8
    cvec = jnp.concatenate([c, c_ctx[None], jnp.zeros((pad, D_MODEL), F32)], axis=0)
    mod_all = _modulation(cvec, w_mod, b_mod)
    grp = np.concatenate([np.full(np_rows // GROUP_ROWS, bs), np.arange(bs)]).astype(np.int32)
    mod_all = mod_all[:, grp].reshape(DEPTH, len(grp), 6, D_MODEL)
    mod_all = jnp.pad(mod_all, ((0, 0), (0, 0), (0, MOD_ROWS - 6), (0, 0)))

    p_lb = jax.nn.softmax(hgrn_lb.astype(F32), axis=0)
    lower = jnp.clip(jnp.cumsum(p_lb, axis=0) - p_lb[0], 0.0, 1.0)
    pair_bias = _na_bias_tables(rpb)
    rope_tabs = _rope_tables(ts)

    w_in_even_b, w_out_even_b = w_in_even.astype(BF16), w_out_even.astype(BF16)
    w_out_odd_b = w_out_odd.astype(BF16)
    w_in_odd_q = w_in_odd[:, :, :D_C].astype(BF16)
    w_in_odd_f = w_in_odd[:, :, D_C:3 * D_C].astype(BF16)
    w_in_odd_ig = w_in_odd[:, :, 3 * D_C:].astype(BF16)
    ffn_w = [_ffn_weights(w_ffn_in[l], w_ffn_out[l]) for l in range(DEPTH)]

    kv_out = ret_out = hg_out = None
    y_split = None
    for l in range(DEPTH):
        mod = mod_all[l]
        g = norm_g[l].reshape(4, 1, D_MODEL)
        if l % 2 == 0:
            e = l // 2
            z, = _in_proj(xs, g[0], mod, [([w_in_even_b[e]], BF16)])
            oa, kv_out = _attention(z, cache_kv, pair_bias, e, n_even, kv_out, bp, tp, n_rows)
            ob, ret_out = _retention(z, ret_decay[e], rope_tabs, state_ret, e, n_even, ret_out, bp, tp, ts, n_rows)
            x = _out_proj([oa, ob], w_out_even_b[e], xs, g[1], mod)
        else:
            oi = l // 2
            z16, z32 = _in_proj(xs, g[0], mod, [([w_in_odd_q[oi], w_in_odd_ig[oi]], BF16),
                                                ([w_in_odd_f[oi]], F32)])
            gn = hgrn_gnorm[oi].reshape(1, DK_C)
            o, hg_out = _hgrn(z16, z32, lower[oi], gn, state_hgrn, oi, n_odd, hg_out, bp, tp, ts, n_rows)
            x = _out_proj([o], w_out_odd_b[oi], xs, g[1], mod)
        if l == DEPTH - 1:
            y_split = _ffn(x, g[2], g[3], mod, *ffn_w[l], split=(x_prompt.shape, x_sample.shape))
        else:
            xs = [_ffn(x, g[2], g[3], mod, *ffn_w[l])]

    return (y_split[0], y_split[1], kv_out, ret_out, hg_out)
```

```python
import functools

import numpy as np
import jax
import jax.numpy as jnp
from jax import lax
from jax.experimental import pallas as pl
from jax.experimental.pallas import tpu as pltpu

F32 = jnp.float32
BF16 = jnp.bfloat16

D_MODEL = 1024
DEPTH = 4
GRID_W = 64
HEAD_DIM = 64
N_HEADS_A = 8
N_HEADS_B = 8
D_A = N_HEADS_A * HEAD_DIM
D_B = N_HEADS_B * HEAD_DIM
WIN_R = 8
WIN_C = 16
N_HEADS_C = 8
DK_C = D_MODEL // N_HEADS_C
D_C = N_HEADS_C * DK_C
D_FF = ((8 * D_MODEL // 3 + 255) // 256) * 256
ATTN_SCALE = HEAD_DIM ** -0.5
HGRN_CHUNK = 32
HGRN_SUPER = 256
HGRN_HEADS = 4
HGRN_SAFE_LOG = 75.0
ROPE_BASE = 10000.0
EPS = 1e-6
MASK_NEG = -1e30
F_MIN = 1e-30

GROUP_ROWS = 1024
MOD_ROWS = 8
LANES = 128
HEADS_PER_TILE = LANES // HEAD_DIM
PROJ_TN = 512
FFN_TF = 256
MOD_TN = 1536
MIB = 1024 * 1024

SH1, SC1, GT1, SH2, SC2, GT2 = range(6)

NA_QROWS = 4
NA_KROWS = 12
NA_KSTART = (0, 0, 4, 4)


def _params(sem, vmem_mib):
    return pltpu.CompilerParams(dimension_semantics=sem, vmem_limit_bytes=vmem_mib * MIB)


def _sigmoid(x):
    return 1.0 / (1.0 + jnp.exp(-x))


def _silu(x):
    return x * _sigmoid(x)


def _dot(a, b):
    return jnp.dot(a, b, preferred_element_type=F32)


def _dot_nt(a, b):
    return lax.dot_general(a, b, (((1,), (1,)), ((), ())), preferred_element_type=F32)


def _dot_tt(a, b):
    return lax.dot_general(a, b, (((0,), (1,)), ((), ())), preferred_element_type=F32)


def _dot_tn(a, b):
    return lax.dot_general(a, b, (((0,), (0,)), ((), ())), preferred_element_type=F32)


def _mod_kernel(c_ref, w_ref, b_ref, o_ref):
    s = _silu(c_ref[...]).astype(BF16)
    o_ref[...] = _dot(s, w_ref[...].astype(BF16)) + b_ref[...]


def _modulation(cvec, w_mod, b_mod):
    rows = cvec.shape[0]
    n = w_mod.shape[-1]
    return pl.pallas_call(
        _mod_kernel,
        grid=(DEPTH, n // MOD_TN),
        in_specs=[
            pl.BlockSpec((rows, D_MODEL), lambda l, j: (0, 0)),
            pl.BlockSpec((None, D_MODEL, MOD_TN), lambda l, j: (l, 0, j)),
            pl.BlockSpec((None, 1, MOD_TN), lambda l, j: (l, 0, j)),
        ],
        out_specs=pl.BlockSpec((None, rows, MOD_TN), lambda l, j: (l, 0, j)),
        out_shape=jax.ShapeDtypeStruct((DEPTH, rows, n), F32),
        compiler_params=_params(("arbitrary", "arbitrary"), 32),
        name="modulation",
    )(cvec, w_mod, b_mod.reshape(DEPTH, 1, n))


def _norm_mod(x, g, mod, sh_row, sc_row):
    y = x * lax.rsqrt(jnp.mean(x * x, axis=-1, keepdims=True) + EPS) * g
    return y * (1.0 + mod[sc_row:sc_row + 1]) + mod[sh_row:sh_row + 1]


def _stream_specs(xs):
    tm = GROUP_ROWS
    if len(xs) == 1:
        return [pl.BlockSpec((tm, D_MODEL), lambda i: (i, 0))]
    n_ctx = xs[0].shape[0] // tm
    return [pl.BlockSpec((tm, D_MODEL), lambda i: (jnp.minimum(i, n_ctx - 1), 0)),
            pl.BlockSpec((tm, D_MODEL), lambda i: (jnp.maximum(i - n_ctx, 0), 0))]


def _stream_tile(x_refs, n_ctx):
    if len(x_refs) == 1:
        return x_refs[0][...]
    return jnp.where(pl.program_id(0) < n_ctx, x_refs[0][...], x_refs[1][...])


def _in_proj_kernel(*refs, n_x, n_ctx, n_w):
    x_refs, (g_ref, mod_ref) = refs[:n_x], refs[n_x:n_x + 2]
    w_refs, o_refs = refs[n_x + 2:n_x + 2 + sum(n_w)], refs[n_x + 2 + sum(n_w):]
    h = _norm_mod(_stream_tile(x_refs, n_ctx), g_ref[...], mod_ref[...], SH1, SC1).astype(BF16)
    w_iter = iter(w_refs)
    for o_ref, count in zip(o_refs, n_w):
        col0 = 0
        for w_ref in (next(w_iter) for _ in range(count)):
            for j in range(w_ref.shape[1] // PROJ_TN):
                cols = slice(j * PROJ_TN, (j + 1) * PROJ_TN)
                o_ref[:, col0 + j * PROJ_TN:col0 + (j + 1) * PROJ_TN] = _dot(h, w_ref[:, cols]).astype(o_ref.dtype)
            col0 += w_ref.shape[1]


def _resident(block_shape):
    return pl.BlockSpec(block_shape, lambda *_: (0,) * len(block_shape), pipeline_mode=pl.Buffered(1))


def _in_proj(xs, g, mod, outputs):
    m = sum(x.shape[0] for x in xs)
    tm = GROUP_ROWS
    ws = [w for pieces, _ in outputs for w in pieces]
    ns = [sum(w.shape[1] for w in pieces) for pieces, _ in outputs]
    assert all(w.shape[1] % PROJ_TN == 0 for w in ws)
    return pl.pallas_call(
        functools.partial(_in_proj_kernel, n_x=len(xs), n_ctx=xs[0].shape[0] // tm,
                          n_w=tuple(len(pieces) for pieces, _ in outputs)),
        grid=(m // tm,),
        in_specs=_stream_specs(xs) + [
            pl.BlockSpec((1, D_MODEL), lambda i: (0, 0)),
            pl.BlockSpec((None, MOD_ROWS, D_MODEL), lambda i: (i, 0, 0)),
        ] + [_resident(w.shape) for w in ws],
        out_specs=[pl.BlockSpec((tm, n), lambda i: (i, 0)) for n in ns],
        out_shape=[jax.ShapeDtypeStruct((m, n), dt) for n, (_, dt) in zip(ns, outputs)],
        compiler_params=_params(("arbitrary",), 56),
        name="in_proj",
    )(*xs, g, mod, *ws)


def _out_proj_kernel(*refs, n_in, n_x, n_ctx):
    a_refs, w_refs = refs[:n_in], refs[n_in:2 * n_in]
    x_refs = refs[2 * n_in:2 * n_in + n_x]
    g_ref, mod_ref, o_ref = refs[2 * n_in + n_x:]
    y = _dot(a_refs[0][...], w_refs[0][...])
    for a_ref, w_ref in zip(a_refs[1:], w_refs[1:]):
        y = y + _dot(a_ref[...], w_ref[...])
    yn = y * lax.rsqrt(jnp.mean(y * y, axis=-1, keepdims=True) + EPS) * g_ref[...]
    o_ref[...] = _stream_tile(x_refs, n_ctx) + mod_ref[GT1:GT1 + 1, :] * yn


def _out_proj(acts, w, xs, g, mod):
    m = sum(x.shape[0] for x in xs)
    tm = GROUP_ROWS
    n_in = len(acts)
    ks = [a.shape[1] for a in acts]
    assert sum(ks) == w.shape[0] and len(set(ks)) == 1
    in_specs = [pl.BlockSpec((tm, k), lambda i: (i, 0)) for k in ks]
    in_specs += [pl.BlockSpec((ks[0], D_MODEL), lambda i, p=p: (p, 0)) for p in range(n_in)]
    in_specs += _stream_specs(xs) + [
        pl.BlockSpec((1, D_MODEL), lambda i: (0, 0)),
        pl.BlockSpec((None, MOD_ROWS, D_MODEL), lambda i: (i, 0, 0)),
    ]
    return pl.pallas_call(
        functools.partial(_out_proj_kernel, n_in=n_in, n_x=len(xs), n_ctx=xs[0].shape[0] // tm),
        grid=(m // tm,),
        in_specs=in_specs,
        out_specs=pl.BlockSpec((tm, D_MODEL), lambda i: (i, 0)),
        out_shape=jax.ShapeDtypeStruct((m, D_MODEL), F32),
        compiler_params=_params(("arbitrary",), 48),
        name="out_proj",
    )(*acts, *([w] * n_in), *xs, g, mod)


def _ffn_kernel(x_ref, xn_ref, g2_ref, g3_ref, mod_ref, modn_ref, win_ref, wo_ref, *rest, n_ctx_groups):
    o_refs, (h_ref, acc_ref) = rest[:-2], rest[-2:]
    tm = x_ref.shape[0]
    nf = wo_ref.shape[0]
    ahead = -(-tm // (nf * 16)) * 16
    i = pl.program_id(0)
    cur, nxt = i % 2, (i + 1) % 2

    @pl.when(i == 0)
    def _():
        h_ref[0] = _norm_mod(x_ref[...], g2_ref[...], mod_ref[...], SH2, SC2).astype(BF16)

    acc_ref[...] = jnp.zeros_like(acc_ref)

    def hidden_step(f, carry):
        h = h_ref[cur]
        a = _dot(h, win_ref[:, pl.ds(pl.multiple_of(f * FFN_TF, FFN_TF), FFN_TF)])
        u = _dot(h, win_ref[:, pl.ds(pl.multiple_of(D_FF + f * FFN_TF, FFN_TF), FFN_TF)])
        acc_ref[...] += _dot((_silu(a) * u).astype(BF16), wo_ref[f])
        r0 = pl.multiple_of(jnp.minimum(f * ahead, tm - ahead), 16)
        rows = pl.ds(r0, ahead)
        h_ref[nxt, rows, :] = _norm_mod(xn_ref[rows, :], g2_ref[...], modn_ref[...], SH2, SC2).astype(BF16)
        return carry

    lax.fori_loop(0, nf, hidden_step, 0, unroll=True)

    def finish(o_ref):
        y = acc_ref[...]
        yn = y * lax.rsqrt(jnp.mean(y * y, axis=-1, keepdims=True) + EPS) * g3_ref[...]
        o_ref[...] = (x_ref[...] + mod_ref[GT2:GT2 + 1, :] * yn).reshape(o_ref.shape)

    if len(o_refs) == 1:
        finish(o_refs[0])
    else:
        is_ctx = pl.program_id(0) < n_ctx_groups
        pl.when(is_ctx)(functools.partial(finish, o_refs[0]))
        pl.when(jnp.logical_not(is_ctx))(functools.partial(finish, o_refs[1]))


def _ffn_weights(w_in, w_out):
    return w_in.astype(BF16), w_out.astype(BF16).reshape(D_FF // FFN_TF, FFN_TF, D_MODEL)


def _ffn(x, g2, g3, mod, w_in, wo, split=None):
    m = x.shape[0]
    tm = GROUP_ROWS
    n_ctx = 0
    if split is None:
        out_specs = pl.BlockSpec((tm, D_MODEL), lambda i: (i, 0))
        out_shape = jax.ShapeDtypeStruct((m, D_MODEL), F32)
    else:
        (bp, tp, _), (bs, ts, _) = split
        n_ctx = bp * tp // tm
        assert ts == tm and (bp * tp) % tm == 0
        out_specs = [
            pl.BlockSpec((tm // tp, tp, D_MODEL), lambda i: (jnp.minimum(i, n_ctx - 1), 0, 0)),
            pl.BlockSpec((1, ts, D_MODEL), lambda i: (jnp.maximum(i - n_ctx, 0), 0, 0)),
        ]
        out_shape = [jax.ShapeDtypeStruct(s, F32) for s in split]
    last = m // tm - 1
    return pl.pallas_call(
        functools.partial(_ffn_kernel, n_ctx_groups=n_ctx),
        grid=(m // tm,),
        in_specs=[
            pl.BlockSpec((tm, D_MODEL), lambda i: (i, 0)),
            pl.BlockSpec((tm, D_MODEL), lambda i: (jnp.minimum(i + 1, last), 0)),
            pl.BlockSpec((1, D_MODEL), lambda i: (0, 0)),
            pl.BlockSpec((1, D_MODEL), lambda i: (0, 0)),
            pl.BlockSpec((None, MOD_ROWS, D_MODEL), lambda i: (i, 0, 0)),
            pl.BlockSpec((None, MOD_ROWS, D_MODEL), lambda i: (jnp.minimum(i + 1, last), 0, 0)),
            _resident(w_in.shape), _resident(wo.shape),
        ],
        out_specs=out_specs,
        out_shape=out_shape,
        scratch_shapes=[pltpu.VMEM((2, tm, D_MODEL), BF16), pltpu.VMEM((tm, D_MODEL), F32)],
        compiler_params=_params(("arbitrary",), 60),
        name="ffn",
    )(x, x, g2, g3, mod, mod, w_in, wo)


def _state_spec(n_req, n_layers, layer, tail, tail_block, tail_index, n_ctx, nb):
    n_own = n_layers if layer == 0 else 1
    first = 0 if layer == 0 else layer

    def index(j, g):
        return (jnp.minimum(g, n_ctx - 1), first) + tail_index(j)

    spec = pl.BlockSpec((nb, n_own) + tail_block, index)
    shape = jax.ShapeDtypeStruct((n_req, n_layers) + tail, F32)
    return spec, shape


def _zero_other_layers(st_ref):
    if st_ref.shape[1] > 1:
        st_ref[:, 1:] = jnp.zeros((st_ref.shape[0], st_ref.shape[1] - 1) + st_ref.shape[2:], st_ref.dtype)


NA_NRO = 2 * WIN_R - 1


def _na_bias_tables(rpb):
    qc = np.arange(GRID_W)[:, None]
    kc = np.arange(GRID_W)[None, :]
    win0 = np.clip(qc - WIN_C // 2, 0, GRID_W - WIN_C)
    col_valid = (kc >= win0) & (kc < win0 + WIN_C)
    col_off = np.clip(kc - qc, 1 - WIN_C, WIN_C - 1) + (WIN_C - 1)
    onehot = (col_off[None] == np.arange(2 * WIN_C - 1)[:, None, None]).astype(np.float32)
    tiles = jnp.einsum('ehrc,cqk->ehrqk', rpb.astype(F32), jnp.asarray(onehot), precision=lax.Precision.HIGHEST)
    tiles = jnp.where(col_valid[None, None, None], tiles, MASK_NEG)
    nxt = jnp.concatenate([tiles[:, :, 1:], tiles[:, :, -1:]], axis=2)
    return jnp.concatenate([tiles, nxt], axis=-1)


def _na_group_bias(pair_ref, hh, g):
    rows = GROUP_ROWS // GRID_W
    first_half = lax.broadcasted_iota(jnp.int32, (GRID_W, 2 * GRID_W), 1) < GRID_W
    bias = []
    for a in range(NA_QROWS):
        r = g * NA_QROWS + a
        row0 = min(max(r - WIN_R // 2, 0), rows - WIN_R)
        pieces = []
        for w in range(0, NA_KROWS, 2):
            kr = NA_KSTART[g] + w
            ro = kr - r + (WIN_R - 1)
            in0 = row0 <= kr < row0 + WIN_R
            in1 = row0 <= kr + 1 < row0 + WIN_R
            assert not (in0 or in1) or 0 <= ro < NA_NRO
            if in0 and in1:
                piece = pair_ref[hh, ro]
            elif in0:
                piece = jnp.where(first_half, pair_ref[hh, ro], MASK_NEG)
            elif in1:
                piece = jnp.where(first_half, MASK_NEG, pair_ref[hh, ro])
            else:
                piece = jnp.full((GRID_W, 2 * GRID_W), MASK_NEG, F32)
            pieces.append(piece)
        bias.append(jnp.concatenate(pieces, axis=1))
    return jnp.concatenate(bias, axis=0)


def _ctx_attend(q_ref, k_ref, v_ref, o_ref, kv_ref, t):
    for r in range(GROUP_ROWS // t):
        rs = slice(r * t, (r + 1) * t)
        q2, k2, v2 = q_ref[rs, :] * ATTN_SCALE, k_ref[rs, :], v_ref[rs, :]
        head_of_lane = lax.broadcasted_iota(jnp.int32, q2.shape, 1) // HEAD_DIM
        out = None
        for hh in range(HEADS_PER_TILE):
            hs = slice(hh * HEAD_DIM, (hh + 1) * HEAD_DIM)
            kv_ref[r, 0, 0, hh] = k2[:, hs].astype(F32)
            kv_ref[r, 0, 1, hh] = v2[:, hs].astype(F32)
            in_head = head_of_lane == hh
            s = _dot_nt(jnp.where(in_head, q2, jnp.zeros_like(q2)), k2)
            p = jnp.exp(s - jnp.max(s, axis=-1, keepdims=True))
            l = jnp.sum(p, axis=-1, keepdims=True)
            o = _dot(p.astype(BF16), v2) / l
            out = o if out is None else jnp.where(in_head, o, out)
        o_ref[rs, :] = out.astype(o_ref.dtype)
    _zero_other_layers(kv_ref)


def _na_attend(q_ref, k_ref, v_ref, ckv_ref, pair_ref, o_ref):
    q2, k2, v2 = q_ref[...] * ATTN_SCALE, k_ref[...], v_ref[...]
    tq = NA_QROWS * GRID_W
    tk = NA_KROWS * GRID_W
    head_of_lane = lax.broadcasted_iota(jnp.int32, q2.shape, 1) // HEAD_DIM
    kc2 = jnp.concatenate([ckv_ref[0, hh] for hh in range(HEADS_PER_TILE)], axis=1).astype(BF16)
    vc2 = jnp.concatenate([ckv_ref[1, hh] for hh in range(HEADS_PER_TILE)], axis=1).astype(BF16)
    keys = [jnp.concatenate([k2[s0 * GRID_W:s0 * GRID_W + tk], kc2], axis=0) for s0 in NA_KSTART]
    vals = [jnp.concatenate([v2[s0 * GRID_W:s0 * GRID_W + tk], vc2], axis=0) for s0 in NA_KSTART]
    no_bias = jnp.zeros((tq, kc2.shape[0]), F32)
    out = None
    for hh in range(HEADS_PER_TILE):
        in_head = head_of_lane == hh
        q = jnp.where(in_head, q2, jnp.zeros_like(q2))
        rows = []
        for g in range(len(NA_KSTART)):
            bias = jnp.concatenate([_na_group_bias(pair_ref, hh, g), no_bias], axis=1)
            s = _dot_nt(q[g * tq:(g + 1) * tq], keys[g]) + bias
            p = jnp.exp(s - jnp.max(s, axis=-1, keepdims=True))
            l = jnp.sum(p, axis=-1, keepdims=True)
            rows.append(_dot_tt(vals[g], p.astype(BF16)).T / l)
        o_h = jnp.concatenate(rows, axis=0)
        out = o_h if out is None else jnp.where(in_head, o_h, out)
    o_ref[...] = out.astype(o_ref.dtype)


def _attn_kernel(q_ref, k_ref, v_ref, ckv_ref, pair_ref, *rest, t_ctx, n_ctx):
    o_ref, kv_ref = rest[-2:]
    is_ctx = pl.program_id(1) < n_ctx
    pl.when(is_ctx)(functools.partial(_ctx_attend, q_ref, k_ref, v_ref, o_ref, kv_ref, t_ctx))
    pl.when(jnp.logical_not(is_ctx))(
        functools.partial(_na_attend, q_ref, k_ref, v_ref, ckv_ref, pair_ref, o_ref))


def _attention(z, cache_kv, pair_bias, e, n_layers, prev_kv, n_ctx_req, t_ctx, n_rows):
    npair = D_A // LANES
    hp = HEADS_PER_TILE
    n_ctx = n_ctx_req * t_ctx // GROUP_ROWS
    n_grp = n_rows // GROUP_ROWS
    past = cache_kv.shape[-2]
    blk = lambda p: pl.BlockSpec((GROUP_ROWS, LANES), lambda j, g, p=p: (g, p * npair + j))
    kv_spec, kv_shape = _state_spec(n_ctx_req, n_layers, e, (2, N_HEADS_A, t_ctx, HEAD_DIM),
                                    (2, hp, t_ctx, HEAD_DIM), lambda j: (0, j, 0, 0), n_ctx, GROUP_ROWS // t_ctx)
    in_specs = [
        blk(0), blk(1), blk(2),
        pl.BlockSpec((None, None, 2, hp, past, HEAD_DIM), lambda j, g: (jnp.maximum(g - n_ctx, 0), e, 0, j, 0, 0)),
        pl.BlockSpec((None, hp) + pair_bias.shape[2:], lambda j, g: (e, j, 0, 0, 0)),
    ]
    args = [z, z, z, cache_kv, pair_bias]
    aliases = {}
    if prev_kv is not None:
        in_specs.append(pl.BlockSpec(memory_space=pl.ANY))
        args.append(prev_kv)
        aliases = {len(args) - 1: 1}
    return pl.pallas_call(
        functools.partial(_attn_kernel, t_ctx=t_ctx, n_ctx=n_ctx),
        grid=(npair, n_grp),
        in_specs=in_specs,
        out_specs=[pl.BlockSpec((GROUP_ROWS, LANES), lambda j, g: (g, j)), kv_spec],
        out_shape=[jax.ShapeDtypeStruct((n_rows, D_A), BF16), kv_shape],
        input_output_aliases=aliases,
        compiler_params=_params(("arbitrary", "arbitrary"), 56),
        name="attention",
    )(*args)


def _rope_tables(t):
    half = HEAD_DIM // 2
    nf = half // 2
    inv = ROPE_BASE ** (-np.arange(nf, dtype=np.float32) / nf)
    pos = np.arange(t)
    ang_r = (pos // GRID_W).astype(np.float32)[:, None] * inv[None, :]
    ang_c = (pos % GRID_W).astype(np.float32)[:, None] * inv[None, :]
    ang_r, ang_c = jnp.asarray(ang_r), jnp.asarray(ang_c)
    cr, sr, cc, sc = jnp.cos(ang_r), jnp.sin(ang_r), jnp.cos(ang_c), jnp.sin(ang_c)
    cos = jnp.concatenate([cr, cr, cc, cc], axis=-1)
    sin = jnp.concatenate([-sr, sr, -sc, sc], axis=-1)
    return jnp.tile(cos, (1, HEADS_PER_TILE)), jnp.tile(sin, (1, HEADS_PER_TILE))


def _rope(x, cos, sin):
    nf = HEAD_DIM // 4
    lane = lax.broadcasted_iota(jnp.int32, x.shape, 1)
    partner = jnp.where(lane % (2 * nf) < nf, pltpu.roll(x, LANES - nf, axis=1), pltpu.roll(x, nf, axis=1))
    return x * cos + partner * sin


def _decay_matrix(dm_ref, lg, t):
    n_i = lax.broadcasted_iota(jnp.int32, (t, t), 0)
    m_i = lax.broadcasted_iota(jnp.int32, (t, t), 1)
    diff = (n_i - m_i).astype(F32)
    dist = jnp.abs(diff)
    for hh in range(HEADS_PER_TILE):
        one_sided = jnp.exp(jnp.where(diff >= 0, lg[hh][0], lg[hh][1]) * dist)
        dm_ref[hh] = jnp.where(diff == 0, 2.0, one_sided)


def _head_rms_gate(o, g, head_of_lane):
    sq = o * o
    ms = sum(jnp.where(head_of_lane == hh,
                       jnp.sum(jnp.where(head_of_lane == hh, sq, 0.0), axis=-1, keepdims=True), 0.0)
             for hh in range(HEADS_PER_TILE)) * (1.0 / HEAD_DIM)
    return o * lax.rsqrt(ms + EPS) * _silu(g)


def _two_sided_scores(q, kb, vb, dm_ref, head_of_lane):
    o = None
    for hh in range(HEADS_PER_TILE):
        in_head = head_of_lane == hh
        att = _dot_nt(jnp.where(in_head, q, 0.0).astype(BF16), kb) * dm_ref[hh]
        o_h = _dot(att.astype(BF16), vb)
        o = o_h if o is None else jnp.where(in_head, o_h, o)
    return o


def _retain(q_ref, k_ref, v_ref, g_ref, o_ref, st_ref, dm_ref, lg, t):
    pos = lax.broadcasted_iota(jnp.int32, (t, 1), 0).astype(F32)
    head_of_lane = lax.broadcasted_iota(jnp.int32, (1, LANES), 1) // HEAD_DIM
    lgv = [sum(jnp.where(head_of_lane == hh, lg[hh][d], 0.0) for hh in range(HEADS_PER_TILE)) for d in range(2)]
    for r in range(GROUP_ROWS // t):
        rs = slice(r * t, (r + 1) * t)
        q2 = q_ref[rs, :].astype(F32)
        k2 = k_ref[rs, :].astype(F32) * (HEAD_DIM ** -0.5)
        v2 = v_ref[rs, :]
        st = (_dot_tn((k2 * jnp.exp(lgv[0] * (t - 1.0 - pos))).astype(BF16), v2),
              _dot_tn((k2 * jnp.exp(lgv[1] * pos)).astype(BF16), v2))
        for hh in range(HEADS_PER_TILE):
            hs = slice(hh * HEAD_DIM, (hh + 1) * HEAD_DIM)
            st_ref[r, 0, 0, hh] = st[0][hs, hs]
            st_ref[r, 0, 1, hh] = st[1][hs, hs]
        o = _two_sided_scores(q2, k2.astype(BF16), v2, dm_ref, head_of_lane)
        o_ref[rs, :] = _head_rms_gate(o, g_ref[rs, :].astype(F32), head_of_lane).astype(o_ref.dtype)
    _zero_other_layers(st_ref)


def _retain_chunked(q_ref, k_ref, v_ref, g_ref, o_ref, dm_ref, lg, t, c, rope, s0_ref):
    n_c = t // c
    pos = lax.broadcasted_iota(jnp.int32, (c, 1), 0).astype(F32)
    head_of_lane = lax.broadcasted_iota(jnp.int32, (1, LANES), 1) // HEAD_DIM
    head_of_row = lax.broadcasted_iota(jnp.int32, (LANES, 1), 0) // HEAD_DIM
    same_head = head_of_row == head_of_lane
    lgv = [sum(jnp.where(head_of_lane == hh, lg[hh][d], 0.0) for hh in range(HEADS_PER_TILE)) for d in range(2)]
    lgc = [sum(jnp.where(head_of_row == hh, lg[hh][d], 0.0) for hh in range(HEADS_PER_TILE)) for d in range(2)]
    chunk_decay = [jnp.exp(lgc[d] * float(c)) for d in range(2)]
    q_dec = (jnp.exp(lgv[0] * (pos + 1.0)), jnp.exp(lgv[1] * (c - pos)))
    k_dec = (jnp.exp(lgv[0] * (c - 1.0 - pos)), jnp.exp(lgv[1] * pos))

    q2 = _rope(q_ref[...].astype(F32), rope[0][...], rope[1][...])
    k2 = _rope(k_ref[...].astype(F32) * (HEAD_DIM ** -0.5), rope[0][...], rope[1][...])
    cs = [slice(i * c, (i + 1) * c) for i in range(n_c)]

    def initial(d):
        zero = jnp.zeros((HEAD_DIM, HEAD_DIM), F32)
        rows = [jnp.concatenate([s0_ref[d, hh] if col == hh else zero for col in range(HEADS_PER_TILE)], axis=1)
                for hh in range(HEADS_PER_TILE)]
        return jnp.concatenate(rows, axis=0)

    incr = [[jnp.where(same_head, _dot_tn((k2[s] * k_dec[d]).astype(BF16), v_ref[s, :]), 0.0) for s in cs]
            for d in range(2)]
    fwd = [initial(0)]
    for i in range(n_c - 1):
        fwd.append(fwd[i] * chunk_decay[0] + incr[0][i])
    bwd = [initial(1)]
    for i in reversed(range(1, n_c)):
        bwd.insert(0, bwd[0] * chunk_decay[1] + incr[1][i])

    for i, s in enumerate(cs):
        qc = q2[s]
        o = _two_sided_scores(qc, k2[s].astype(BF16), v_ref[s, :], dm_ref, head_of_lane)
        o = o + _dot((qc * q_dec[0]).astype(BF16), fwd[i].astype(BF16))
        o = o + _dot((qc * q_dec[1]).astype(BF16), bwd[i].astype(BF16))
        o_ref[s, :] = _head_rms_gate(o, g_ref[s, :].astype(F32), head_of_lane).astype(o_ref.dtype)


def _ret_kernel(dec_ref, q_ref, k_ref, v_ref, g_ref, cos_ref, sin_ref, s0_ref, *rest, t_ctx, t_lat, n_ctx):
    o_ref, st_ref, dm_ref = rest[-3:]
    j, g = pl.program_id(0), pl.program_id(1)
    lg = [[-jnp.exp(jnp.full((1, 1), dec_ref[d, j * HEADS_PER_TILE + hh], F32)) for d in range(2)]
          for hh in range(HEADS_PER_TILE)]
    pl.when(g == 0)(functools.partial(_decay_matrix, dm_ref, lg, t_ctx))
    is_ctx = g < n_ctx
    pl.when(is_ctx)(functools.partial(_retain, q_ref, k_ref, v_ref, g_ref, o_ref, st_ref, dm_ref, lg, t_ctx))
    pl.when(jnp.logical_not(is_ctx))(functools.partial(
        _retain_chunked, q_ref, k_ref, v_ref, g_ref, o_ref, dm_ref, lg, t_lat, t_ctx, (cos_ref, sin_ref), s0_ref))


def _retention(z, ret_decay, rope_tabs, state_ret, e, n_layers, prev_st, n_ctx_req, t_ctx, t_lat, n_rows):
    npair = D_B // LANES
    hp = HEADS_PER_TILE
    sec0 = 3 * D_A // LANES
    n_ctx = n_ctx_req * t_ctx // GROUP_ROWS
    n_grp = n_rows // GROUP_ROWS
    assert t_lat == GROUP_ROWS
    blk = lambda p: pl.BlockSpec((GROUP_ROWS, LANES), lambda j, g, p=p: (g, sec0 + p * npair + j))
    st_spec, st_shape = _state_spec(n_ctx_req, n_layers, e, (2, N_HEADS_B, HEAD_DIM, HEAD_DIM),
                                    (2, hp, HEAD_DIM, HEAD_DIM), lambda j: (0, j, 0, 0), n_ctx,
                                    GROUP_ROWS // t_ctx)
    in_specs = [
        pl.BlockSpec(memory_space=pltpu.SMEM), blk(0), blk(1), blk(2), blk(3),
        pl.BlockSpec((t_lat, LANES), lambda j, g: (0, 0)),
        pl.BlockSpec((t_lat, LANES), lambda j, g: (0, 0)),
        pl.BlockSpec((None, None, 2, hp, HEAD_DIM, HEAD_DIM),
                     lambda j, g: (jnp.maximum(g - n_ctx, 0), e, 0, j, 0, 0)),
    ]
    args = [ret_decay, z, z, z, z, rope_tabs[0], rope_tabs[1], state_ret]
    aliases = {}
    if prev_st is not None:
        in_specs.append(pl.BlockSpec(memory_space=pl.ANY))
        args.append(prev_st)
        aliases = {len(args) - 1: 1}
    return pl.pallas_call(
        functools.partial(_ret_kernel, t_ctx=t_ctx, t_lat=t_lat, n_ctx=n_ctx),
        grid=(npair, n_grp),
        in_specs=in_specs,
        out_specs=[pl.BlockSpec((GROUP_ROWS, LANES), lambda j, g: (g, j)), st_spec],
        out_shape=[jax.ShapeDtypeStruct((n_rows, D_B), BF16), st_shape],
        input_output_aliases=aliases,
        scratch_shapes=[pltpu.VMEM((hp, t_ctx, t_ctx), F32)],
        compiler_params=_params(("arbitrary", "arbitrary"), 56),
        name="retention",
    )(*args)


def _seg_scan(x, reverse):
    t = x.shape[0]
    row = lax.broadcasted_iota(jnp.int32, (t, 1), 0) % HGRN_CHUNK
    sft = 1
    while sft < HGRN_CHUNK:
        if reverse:
            x = x + jnp.where(row < HGRN_CHUNK - sft, pltpu.roll(x, t - sft, axis=0), 0.0)
        else:
            x = x + jnp.where(row >= sft, pltpu.roll(x, sft, axis=0), 0.0)
        sft *= 2
    return x


def _hgrn_intra(q, k, v, c, reverse):
    cs = HGRN_CHUNK
    sub = 8
    row = lax.broadcasted_iota(jnp.int32, (cs, 1), 0)
    parts = [jnp.zeros((sub, v.shape[1]), F32) for _ in range(cs // sub)]
    for s in range(cs):
        blk_s = s // sub
        blks = range(0, blk_s + 1) if reverse else range(blk_s, cs // sub)
        c_s, k_s, v_s = c[s:s + 1], k[s:s + 1], v[s:s + 1]
        for bt in blks:
            rs = slice(bt * sub, (bt + 1) * sub)
            w = q[rs] * k_s * jnp.exp(jnp.minimum(c[rs] - c_s, 0.0))
            col = jnp.sum(w, axis=-1, keepdims=True)
            if bt == blk_s:
                keep = (row[rs] <= s) if reverse else (row[rs] >= s)
                col = jnp.where(keep, col, 0.0)
            parts[bt] = parts[bt] + col * v_s
    return jnp.concatenate(parts, axis=0)


def _hgrn_states(qe_ref, ke_ref, v_ref, ee_ref, oi_ref, t, s0_ref=None, st_ref=None):
    sb = HGRN_SUPER
    n_sb = t // sb
    for r in range(GROUP_ROWS // t):
        incr = []
        for j in range(n_sb):
            rows = slice((r * n_sb + j) * sb, (r * n_sb + j + 1) * sb)
            keys = jnp.concatenate([ke_ref[0, rows, :], ke_ref[1, rows, :]], axis=1)
            incr.append(_dot_tn(keys, v_ref[rows, :]))
        for d in range(2):
            st = s0_ref[d] if s0_ref is not None else None
            order = range(n_sb) if d == 0 else reversed(range(n_sb))
            for j in order:
                blk = r * n_sb + j
                rows = slice(blk * sb, (blk + 1) * sb)
                upd = incr[j][d * DK_C:(d + 1) * DK_C]
                if st is None:
                    st = upd
                else:
                    oi_ref[rows, :] += _dot(qe_ref[d, rows, :], st.astype(BF16))
                    decay = jnp.broadcast_to(ee_ref[d, blk][0:1], (DK_C, DK_C)).T
                    st = st * decay + upd
            if st_ref is not None:
                st_ref[r, 0, d] = st
    if st_ref is not None:
        _zero_other_layers(st_ref)


def _hgrn_kernel(q_ref, i_ref, g_ref, ff_ref, fb_ref, lb_ref, gn_ref, s0_ref, *rest, t_ctx, t_lat, n_ctx):
    o_ref, st_ref = rest[-11:-9]
    scratch = rest[-9:]
    mask_ref = scratch[-2]
    sb = HGRN_SUPER
    half = sb // 2

    @pl.when((pl.program_id(0) == 0) & (pl.program_id(1) == 0))
    def _():
        t = lax.broadcasted_iota(jnp.int32, (sb, half), 0)
        s = lax.broadcasted_iota(jnp.int32, (sb, half), 1) + ((t >> 7) << 7)
        same32 = (t >> 5) == (s >> 5)
        same64 = (t >> 6) == (s >> 6)
        for d, (incl, strict) in enumerate(((s <= t, s < t), (s >= t, s > t))):
            for i, m in enumerate((same32 & incl, same64 & jnp.logical_not(same32) & strict,
                                   jnp.logical_not(same64) & strict)):
                mask_ref[d, i] = jnp.where(m, 1.0, 0.0)

    for hd in range(HGRN_HEADS):
        ls = slice(hd * DK_C, (hd + 1) * DK_C)
        _hgrn_head(q_ref.at[:, ls], i_ref.at[:, ls], g_ref.at[:, ls], ff_ref.at[:, ls], fb_ref.at[:, ls],
                   lb_ref.at[:, ls], gn_ref, s0_ref.at[:, hd], o_ref.at[:, ls], st_ref.at[:, :, :, hd],
                   *scratch, t_ctx=t_ctx, t_lat=t_lat, n_ctx=n_ctx)


def _hgrn_head(q_ref, i_ref, g_ref, ff_ref, fb_ref, lb_ref, gn_ref, s0_ref, o_ref, st_ref,
               qe_ref, ke_ref, ee_ref, oi_ref, sq_ref, sk_ref, sc_ref, mask_ref, unsafe_ref,
               *, t_ctx, t_lat, n_ctx):
    cs = HGRN_CHUNK
    sb = HGRN_SUPER
    half = sb // 2
    nch = sb // cs

    def block_step(blk, carry):
        rows = pl.ds(pl.multiple_of(blk * sb, sb), sb)
        qs = _silu(q_ref[rows, :].astype(F32)) * (DK_C ** -0.5)
        v = i_ref[rows, :]
        per_dir = []
        c_min = None
        for d, f_ref in enumerate((ff_ref, fb_ref)):
            lb = lb_ref[d:d + 1, :]
            fr = f_ref[rows, :]
            e = jnp.exp(-jnp.abs(fr))
            r = 1.0 / (1.0 + e)
            sig_pos = jnp.where(fr >= 0, r, e * r)
            sig_neg = jnp.where(fr >= 0, e * r, r)
            f = lb + (1.0 - lb) * sig_pos
            k = (1.0 - lb) * sig_neg
            c = _seg_scan(jnp.log(jnp.maximum(f, F_MIN)), reverse=(d == 1))
            c3 = c.reshape(sb // cs, cs, DK_C)
            c_end = c3[:, cs - 1:cs, :] if d == 0 else c3[:, 0:1, :]
            tot = jnp.broadcast_to(c_end, c3.shape).reshape(sb, DK_C)
            per_dir.append((k, c, tot))
            m = jnp.min(c)
            c_min = m if c_min is None else jnp.minimum(c_min, m)

        safe = c_min >= -HGRN_SAFE_LOG
        diag = []
        quad = []
        for d, (k, c, tot) in enumerate(per_dir):
            q32 = qs * jnp.exp(c)
            k32 = k * jnp.exp(tot - c)
            k_hat = jnp.where(safe, k * jnp.exp(-c), 0.0).astype(BF16)
            e32 = [jnp.exp(tot[j * cs:j * cs + 1]) for j in range(nch)]

            def span(lo, hi):
                out = None
                for j in range(lo, hi):
                    out = e32[j] if out is None else out * e32[j]
                return out

            def rescaled(x, facs):
                parts = [x[j * cs:(j + 1) * cs] if f is None else x[j * cs:(j + 1) * cs] * f
                         for j, f in enumerate(facs)]
                return jnp.concatenate(parts, axis=0).astype(BF16)

            def level(n):
                before = [span((j // n) * n, j) for j in range(nch)]
                after = [span(j + 1, (j // n + 1) * n) for j in range(nch)]
                qf, kf = (before, after) if d == 0 else (after, before)
                return rescaled(q32, qf), rescaled(k32, kf)

            q32b, k32b = q32.astype(BF16), k32.astype(BF16)
            q64, k64 = level(2)
            q128, k128 = level(4)
            q256, k256 = level(8)
            diag.append((_dot_nt(q32b, k_hat), _dot_nt(q32b, k32b), _dot_nt(q64, k64)))
            if d == 0:
                quad.append(_dot_nt(q128[half:], k128[:half]))
            else:
                quad.append(_dot_nt(q128[:half], k128[half:]))
            qe_ref[d, rows, :] = q256
            ke_ref[d, rows, :] = k256
            ee_ref[d, blk] = jnp.broadcast_to(span(0, nch), (8, DK_C))

        pieces = []
        for j in range(nch):
            rs = slice(j * cs, (j + 1) * cs)
            own = j * cs // half
            cols = slice(own * half, (own + 1) * half)
            acc = None
            for d in range(2):
                for i in range(3):
                    term = diag[d][i][rs, cols] * mask_ref[d, i, rs, :]
                    acc = term if acc is None else acc + term
            other = quad[0][rs.start - half:rs.stop - half] if own == 1 else quad[1][rs]
            pieces.append(jnp.concatenate([other, acc] if own == 1 else [acc, other], axis=1))
        att = jnp.concatenate(pieces, axis=0)
        oi_ref[rows, :] = _dot(att.astype(BF16), v)

        unsafe_ref[blk] = jnp.logical_not(safe).astype(jnp.int32)
        sq_ref[rows, :] = qs
        for d in range(2):
            sk_ref[d, rows, :] = per_dir[d][0]
            sc_ref[d, rows, :] = per_dir[d][1]
        return carry

    n_blk = GROUP_ROWS // sb
    lax.fori_loop(0, n_blk, block_step, 0, unroll=True)

    def vpu_block(blk, carry):
        @pl.when(unsafe_ref[blk] != 0)
        def _():
            def chunk_step(i, carry2):
                crow = pl.ds(pl.multiple_of(blk * sb + i * cs, cs), cs)
                q, vv = sq_ref[crow, :], i_ref[crow, :].astype(F32)
                oi_ref[crow, :] += (_hgrn_intra(q, sk_ref[0, crow, :], vv, sc_ref[0, crow, :], reverse=False)
                                    + _hgrn_intra(q, sk_ref[1, crow, :], vv, sc_ref[1, crow, :], reverse=True))
                return carry2

            lax.fori_loop(0, sb // cs, chunk_step, 0)

        return carry

    lax.fori_loop(0, n_blk, vpu_block, 0)

    is_ctx = pl.program_id(1) < n_ctx
    pl.when(is_ctx)(functools.partial(
        _hgrn_states, qe_ref, ke_ref, i_ref, ee_ref, oi_ref, t_ctx, st_ref=st_ref))
    pl.when(jnp.logical_not(is_ctx))(functools.partial(
        _hgrn_states, qe_ref, ke_ref, i_ref, ee_ref, oi_ref, t_lat, s0_ref=s0_ref))

    o = oi_ref[...]
    on = o * lax.rsqrt(jnp.mean(o * o, axis=-1, keepdims=True) + EPS) * gn_ref[...]
    o_ref[...] = (on * _silu(g_ref[...].astype(F32))).astype(o_ref.dtype)


def _hgrn(z16, z32, lower, gnorm, state_hgrn, oi, n_layers, prev_st, n_ctx_req, t_ctx, t_lat, n_rows):
    nh = N_HEADS_C
    n_ctx = n_ctx_req * t_ctx // GROUP_ROWS
    n_grp = n_rows // GROUP_ROWS
    rows = GROUP_ROWS
    assert t_lat == GROUP_ROWS and t_ctx % HGRN_SUPER == 0
    hs = HGRN_HEADS
    width = hs * DK_C
    assert nh % hs == 0
    blk = lambda p: pl.BlockSpec((rows, width), lambda h, g, p=p: (g, p * nh // hs + h))
    st_spec, st_shape = _state_spec(n_ctx_req, n_layers, oi, (2, nh, DK_C, DK_C),
                                    (2, hs, DK_C, DK_C), lambda h: (0, h, 0, 0), n_ctx, GROUP_ROWS // t_ctx)
    in_specs = [blk(0), blk(1), blk(2), blk(0), blk(1),
                pl.BlockSpec((2, width), lambda h, g: (0, h)),
                pl.BlockSpec((1, DK_C), lambda h, g: (0, 0)),
                pl.BlockSpec((None, None, 2, hs, DK_C, DK_C),
                             lambda h, g: (jnp.maximum(g - n_ctx, 0), oi, 0, h, 0, 0))]
    args = [z16, z16, z16, z32, z32, lower, gnorm, state_hgrn]
    aliases = {}
    if prev_st is not None:
        in_specs.append(pl.BlockSpec(memory_space=pl.ANY))
        args.append(prev_st)
        aliases = {len(args) - 1: 1}
    return pl.pallas_call(
        functools.partial(_hgrn_kernel, t_ctx=t_ctx, t_lat=t_lat, n_ctx=n_ctx),
        grid=(nh // hs, n_grp),
        in_specs=in_specs,
        out_specs=[pl.BlockSpec((rows, width), lambda h, g: (g, h)), st_spec],
        out_shape=[jax.ShapeDtypeStruct((n_rows, D_MODEL), BF16), st_shape],
        input_output_aliases=aliases,
        scratch_shapes=[
            pltpu.VMEM((2, rows, DK_C), BF16),
            pltpu.VMEM((2, rows, DK_C), BF16),
            pltpu.VMEM((2, rows // HGRN_SUPER, 8, DK_C), F32),
            pltpu.VMEM((rows, DK_C), F32),
            pltpu.VMEM((rows, DK_C), F32),
            pltpu.VMEM((2, rows, DK_C), F32),
            pltpu.VMEM((2, rows, DK_C), F32),
            pltpu.VMEM((2, 3, HGRN_SUPER, HGRN_SUPER // 2), F32),
            pltpu.SMEM((rows // HGRN_SUPER,), jnp.int32),
        ],
        compiler_params=_params(("arbitrary", "arbitrary"), 56),
        name="hgrn",
    )(*args)


def kernel(x_prompt, x_sample, cache_kv, state_ret, state_hgrn, c, c_ctx, w_mod, b_mod, norm_g,
           w_in_even, w_out_even, rpb, ret_decay, w_in_odd, w_out_odd, hgrn_lb, hgrn_gnorm,
           w_ffn_in, w_ffn_out):
    bp, tp, _ = x_prompt.shape
    bs, ts, _ = x_sample.shape
    np_rows, ns_rows = bp * tp, bs * ts
    n_rows = np_rows + ns_rows
    n_even, n_odd = w_in_even.shape[0], w_in_odd.shape[0]
    assert np_rows % GROUP_ROWS == 0 and ts == GROUP_ROWS
    xs = [x_prompt.reshape(np_rows, D_MODEL), x_sample.reshape(ns_rows, D_MODEL)]

    n_c = bs + 1
    pad = (-n_c) % 8
    cvec = jnp.concatenate([c, c_ctx[None], jnp.zeros((pad, D_MODEL), F32)], axis=0)
    mod_all = _modulation(cvec, w_mod, b_mod)
    grp = np.concatenate([np.full(np_rows // GROUP_ROWS, bs), np.arange(bs)]).astype(np.int32)
    mod_all = mod_all[:, grp].reshape(DEPTH, len(grp), 6, D_MODEL)
    mod_all = jnp.pad(mod_all, ((0, 0), (0, 0), (0, MOD_ROWS - 6), (0, 0)))

    p_lb = jax.nn.softmax(hgrn_lb.astype(F32), axis=0)
    lower = jnp.clip(jnp.cumsum(p_lb, axis=0) - p_lb[0], 0.0, 1.0)
    pair_bias = _na_bias_tables(rpb)
    rope_tabs = _rope_tables(ts)

    w_in_even_b, w_out_even_b = w_in_even.astype(BF16), w_out_even.astype(BF16)
    w_out_odd_b = w_out_odd.astype(BF16)
    w_in_odd_q = w_in_odd[:, :, :D_C].astype(BF16)
    w_in_odd_f = w_in_odd[:, :, D_C:3 * D_C].astype(BF16)
    w_in_odd_ig = w_in_odd[:, :, 3 * D_C:].astype(BF16)
    ffn_w = [_ffn_weights(w_ffn_in[l], w_ffn_out[l]) for l in range(DEPTH)]

    kv_out = ret_out = hg_out = None
    y_split = None
    for l in range(DEPTH):
        mod = mod_all[l]
        g = norm_g[l].reshape(4, 1, D_MODEL)
        if l % 2 == 0:
            e = l // 2
            z, = _in_proj(xs, g[0], mod, [([w_in_even_b[e]], BF16)])
            oa, kv_out = _attention(z, cache_kv, pair_bias, e, n_even, kv_out, bp, tp, n_rows)
            ob, ret_out = _retention(z, ret_decay[e], rope_tabs, state_ret, e, n_even, ret_out, bp, tp, ts, n_rows)
            x = _out_proj([oa, ob], w_out_even_b[e], xs, g[1], mod)
        else:
            oi = l // 2
            z16, z32 = _in_proj(xs, g[0], mod, [([w_in_odd_q[oi], w_in_odd_ig[oi]], BF16),
                                                ([w_in_odd_f[oi]], F32)])
            gn = hgrn_gnorm[oi].reshape(1, DK_C)
            o, hg_out = _hgrn(z16, z32, lower[oi], gn, state_hgrn, oi, n_odd, hg_out, bp, tp, ts, n_rows)
            x = _out_proj([o], w_out_odd_b[oi], xs, g[1], mod)
        if l == DEPTH - 1:
            y_split = _ffn(x, g[2], g[3], mod, *ffn_w[l], split=(x_prompt.shape, x_sample.shape))
        else:
            xs = [_ffn(x, g[2], g[3], mod, *ffn_w[l])]

    return (y_split[0], y_split[1], kv_out, ret_out, hg_out)
```
